```python
import jax, jax.numpy as jnp
from jax import lax
import numpy as np

D_MODEL = 2048
BATCH = 4
SEQ = 4096
DEPTH = 4

GRID_W = 64
CTX_LEN = 256
EPS = 1e-6
N_MOD = 6

LRU_WIDTH = 1024
LRU_BLOCKS = 8
LRU_BLOCK = LRU_WIDTH // LRU_BLOCKS
LRU_C = 8.0
CONV_W = 4
ATT_HEADS = 8
KV_HEADS = 2
GROUP = ATT_HEADS // KV_HEADS
HEAD_DIM = 128
ATT_WIDTH = ATT_HEADS * HEAD_DIM
KV_WIDTH = KV_HEADS * HEAD_DIM
WINDOW = 128
BLOCK_Q = 128
ROPE_PAIRS = HEAD_DIM // 4
ROPE_BASE = 10000.0
EVEN_IN = 2 * LRU_WIDTH + ATT_WIDTH + 2 * KV_WIDTH
EVEN_SPLITS = (LRU_WIDTH, 2 * LRU_WIDTH, 2 * LRU_WIDTH + ATT_WIDTH, 2 * LRU_WIDTH + ATT_WIDTH + KV_WIDTH)
EVEN_OUT = LRU_WIDTH + ATT_WIDTH

M_HEADS = 8
M_DK = 128
M_DV = 256
M_QK = M_HEADS * M_DK
M_V = M_HEADS * M_DV
M_CHUNK = 64
ODD_IN = 2 * M_QK + 2 * M_V + 4 * M_HEADS
ODD_SPLITS = (2 * M_QK, 2 * M_QK + M_V, 2 * M_QK + 2 * M_V)

N_EXPERTS = 16
EC_FACTOR = 2
D_EXPERT = 1536

kernel_name = 'hybrid_rglru_swa_mlstm_ecmoe_dit'


def rmsnorm(x, w):
    xf = x.astype(jnp.float32)
    y = xf * lax.rsqrt(jnp.mean(xf * xf, axis=-1, keepdims=True) + EPS)
    return (y * w.astype(jnp.float32)).astype(x.dtype)


def modulate(h, shift, scale):
    return h * (1 + scale) + shift


def centred_dwconv(x, w, b):
    n = x.shape[1]
    left = CONV_W // 2
    xp = jnp.pad(x, ((0, 0), (left, CONV_W - 1 - left), (0, 0)))
    y = b
    for j in range(CONV_W):
        y = y + xp[:, j:j + n] * w[j]
    return y


def axial_angles(n):
    rows = n // GRID_W
    inv = jnp.power(ROPE_BASE, -jnp.arange(ROPE_PAIRS, dtype=jnp.float32) / ROPE_PAIRS)
    row = jnp.repeat(jnp.arange(rows, dtype=jnp.float32), GRID_W)
    col = (jnp.arange(rows * GRID_W) % GRID_W).astype(jnp.float32)
    return row[:, None] * inv, col[:, None] * inv


def rope_axis(x, ang):
    cos = jnp.cos(ang)[:, None, :].astype(x.dtype)
    sin = jnp.sin(ang)[:, None, :].astype(x.dtype)
    x1, x2 = jnp.split(x, 2, axis=-1)
    return jnp.concatenate([x1 * cos - x2 * sin, x2 * cos + x1 * sin], axis=-1)


def rope_2d(x, row_ang, col_ang):
    xr, xc = jnp.split(x, 2, axis=-1)
    return jnp.concatenate([rope_axis(xr, row_ang), rope_axis(xc, col_ang)], axis=-1)


def softmax_with_sink(logits, sink):
    s = jnp.broadcast_to(sink.reshape(KV_HEADS, GROUP, 1, 1).astype(jnp.float32), logits.shape[:-1] + (1,))
    return jax.nn.softmax(jnp.concatenate([s, logits], axis=-1), axis=-1)[..., 1:]


def banded_window_attention(q, k, v, ck, cv, sink):
    dt = q.dtype
    b_, n = q.shape[:2]
    m = ck.shape[1]
    nb = n // BLOCK_Q
    qb = q.reshape(b_, nb, BLOCK_Q, KV_HEADS, GROUP, HEAD_DIM)

    def band(t):
        tp = jnp.pad(t, ((0, 0), (BLOCK_Q, BLOCK_Q), (0, 0), (0, 0)))
        return jnp.concatenate([tp[:, j * BLOCK_Q:j * BLOCK_Q + n].reshape(b_, nb, BLOCK_Q, KV_HEADS, HEAD_DIM)
                                for j in range(3)], axis=2)

    kb, vb = band(k), band(v)
    scale = HEAD_DIM ** -0.5
    s_band = jnp.einsum('bnqkgd,bnjkd->bnkgqj', qb, kb).astype(jnp.float32) * scale
    s_ctx = jnp.einsum('bnqkgd,bmkd->bnkgqm', qb, ck).astype(jnp.float32) * scale
    qpos = jnp.arange(n).reshape(nb, BLOCK_Q)
    kpos = (jnp.arange(nb) * BLOCK_Q)[:, None] - BLOCK_Q + jnp.arange(3 * BLOCK_Q)[None, :]
    valid = ((jnp.abs(kpos[:, None, :] - qpos[:, :, None]) <= WINDOW)
             & (kpos >= 0)[:, None, :] & (kpos < n)[:, None, :])
    s_band = jnp.where(valid[None, :, None, None], s_band, -jnp.inf)
    p = softmax_with_sink(jnp.concatenate([s_ctx, s_band], axis=-1), sink).astype(dt)
    out = (jnp.einsum('bnkgqm,bmkd->bnqkgd', p[..., :m], cv)
           + jnp.einsum('bnkgqj,bnjkd->bnqkgd', p[..., m:], vb))
    return out.reshape(b_, n, ATT_WIDTH)


def context_attention(q, k, v, sink):
    dt = q.dtype
    b_, m = q.shape[:2]
    qg = q.reshape(b_, m, KV_HEADS, GROUP, HEAD_DIM)
    s = jnp.einsum('bmkgd,bjkd->bkgmj', qg, k).astype(jnp.float32) * HEAD_DIM ** -0.5
    p = softmax_with_sink(s, sink).astype(dt)
    return jnp.einsum('bkgmj,bjkd->bmkgd', p, v).reshape(b_, m, ATT_WIDTH)


def linear_scan(a, b, h0):
    b = b.at[:, 0].add(a[:, 0] * h0)

    def combine(left, right):
        al, bl = left
        ar, br = right
        return al * ar, ar * bl + br

    _, h = lax.associative_scan(combine, (a, b), axis=1)
    return h


def rglru_scan(x, ra_w, ra_b, ix_w, ix_b, lam, h0):
    b_, n, _ = x.shape
    xh = x.reshape(b_, n, LRU_BLOCKS, LRU_BLOCK)
    r = jax.nn.sigmoid(jnp.einsum('bnhi,hij->bnhj', xh, ra_w) + ra_b).reshape(b_, n, LRU_WIDTH)
    i = jax.nn.sigmoid(jnp.einsum('bnhi,hij->bnhj', xh, ix_w) + ix_b).reshape(b_, n, LRU_WIDTH)
    log_a = -LRU_C * r * jax.nn.softplus(-lam.astype(jnp.float32))
    a = jnp.exp(log_a)
    inp = jnp.sqrt(-jnp.expm1(2.0 * log_a)) * (i * x)
    return linear_scan(a, inp, h0)


def even_mixer(hc, hl, w_in, conv_w, conv_b, ra_w, ra_b, ix_w, ix_b, lam, sink, w_out,
               row_ang, col_ang, with_ctx_out):
    dt = hl.dtype

    def project(h):
        b_, n = h.shape[:2]
        xa, ya, q, k, v = jnp.split(h @ w_in, EVEN_SPLITS, axis=-1)
        xa = centred_dwconv(xa, conv_w, conv_b).astype(jnp.float32)
        return (xa, ya, q.reshape(b_, n, ATT_HEADS, HEAD_DIM),
                k.reshape(b_, n, KV_HEADS, HEAD_DIM), v.reshape(b_, n, KV_HEADS, HEAD_DIM))

    cxa, cya, cq, ck, cv = project(hc)
    lxa, lya, lq, lk, lv = project(hl)
    zero = jnp.zeros((hl.shape[0], LRU_WIDTH), jnp.float32)

    def lru(xa, d, h0):
        return rglru_scan(xa, ra_w[d], ra_b[d], ix_w[d], ix_b[d], lam[d], h0)

    hf_c = lru(cxa, 0, zero)
    hf_l = lru(lxa, 0, hf_c[:, -1])
    hb_c = lru(cxa[:, ::-1], 1, zero)
    hb_l = lru(lxa[:, ::-1], 1, hb_c[:, -1])[:, ::-1]
    lat_a = ((hf_l + hb_l) * jax.nn.gelu(lya)).astype(dt)
    lat_b = banded_window_attention(rope_2d(lq, row_ang, col_ang), rope_2d(lk, row_ang, col_ang),
                                    lv, ck, cv, sink)
    lat = jnp.concatenate([lat_a, lat_b], axis=-1) @ w_out
    if not with_ctx_out:
        return None, lat
    ctx_a = ((hf_c + hb_c[:, ::-1]) * jax.nn.gelu(cya)).astype(dt)
    ctx_b = context_attention(cq, ck, cv, sink)
    return jnp.concatenate([ctx_a, ctx_b], axis=-1) @ w_out, lat


def mlstm_state_update(state, k, v, li, b):
    c0, n0, m0 = state
    b_last = b[..., -1]
    log_u = b_last[..., None] - b + li
    m_new = jnp.maximum(b_last + m0, jnp.max(log_u, axis=-1))
    u = jnp.exp(log_u - m_new[..., None])
    decay = jnp.exp(b_last + m0 - m_new)
    uk = u[..., None] * k
    c_new = decay[..., None, None] * c0 + jnp.einsum('bhsd,bhsv->bhdv', uk, v)
    n_new = decay[..., None] * n0 + jnp.sum(uk, axis=2)
    return (c_new, n_new, m_new)


def mlstm_chunk_step(state, xs):
    c0, n0, m0 = state
    q, k, v, li, lf = xs
    b = jnp.cumsum(lf, axis=-1)
    length = q.shape[2]
    causal = jnp.tril(jnp.ones((length, length), dtype=bool))
    log_w = jnp.where(causal, b[..., :, None] - b[..., None, :] + li[..., None, :], -jnp.inf)
    log_inter = b + m0[..., None]
    m = jnp.maximum(log_inter, jnp.max(log_w, axis=-1))
    w = jnp.exp(log_w - m[..., None])
    inter = jnp.exp(log_inter - m)
    s = jnp.einsum('bhtd,bhsd->bhts', q, k) * w
    num = jnp.einsum('bhts,bhsv->bhtv', s, v) + inter[..., None] * jnp.einsum('bhtd,bhdv->bhtv', q, c0)
    den = jnp.sum(s, axis=-1) + inter * jnp.einsum('bhtd,bhd->bht', q, n0)
    h = num / jnp.maximum(jnp.abs(den), jnp.exp(-m))[..., None]
    return mlstm_state_update(state, k, v, li, b), h


def mlstm_chunkwise(state0, q, k, v, li, lf):
    b_, nh, n, _ = q.shape
    nc = n // M_CHUNK

    def chunks(t):
        return jnp.moveaxis(t.reshape((b_, nh, nc, M_CHUNK) + t.shape[3:]), 2, 0)

    state, h = lax.scan(mlstm_chunk_step, state0, (chunks(q), chunks(k), chunks(v), chunks(li), chunks(lf)))
    return state, jnp.moveaxis(h, 0, 2).reshape(b_, nh, n, M_DV)


def mlstm_direction(c_in, l_in, d, with_ctx_out):
    def orient(t):
        return jnp.flip(t, axis=2) if d == 1 else t

    def select(inp):
        q, k, v, li, lf = inp
        return tuple(orient(t) for t in (q, k, v, li[:, d], lf[:, d]))

    cq, ck, cv, cli, clf = select(c_in)
    lq, lk, lv, lli, llf = select(l_in)
    b_ = cq.shape[0]
    state0 = (jnp.zeros((b_, M_HEADS, M_DK, M_DV), jnp.float32),
              jnp.zeros((b_, M_HEADS, M_DK), jnp.float32),
              jnp.zeros((b_, M_HEADS), jnp.float32))
    if with_ctx_out:
        state_c, hc = mlstm_chunkwise(state0, cq, ck, cv, cli, clf)
        hc = orient(hc)
    else:
        state_c = mlstm_state_update(state0, ck, cv, cli, jnp.cumsum(clf, axis=-1))
        hc = None
    _, hl = mlstm_chunkwise(state_c, lq, lk, lv, lli, llf)
    return hc, orient(hl)


def mlstm_output(h, o, hnorm_w, w_out):
    b_, _, n, _ = h.shape
    h = h.transpose(0, 2, 1, 3)
    h = h * lax.rsqrt(jnp.mean(h * h, axis=-1, keepdims=True) + EPS)
    h = h.reshape(b_, n, M_V) * hnorm_w.astype(jnp.float32) * jax.nn.sigmoid(o.astype(jnp.float32))
    return h.astype(o.dtype) @ w_out


def odd_mixer(hc, hl, w_in, conv_w, conv_b, gate_b, hnorm_w, w_out, with_ctx_out):
    def project(h):
        b_, n = h.shape[:2]
        qk, v, o, g = jnp.split(h @ w_in, ODD_SPLITS, axis=-1)
        q, k = jnp.split(jax.nn.silu(centred_dwconv(qk, conv_w, conv_b)), 2, axis=-1)

        def heads(t, dd):
            return t.reshape(b_, n, M_HEADS, dd).transpose(0, 2, 1, 3).astype(jnp.float32)

        g = (g.reshape(b_, n, 4, M_HEADS).astype(jnp.float32) + gate_b).transpose(0, 2, 3, 1)
        return (heads(q, M_DK) * M_DK ** -0.5, heads(k, M_DK), heads(v, M_DV),
                g[:, 0::2], jax.nn.log_sigmoid(g[:, 1::2])), o

    c_in, co = project(hc)
    l_in, lo = project(hl)
    hf_c, hf_l = mlstm_direction(c_in, l_in, 0, with_ctx_out)
    hb_c, hb_l = mlstm_direction(c_in, l_in, 1, with_ctx_out)
    lat = mlstm_output(hf_l + hb_l, lo, hnorm_w, w_out)
    if not with_ctx_out:
        return None, lat
    return mlstm_output(hf_c + hb_c, co, hnorm_w, w_out), lat


def expert_choice_ffn(h, w_router, w_gate, w_up, w_down):
    b_, n, d = h.shape
    cap = EC_FACTOR * n // N_EXPERTS
    aff = jax.nn.softmax((h @ w_router).astype(jnp.float32), axis=-1)
    g, idx = lax.top_k(jnp.swapaxes(aff, 1, 2), cap)
    xs = jax.vmap(lambda hb, ib: hb[ib])(h, idx)
    hid = jax.nn.silu(jnp.einsum('becd,edf->becf', xs, w_gate)) * jnp.einsum('becd,edf->becf', xs, w_up)
    y = jnp.einsum('becf,efd->becd', hid, w_down) * g[..., None].astype(h.dtype)
    return jax.vmap(lambda yb, ib: jnp.zeros((n, d), h.dtype).at[ib.reshape(-1)].add(yb.reshape(-1, d)))(y, idx)


def setup_inputs(seed: int = 0) -> dict:
    key = jax.random.key(seed)
    keys = jax.random.split(key, 32)
    D = D_MODEL
    ne = (DEPTH + 1) // 2
    no = DEPTH // 2

    def nrm(i, shape, scale):
        return jax.random.normal(keys[i], shape, jnp.float32) * scale

    lam_u = jax.random.uniform(keys[15], (ne, 2, LRU_WIDTH), jnp.float32, 0.9, 0.999)
    lam_a = lam_u ** (1.0 / LRU_C)
    gate_i = nrm(21, (no, 2, M_HEADS), 0.1)
    gate_f = jax.random.uniform(keys[22], (no, 2, M_HEADS), jnp.float32, 3.0, 6.0)
    return {
        'x': nrm(0, (BATCH, SEQ, D), 1.0),
        'c': nrm(1, (BATCH, D), 1.0),
        'ctx': nrm(2, (BATCH, CTX_LEN, D), 1.0),
        'c_ctx': nrm(3, (D,), 1.0),
        'ada_w': nrm(4, (DEPTH, D, N_MOD * D), 0.5 * D ** -0.5),
        'ada_b': nrm(5, (DEPTH, N_MOD * D), 0.02),
        'norm_mix_w': 1.0 + nrm(6, (DEPTH, D), 0.02),
        'norm_ffn_w': 1.0 + nrm(7, (DEPTH, D), 0.02),
        'ev_w_in': nrm(8, (ne, D, EVEN_IN), D ** -0.5),
        'ev_conv_w': nrm(9, (ne, CONV_W, LRU_WIDTH), CONV_W ** -0.5),
        'ev_conv_b': nrm(10, (ne, LRU_WIDTH), 0.02),
        'ev_ra_w': nrm(11, (ne, 2, LRU_BLOCKS, LRU_BLOCK, LRU_BLOCK), LRU_BLOCK ** -0.5),
        'ev_ra_b': nrm(12, (ne, 2, LRU_BLOCKS, LRU_BLOCK), 0.1),
        'ev_ix_w': nrm(13, (ne, 2, LRU_BLOCKS, LRU_BLOCK, LRU_BLOCK), LRU_BLOCK ** -0.5),
        'ev_ix_b': nrm(14, (ne, 2, LRU_BLOCKS, LRU_BLOCK), 0.1),
        'ev_lambda': jnp.log(lam_a) - jnp.log1p(-lam_a),
        'ev_sink': nrm(16, (ne, ATT_HEADS), 0.5),
        'ev_w_out': nrm(17, (ne, EVEN_OUT, D), EVEN_OUT ** -0.5),
        'od_w_in': nrm(18, (no, D, ODD_IN), D ** -0.5),
        'od_conv_w': nrm(19, (no, CONV_W, 2 * M_QK), CONV_W ** -0.5),
        'od_conv_b': nrm(20, (no, 2 * M_QK), 0.02),
        'od_gate_b': jnp.stack([gate_i, gate_f], axis=2).reshape(no, 4, M_HEADS),
        'od_hnorm_w': 1.0 + nrm(23, (no, M_V), 0.02),
        'od_w_out': nrm(24, (no, M_V, D), M_V ** -0.5),
        'moe_router': nrm(25, (DEPTH, D, N_EXPERTS), D ** -0.5),
        'moe_w_gate': nrm(26, (DEPTH, N_EXPERTS, D, D_EXPERT), D ** -0.5),
        'moe_w_up': nrm(27, (DEPTH, N_EXPERTS, D, D_EXPERT), D ** -0.5),
        'moe_w_down': nrm(28, (DEPTH, N_EXPERTS, D_EXPERT, D), D_EXPERT ** -0.5),
        'final_norm_w': 1.0 + nrm(29, (D,), 0.02),
    }


def reference(x, c, ctx, c_ctx, ada_w, ada_b, norm_mix_w, norm_ffn_w,
              ev_w_in, ev_conv_w, ev_conv_b, ev_ra_w, ev_ra_b, ev_ix_w, ev_ix_b, ev_lambda, ev_sink, ev_w_out,
              od_w_in, od_conv_w, od_conv_b, od_gate_b, od_hnorm_w, od_w_out,
              moe_router, moe_w_gate, moe_w_up, moe_w_down, final_norm_w):
    n = x.shape[1]
    row_ang, col_ang = axial_angles(n)
    xl, xc = x, ctx
    silu_c = jax.nn.silu(c)
    silu_cc = jax.nn.silu(c_ctx)
    for layer in range(DEPTH):
        last = layer == DEPTH - 1
        mod_l = jnp.split((silu_c @ ada_w[layer] + ada_b[layer])[:, None, :], N_MOD, axis=-1)
        mod_c = jnp.split(silu_cc @ ada_w[layer] + ada_b[layer], N_MOD, axis=-1)
        hl = modulate(rmsnorm(xl, norm_mix_w[layer]), mod_l[0], mod_l[1])
        hc = modulate(rmsnorm(xc, norm_mix_w[layer]), mod_c[0], mod_c[1])
        if layer % 2 == 0:
            e = layer // 2
            oc, ol = even_mixer(hc, hl, ev_w_in[e], ev_conv_w[e], ev_conv_b[e], ev_ra_w[e], ev_ra_b[e],
                                ev_ix_w[e], ev_ix_b[e], ev_lambda[e], ev_sink[e], ev_w_out[e],
                                row_ang, col_ang, not last)
        else:
            o = layer // 2
            oc, ol = odd_mixer(hc, hl, od_w_in[o], od_conv_w[o], od_conv_b[o], od_gate_b[o],
                               od_hnorm_w[o], od_w_out[o], not last)
        xl = xl + mod_l[2] * ol
        xl = xl + mod_l[5] * expert_choice_ffn(
            modulate(rmsnorm(xl, norm_ffn_w[layer]), mod_l[3], mod_l[4]),
            moe_router[layer], moe_w_gate[layer], moe_w_up[layer], moe_w_down[layer])
        if not last:
            xc = xc + mod_c[2] * oc
            xc = xc + mod_c[5] * expert_choice_ffn(
                modulate(rmsnorm(xc, norm_ffn_w[layer]), mod_c[3], mod_c[4]),
                moe_router[layer], moe_w_gate[layer], moe_w_up[layer], moe_w_down[layer])
    return rmsnorm(xl, final_norm_w)
```

```python
import functools

import jax
import jax.numpy as jnp
from jax import lax
from jax.experimental import pallas as pl
from jax.experimental.pallas import tpu as pltpu

F32 = jnp.float32
BF16 = jnp.bfloat16
I32 = jnp.int32

D = 2048
BATCH = 4
SEQ = 4096
CTX = 256
S = CTX + SEQ
DEPTH = 4
N_MOD = 6
EPS = 1e-6
GRID_W = 64

LRU_W = 1024
LRU_BLOCKS = 8
LRU_BLOCK = 128
LRU_C = 8.0
CONV_W = 4
ATT_HEADS = 8
KV_HEADS = 2
GROUP = ATT_HEADS // KV_HEADS
HEAD_DIM = 128
ATT_W = ATT_HEADS * HEAD_DIM
KV_W = KV_HEADS * HEAD_DIM
WINDOW = 128
ROPE_PAIRS = HEAD_DIM // 4
ROPE_BASE = 10000.0
EVEN_IN = 2 * LRU_W + ATT_W + 2 * KV_W

M_HEADS = 8
M_DK = 128
M_DV = 256
M_QK = M_HEADS * M_DK
M_V = M_HEADS * M_DV
ODD_MAIN = 2 * M_QK + 2 * M_V
N_GATES = 4 * M_HEADS

N_EXPERTS = 16
EC_FACTOR = 2
D_EXPERT = 1536
CAP_LAT = EC_FACTOR * SEQ // N_EXPERTS
CAP_CTX = EC_FACTOR * CTX // N_EXPERTS
RPS = CAP_CTX + CAP_LAT
R_EXP = BATCH * RPS
PAIRS = N_EXPERTS * RPS

LANE = 128
SUBLANE = 8
VMEM_LIMIT = 56 * 1024 * 1024
TM = S // 4
TILES_PER_SAMPLE = S // TM
TN = 512
TT = 256
N_TT = S // TT
CHUNK = 128
N_CHUNK = S // CHUNK
CTX_CHUNKS = CTX // CHUNK


def _cparams(sem, vmem=VMEM_LIMIT):
    return pltpu.CompilerParams(dimension_semantics=sem, vmem_limit_bytes=vmem)


def _sigmoid(x):
    return 1.0 / (1.0 + jnp.exp(-x))


def _silu(x):
    return x * _sigmoid(x)


def _softplus(x):
    return jnp.maximum(x, 0.0) + jnp.log1p(jnp.exp(-jnp.abs(x)))


def _log_sigmoid(x):
    return -_softplus(-x)


def _gelu_tanh(x):
    return 0.5 * x * (1.0 + jnp.tanh(0.7978845608028654 * (x + 0.044715 * (x * x * x))))


def _rms(x):
    return x * lax.rsqrt(jnp.mean(x * x, axis=-1, keepdims=True) + EPS)


def _ctx_select(is_ctx, tab):
    return jnp.where(is_ctx, tab[0:1, :], tab[1:2, :])


def _mod_kernel(c_ref, w_ref, b_ref, o_ref):
    a = _silu(c_ref[...]).astype(BF16)
    o_ref[0] = jnp.dot(a, w_ref[0].astype(BF16), preferred_element_type=F32) + b_ref[0]


def _modulation(cc, ada_w, ada_b):
    tn = 1024
    return pl.pallas_call(
        _mod_kernel,
        grid=(DEPTH, N_MOD * D // tn),
        in_specs=[
            pl.BlockSpec((SUBLANE, D), lambda l, j: (0, 0)),
            pl.BlockSpec((1, D, tn), lambda l, j: (l, 0, j)),
            pl.BlockSpec((1, 1, tn), lambda l, j: (l, 0, j)),
        ],
        out_specs=pl.BlockSpec((1, SUBLANE, tn), lambda l, j: (l, 0, j)),
        out_shape=jax.ShapeDtypeStruct((DEPTH, SUBLANE, N_MOD * D), F32),
        compiler_params=_cparams(("arbitrary", "arbitrary")),
        name="adaln_mod",
    )(cc, ada_w, ada_b.reshape(DEPTH, 1, N_MOD * D))


def _mod_table(mod_layer):
    m = mod_layer.reshape(SUBLANE, N_MOD, D)
    lat = jnp.transpose(m[:BATCH], (1, 0, 2))
    ctx = jnp.broadcast_to(m[BATCH][:, None, :], (N_MOD, BATCH, D))
    return jnp.stack([ctx, lat], axis=2)


def _norm_mod_rows(x, nw, sh_tab, sc_tab, row0):
    rows = x.shape[0]
    is_ctx = (row0 + lax.broadcasted_iota(I32, (rows, 1), 0)) < CTX
    y = _rms(x) * nw
    return y * (1.0 + _ctx_select(is_ctx, sc_tab)) + _ctx_select(is_ctx, sh_tab)


def _in_proj_kernel(x_ref, nw_ref, sh_ref, sc_ref, w_ref, o_ref, xn_ref):
    i = pl.program_id(0)
    j = pl.program_id(1)

    @pl.when(j == 0)
    def _():
        row0 = (i % TILES_PER_SAMPLE) * TM
        xn_ref[...] = _norm_mod_rows(x_ref[0], nw_ref[...], sh_ref[0, 0], sc_ref[0, 0], row0).astype(BF16)

    o_ref[0] = jnp.dot(xn_ref[...], w_ref[...].astype(BF16), preferred_element_type=F32)


def _in_proj(x, nw, mt, w, n_out):
    return pl.pallas_call(
        _in_proj_kernel,
        grid=(BATCH * TILES_PER_SAMPLE, n_out // TN),
        in_specs=[
            pl.BlockSpec((1, TM, D), lambda i, j: (i // TILES_PER_SAMPLE, i % TILES_PER_SAMPLE, 0)),
            pl.BlockSpec((1, D), lambda i, j: (0, 0)),
            pl.BlockSpec((1, 1, 2, D), lambda i, j: (0, i // TILES_PER_SAMPLE, 0, 0)),
            pl.BlockSpec((1, 1, 2, D), lambda i, j: (1, i // TILES_PER_SAMPLE, 0, 0)),
            pl.BlockSpec((D, TN), lambda i, j: (0, j)),
        ],
        out_specs=pl.BlockSpec((1, TM, TN), lambda i, j: (i // TILES_PER_SAMPLE, i % TILES_PER_SAMPLE, j)),
        out_shape=jax.ShapeDtypeStruct((BATCH, S, n_out), F32),
        scratch_shapes=[pltpu.VMEM((TM, D), BF16)],
        compiler_params=_cparams(("arbitrary", "arbitrary")),
        name="in_proj",
    )(x, nw.reshape(1, D), mt, mt, w)


def _gate_proj_kernel(x_ref, nw_ref, sh_ref, sc_ref, w_ref, o_ref):
    t = pl.program_id(1)
    xn = _norm_mod_rows(x_ref[0], nw_ref[...], sh_ref[0, 0], sc_ref[0, 0], t * TT).astype(BF16)
    o_ref[0] = jnp.dot(xn, w_ref[...].astype(BF16), preferred_element_type=F32)


def _gate_proj(x, nw, mt, wg):
    return pl.pallas_call(
        _gate_proj_kernel,
        grid=(BATCH, N_TT),
        in_specs=[
            pl.BlockSpec((1, TT, D), lambda b, t: (b, t, 0)),
            pl.BlockSpec((1, D), lambda b, t: (0, 0)),
            pl.BlockSpec((1, 1, 2, D), lambda b, t: (0, b, 0, 0)),
            pl.BlockSpec((1, 1, 2, D), lambda b, t: (1, b, 0, 0)),
            pl.BlockSpec((D, LANE), lambda b, t: (0, 0)),
        ],
        out_specs=pl.BlockSpec((1, TT, LANE), lambda b, t: (b, t, 0)),
        out_shape=jax.ShapeDtypeStruct((BATCH, S, LANE), F32),
        compiler_params=_cparams(("arbitrary", "arbitrary")),
        name="gate_proj",
    )(x, nw.reshape(1, D), mt, mt, wg)


def _residual_epilogue(i, x_ref, g_ref, acc, o_ref):
    tm = acc.shape[0]
    row0 = (i % (S // tm)) * tm
    is_ctx = (row0 + lax.broadcasted_iota(I32, (tm, 1), 0)) < CTX
    o_ref[0] = x_ref[0] + _ctx_select(is_ctx, g_ref[0, 0]) * acc


def _out_proj_even_kernel(a1_ref, a2_ref, x_ref, g_ref, w_ref, o_ref, a_ref):
    i = pl.program_id(0)
    j = pl.program_id(1)

    @pl.when(j == 0)
    def _():
        a_ref[:, :LRU_W] = a1_ref[0].astype(BF16)
        a_ref[:, LRU_W:] = a2_ref[0].astype(BF16)

    acc = jnp.dot(a_ref[...], w_ref[...].astype(BF16), preferred_element_type=F32)
    _residual_epilogue(i, x_ref, g_ref, acc, o_ref)


def _out_proj_odd_kernel(h_ref, og_ref, hw_ref, x_ref, g_ref, w_ref, o_ref, a_ref):
    i = pl.program_id(0)
    j = pl.program_id(1)

    @pl.when(j == 0)
    def _():
        for h in range(M_HEADS):
            sl = slice(h * M_DV, (h + 1) * M_DV)
            hn = _rms(h_ref[0, :, sl]) * hw_ref[:, sl] * _sigmoid(og_ref[0, :, sl])
            a_ref[:, sl] = hn.astype(BF16)

    acc = jnp.dot(a_ref[...], w_ref[...].astype(BF16), preferred_element_type=F32)
    _residual_epilogue(i, x_ref, g_ref, acc, o_ref)


def _row_spec(tm, width, col=None):
    tps = S // tm
    if col is None:
        return pl.BlockSpec((1, tm, width), lambda i, j: (i // tps, i % tps, j))
    return pl.BlockSpec((1, tm, width), lambda i, j: (i // tps, i % tps, col))


def _out_proj_even(a1, a2, x, mt, w):
    tm = TM
    return pl.pallas_call(
        _out_proj_even_kernel,
        grid=(BATCH * S // tm, D // TN),
        in_specs=[
            _row_spec(tm, LRU_W, 0),
            _row_spec(tm, ATT_W, 0),
            _row_spec(tm, TN),
            pl.BlockSpec((1, 1, 2, TN), lambda i, j: (2, i // (S // tm), 0, j)),
            pl.BlockSpec((D, TN), lambda i, j: (0, j)),
        ],
        out_specs=_row_spec(tm, TN),
        out_shape=jax.ShapeDtypeStruct((BATCH, S, D), F32),
        scratch_shapes=[pltpu.VMEM((tm, D), BF16)],
        compiler_params=_cparams(("arbitrary", "arbitrary")),
        name="out_proj_even",
    )(a1, a2, x, mt, w)


def _out_proj_odd(h, p_odd, hnorm_w, x, mt, w):
    tm = TM // 2
    return pl.pallas_call(
        _out_proj_odd_kernel,
        grid=(BATCH * S // tm, D // TN),
        in_specs=[
            _row_spec(tm, M_V, 0),
            _row_spec(tm, M_V, (2 * M_QK + M_V) // M_V),
            pl.BlockSpec((1, M_V), lambda i, j: (0, 0)),
            _row_spec(tm, TN),
            pl.BlockSpec((1, 1, 2, TN), lambda i, j: (2, i // (S // tm), 0, j)),
            pl.BlockSpec((D, TN), lambda i, j: (0, j)),
        ],
        out_specs=_row_spec(tm, TN),
        out_shape=jax.ShapeDtypeStruct((BATCH, S, D), F32),
        scratch_shapes=[pltpu.VMEM((tm, D), BF16)],
        compiler_params=_cparams(("arbitrary", "arbitrary")),
        name="out_proj_odd",
    )(h, p_odd, hnorm_w.reshape(1, M_V), x, mt, w)


def _final_norm_kernel(x_ref, w_ref, o_ref):
    o_ref[0] = _rms(x_ref[0]) * w_ref[...]


def _final_norm(x, w):
    return pl.pallas_call(
        _final_norm_kernel,
        grid=(BATCH, SEQ // TT),
        in_specs=[
            pl.BlockSpec((1, TT, D), lambda b, t: (b, t + CTX // TT, 0)),
            pl.BlockSpec((1, D), lambda b, t: (0, 0)),
        ],
        out_specs=pl.BlockSpec((1, TT, D), lambda b, t: (b, t, 0)),
        out_shape=jax.ShapeDtypeStruct((BATCH, SEQ, D), F32),
        compiler_params=_cparams(("arbitrary", "arbitrary")),
        name="final_norm",
    )(x, w.reshape(1, D))


def _seg_conv(x, cw, cb):
    n = x.shape[0]
    row = lax.broadcasted_iota(I32, (n, 1), 0)
    seg = row < CTX
    y = cb
    for j in range(CONV_W):
        off = j - CONV_W // 2
        if off == 0:
            tap = x
        else:
            src = row + off
            ok = (src >= 0) & (src < n) & ((src < CTX) == seg)
            tap = jnp.where(ok, pltpu.roll(x, (-off) % n, axis=0), 0.0)
        y = y + tap * cw[j:j + 1, :]
    return y


def _bwd_chunk(j):
    return jnp.where(j < CTX_CHUNKS, CTX_CHUNKS - 1 - j, N_CHUNK + CTX_CHUNKS - 1 - j)


def _lin_scan(a, b, reverse):
    t_len = a.shape[0]
    row = lax.broadcasted_iota(I32, a.shape, 0)
    k = 1
    while k < t_len:
        shift = t_len - k if reverse else k
        ok = (row < t_len - k) if reverse else (row >= k)
        a_s = pltpu.roll(a, shift, axis=0)
        b_s = pltpu.roll(b, shift, axis=0)
        b = jnp.where(ok, a * b_s + b, b)
        a = jnp.where(ok, a * a_s, a)
        k *= 2
    return a, b


def _lru_kernel(xa_ref, ya_ref, cw_ref, cb_ref, raw_ref, rab_ref, ixw_ref, ixb_ref, lam_ref, o_ref, xc_ref, hf_ref):
    xc_ref[...] = _seg_conv(xa_ref[0], cw_ref[...], cb_ref[...])

    def gates(x, d):
        xb = x.astype(BF16)
        r = _sigmoid(jnp.dot(xb, raw_ref[d, 0].astype(BF16), preferred_element_type=F32) + rab_ref[d, 0])
        i = _sigmoid(jnp.dot(xb, ixw_ref[d, 0].astype(BF16), preferred_element_type=F32) + ixb_ref[d, 0])
        log_a = (-LRU_C * r) * _softplus(-lam_ref[d, 0])
        a = jnp.exp(log_a)
        return a, jnp.sqrt(-jnp.tanh(log_a) * (a * a + 1.0)) * (i * x)

    def fwd(c, h):
        rows = pl.ds(pl.multiple_of(c * CHUNK, CHUNK), CHUNK)
        a, b = gates(xc_ref[rows, :], 0)
        a_c, b_c = _lin_scan(a, b, False)
        h_all = b_c + a_c * h
        hf_ref[rows, :] = h_all
        return h_all[CHUNK - 1:CHUNK, :]

    lax.fori_loop(0, N_CHUNK, fwd, jnp.zeros((1, LRU_BLOCK), F32))

    def bwd(j, h):
        rows = pl.ds(pl.multiple_of(_bwd_chunk(j) * CHUNK, CHUNK), CHUNK)
        a, b = gates(xc_ref[rows, :], 1)
        a_c, b_c = _lin_scan(a, b, True)
        h_all = b_c + a_c * h
        o_ref[0, rows, :] = (hf_ref[rows, :] + h_all) * _gelu_tanh(ya_ref[0, rows, :])
        return h_all[0:1, :]

    lax.fori_loop(0, N_CHUNK, bwd, jnp.zeros((1, LRU_BLOCK), F32))


def _lru(p_even, conv_w, conv_b, ra_w, ra_b, ix_w, ix_b, lam):
    nb = LRU_BLOCKS
    blk4 = lambda: pl.BlockSpec((2, 1, LRU_BLOCK, LRU_BLOCK), lambda b, k: (0, k, 0, 0))
    vec4 = lambda: pl.BlockSpec((2, 1, 1, LRU_BLOCK), lambda b, k: (0, k, 0, 0))
    return pl.pallas_call(
        _lru_kernel,
        grid=(BATCH, nb),
        in_specs=[
            pl.BlockSpec((1, S, LRU_BLOCK), lambda b, k: (b, 0, k)),
            pl.BlockSpec((1, S, LRU_BLOCK), lambda b, k: (b, 0, nb + k)),
            pl.BlockSpec((CONV_W, LRU_BLOCK), lambda b, k: (0, k)),
            pl.BlockSpec((1, LRU_BLOCK), lambda b, k: (0, k)),
            blk4(), vec4(), blk4(), vec4(), vec4(),
        ],
        out_specs=pl.BlockSpec((1, S, LRU_BLOCK), lambda b, k: (b, 0, k)),
        out_shape=jax.ShapeDtypeStruct((BATCH, S, LRU_W), F32),
        scratch_shapes=[pltpu.VMEM((S, LRU_BLOCK), F32), pltpu.VMEM((S, LRU_BLOCK), F32)],
        compiler_params=_cparams(("arbitrary", "arbitrary")),
        name="rglru",
    )(p_even, p_even, conv_w, conv_b.reshape(1, LRU_W), ra_w, ra_b.reshape(2, nb, 1, LRU_BLOCK),
      ix_w, ix_b.reshape(2, nb, 1, LRU_BLOCK), lam.reshape(2, nb, 1, LRU_BLOCK))


def _rope(x, cos, sin):
    lane = lax.broadcasted_iota(I32, (1, HEAD_DIM), 1)
    first = (lane % (2 * ROPE_PAIRS)) < ROPE_PAIRS
    swapped = jnp.where(first, pltpu.roll(x, HEAD_DIM - ROPE_PAIRS, axis=1), pltpu.roll(x, ROPE_PAIRS, axis=1))
    return x * cos + swapped * sin


def _attn_kernel(sink_ref, q_ref, kp_ref, ko_ref, kn_ref, vp_ref, vo_ref, vn_ref, ck_ref, cv_ref,
                 cq_ref, sq_ref, cp_ref, sp_ref, cn_ref, sn_ref, o_ref):
    t = pl.program_id(1)
    nq = CHUNK
    lat = t >= CTX_CHUNKS
    c_lo = jnp.where(lat, jnp.where(t > CTX_CHUNKS, 0, nq), 0)
    c_hi = jnp.where(lat, jnp.where(t < N_CHUNK - 1, 3 * nq, 2 * nq), 0)
    r = lax.broadcasted_iota(I32, (nq, CTX + 3 * nq), 0)
    c = lax.broadcasted_iota(I32, (nq, CTX + 3 * nq), 1) - CTX
    band_ok = (jnp.abs(c - nq - r) <= WINDOW) & (c >= c_lo) & (c < c_hi)
    valid = (c < 0) | band_ok
    scale = HEAD_DIM ** -0.5
    for g in range(KV_HEADS):
        ks = slice(g * HEAD_DIM, (g + 1) * HEAD_DIM)
        keys = jnp.concatenate([
            ck_ref[0, :, ks],
            _rope(kp_ref[0, :, ks], cp_ref[...], sp_ref[...]),
            _rope(ko_ref[0, :, ks], cq_ref[...], sq_ref[...]),
            _rope(kn_ref[0, :, ks], cn_ref[...], sn_ref[...]),
        ], axis=0).astype(BF16)
        vals = jnp.concatenate([cv_ref[0, :, ks], vp_ref[0, :, ks], vo_ref[0, :, ks], vn_ref[0, :, ks]],
                               axis=0).astype(BF16)
        for hh in range(GROUP):
            h = g * GROUP + hh
            hs = slice(h * HEAD_DIM, (h + 1) * HEAD_DIM)
            qh = _rope(q_ref[0, :, hs], cq_ref[...], sq_ref[...]).astype(BF16)
            s = lax.dot_general(qh, keys, (((1,), (1,)), ((), ())), preferred_element_type=F32) * scale
            s = jnp.where(valid, s, -jnp.inf)
            sink = sink_ref[h]
            m = jnp.maximum(jnp.max(s, axis=-1, keepdims=True), sink)
            p = jnp.exp(s - m)
            den = jnp.sum(p, axis=-1, keepdims=True) + jnp.exp(sink - m)
            p = (p / den).astype(BF16)
            o_ref[0, :, hs] = jnp.dot(p, vals, preferred_element_type=F32)


def _attention(p_even, sink, cos_t, sin_t):
    qc = 2 * LRU_W // ATT_W
    kc = (2 * LRU_W + ATT_W) // KV_W
    vc = kc + 1
    lo, hi = CTX_CHUNKS, N_CHUNK - 1
    prev = lambda t: jnp.clip(t - 1, lo, hi)
    nxt = lambda t: jnp.clip(t + 1, lo, hi)
    own = lambda t: t
    kv = lambda col, f: pl.BlockSpec((1, CHUNK, KV_W), lambda b, t: (b, f(t), col))
    tab = lambda f: pl.BlockSpec((CHUNK, HEAD_DIM), lambda b, t: (f(t), 0))
    return pl.pallas_call(
        _attn_kernel,
        grid=(BATCH, N_CHUNK),
        in_specs=[
            pl.BlockSpec(memory_space=pltpu.SMEM),
            pl.BlockSpec((1, CHUNK, ATT_W), lambda b, t: (b, t, qc)),
            kv(kc, prev), kv(kc, own), kv(kc, nxt),
            kv(vc, prev), kv(vc, own), kv(vc, nxt),
            pl.BlockSpec((1, CTX, KV_W), lambda b, t: (b, 0, kc)),
            pl.BlockSpec((1, CTX, KV_W), lambda b, t: (b, 0, vc)),
            tab(own), tab(own), tab(prev), tab(prev), tab(nxt), tab(nxt),
        ],
        out_specs=pl.BlockSpec((1, CHUNK, ATT_W), lambda b, t: (b, t, 0)),
        out_shape=jax.ShapeDtypeStruct((BATCH, S, ATT_W), F32),
        compiler_params=_cparams(("arbitrary", "arbitrary")),
        name="window_attention",
    )(sink, p_even, p_even, p_even, p_even, p_even, p_even, p_even, p_even, p_even,
      cos_t, sin_t, cos_t, sin_t, cos_t, sin_t)


def _rope_tables():
    inv = jnp.power(ROPE_BASE, -jnp.arange(ROPE_PAIRS, dtype=F32) / ROPE_PAIRS)
    pos = jnp.arange(SEQ)
    row_ang = (pos // GRID_W).astype(F32)[:, None] * inv
    col_ang = (pos % GRID_W).astype(F32)[:, None] * inv
    cos = jnp.concatenate([jnp.cos(row_ang)] * 2 + [jnp.cos(col_ang)] * 2, axis=-1)
    sin = jnp.concatenate([-jnp.sin(row_ang), jnp.sin(row_ang), -jnp.sin(col_ang), jnp.sin(col_ang)], axis=-1)
    cos = jnp.concatenate([jnp.ones((CTX, HEAD_DIM), F32), cos], axis=0)
    sin = jnp.concatenate([jnp.zeros((CTX, HEAD_DIM), F32), sin], axis=0)
    return cos, sin


def _mlstm_step(q, k, v, li, lf, state, reverse):
    c0, n0, m0 = state
    ln = q.shape[0]
    row = lax.broadcasted_iota(I32, (ln, ln), 0)
    col = lax.broadcasted_iota(I32, (ln, ln), 1)
    b = lf
    kk = 1
    while kk < ln:
        ok = (row < ln - kk) if reverse else (row >= kk)
        b = b + jnp.where(ok, pltpu.roll(b, ln - kk if reverse else kk, axis=0), 0.0)
        kk *= 2
    causal = (col >= row) if reverse else (col <= row)
    logw = jnp.where(causal, b - b.T + li.T, -jnp.inf)
    m = jnp.maximum(b + m0, jnp.max(logw, axis=1, keepdims=True))
    w = jnp.exp(logw - m)
    inter = jnp.exp(b + m0 - m)
    qb, kb, vb = q.astype(BF16), k.astype(BF16), v.astype(BF16)
    s = lax.dot_general(qb, kb, (((1,), (1,)), ((), ())), preferred_element_type=F32) * w
    num = jnp.dot(s.astype(BF16), vb, preferred_element_type=F32)
    num = num + jnp.concatenate([inter, inter], axis=1) * jnp.dot(qb, c0.astype(BF16), preferred_element_type=F32)
    den = jnp.sum(s, axis=1, keepdims=True) + inter[:, :1] * jnp.sum(q * n0, axis=1, keepdims=True)
    h = num / jnp.maximum(jnp.abs(den), jnp.exp(-m[:, :1]))
    edge = 0 if reverse else ln - 1
    b_last = b[edge:edge + 1, :]
    log_u = b_last - b + li
    m_new = jnp.maximum(b_last + m0, jnp.max(log_u, axis=0, keepdims=True))
    uk = jnp.exp(log_u - m_new) * k
    decay = jnp.exp(b_last + m0 - m_new)
    c_new = decay[:, :1] * c0 + lax.dot_general(uk.astype(BF16), vb, (((0,), (0,)), ((), ())),
                                                preferred_element_type=F32)
    n_new = decay * n0 + jnp.sum(uk, axis=0, keepdims=True)
    return h, (c_new, n_new, m_new)


def _mlstm_kernel(gb_ref, q_ref, k_ref, v_ref, g_ref, cwq_ref, cbq_ref, cwk_ref, cbk_ref, o_ref, qc_ref, kc_ref):
    hd = pl.program_id(1)
    ln = CHUNK
    qc_ref[...] = _silu(_seg_conv(q_ref[0], cwq_ref[...], cbq_ref[...])) * (M_DK ** -0.5)
    kc_ref[...] = _silu(_seg_conv(k_ref[0], cwk_ref[...], cbk_ref[...]))
    o_ref[0] = jnp.zeros((S, M_DV), F32)
    lane = lax.broadcasted_iota(I32, (1, LANE), 1)

    def gate(gc, ty):
        colv = jnp.sum(jnp.where(lane == ty * M_HEADS + hd, gc, 0.0), axis=1, keepdims=True)
        return jnp.broadcast_to(colv, (ln, ln)) + gb_ref[ty, hd]

    def one(chunk, state, d):
        rows = pl.ds(pl.multiple_of(chunk * ln, ln), ln)
        gc = g_ref[0, rows, :]
        h, state = _mlstm_step(qc_ref[rows, :], kc_ref[rows, :], v_ref[0, rows, :],
                               gate(gc, 2 * d), _log_sigmoid(gate(gc, 2 * d + 1)), state, d == 1)
        o_ref[0, rows, :] += h
        return state

    def body(j, carry):
        sf, sb = carry
        return one(j, sf, 0), one(_bwd_chunk(j), sb, 1)

    zero = (jnp.zeros((M_DK, M_DV), F32), jnp.zeros((1, M_DK), F32), jnp.zeros((1, ln), F32))
    lax.fori_loop(0, N_CHUNK, body, (zero, zero))


def _mlstm(p_odd, gates, conv_w, conv_b, gate_b):
    nh = M_HEADS
    cw = lambda off: pl.BlockSpec((CONV_W, M_DK), lambda b, h: (0, off + h))
    cb = lambda off: pl.BlockSpec((1, M_DK), lambda b, h: (0, off + h))
    conv_b = conv_b.reshape(1, 2 * M_QK)
    return pl.pallas_call(
        _mlstm_kernel,
        grid=(BATCH, nh),
        in_specs=[
            pl.BlockSpec(memory_space=pltpu.SMEM),
            pl.BlockSpec((1, S, M_DK), lambda b, h: (b, 0, h)),
            pl.BlockSpec((1, S, M_DK), lambda b, h: (b, 0, nh + h)),
            pl.BlockSpec((1, S, M_DV), lambda b, h: (b, 0, 2 * M_QK // M_DV + h)),
            pl.BlockSpec((1, S, LANE), lambda b, h: (b, 0, 0)),
            cw(0), cb(0), cw(nh), cb(nh),
        ],
        out_specs=pl.BlockSpec((1, S, M_DV), lambda b, h: (b, 0, h)),
        out_shape=jax.ShapeDtypeStruct((BATCH, S, M_V), F32),
        scratch_shapes=[pltpu.VMEM((S, M_DK), F32), pltpu.VMEM((S, M_DK), F32)],
        compiler_params=_cparams(("arbitrary", "arbitrary")),
        name="mlstm",
    )(gate_b, p_odd, p_odd, p_odd, gates, conv_w, conv_b, conv_w, conv_b)


def _even_mixer(x, mt, norm_w, w_in, conv_w, conv_b, ra_w, ra_b, ix_w, ix_b, lam, sink, w_out, rope):
    p = _in_proj(x, norm_w, mt, w_in, EVEN_IN)
    a = _lru(p, conv_w, conv_b, ra_w, ra_b, ix_w, ix_b, lam)
    b = _attention(p, sink, rope[0], rope[1])
    return _out_proj_even(a, b, x, mt, w_out)


def _odd_mixer(x, mt, norm_w, w_in, conv_w, conv_b, gate_b, hnorm_w, w_out):
    p = _in_proj(x, norm_w, mt, w_in, ODD_MAIN)
    wg = jnp.pad(w_in[:, ODD_MAIN:], ((0, 0), (0, LANE - N_GATES)))
    gates = _gate_proj(x, norm_w, mt, wg)
    h = _mlstm(p, gates, conv_w, conv_b, gate_b)
    return _out_proj_odd(h, p, hnorm_w, x, mt, w_out)


def _router_logits_kernel(x_ref, nw_ref, sh_ref, sc_ref, w_ref, o_ref):
    t = pl.program_id(1)
    h = _norm_mod_rows(x_ref[0], nw_ref[...], sh_ref[0, 0], sc_ref[0, 0], t * TT)
    o_ref[0] = lax.dot_general(w_ref[...], h, (((1,), (1,)), ((), ())),
                               precision=lax.Precision.HIGHEST, preferred_element_type=F32)


def _router_logits(x, nw, mt, w_router):
    return pl.pallas_call(
        _router_logits_kernel,
        grid=(BATCH, N_TT),
        in_specs=[
            pl.BlockSpec((1, TT, D), lambda b, t: (b, t, 0)),
            pl.BlockSpec((1, D), lambda b, t: (0, 0)),
            pl.BlockSpec((1, 1, 2, D), lambda b, t: (3, b, 0, 0)),
            pl.BlockSpec((1, 1, 2, D), lambda b, t: (4, b, 0, 0)),
            pl.BlockSpec((N_EXPERTS, D), lambda b, t: (0, 0)),
        ],
        out_specs=pl.BlockSpec((1, N_EXPERTS, TT), lambda b, t: (b, 0, t)),
        out_shape=jax.ShapeDtypeStruct((BATCH, N_EXPERTS, S), F32),
        compiler_params=_cparams(("arbitrary", "arbitrary")),
        name="router_logits",
    )(x, nw.reshape(1, D), mt, mt, w_router.T)


def _cumsum_lanes(x):
    n = x.shape[1]
    lane = lax.broadcasted_iota(I32, x.shape, 1)
    k = 1
    while k < n:
        x = x + jnp.where(lane >= k, pltpu.roll(x, k, axis=1), 0.0)
        k *= 2
    return x


def _cumsum_rows_excl(x):
    n = x.shape[0]
    row = lax.broadcasted_iota(I32, x.shape, 0)
    inc = x
    k = 1
    while k < n:
        inc = inc + jnp.where(row >= k, pltpu.roll(inc, k, axis=0), 0.0)
        k *= 2
    return inc - x


def _split3(x):
    hi = x.astype(BF16).astype(F32)
    r = x - hi
    mid = r.astype(BF16).astype(F32)
    return hi, mid, (r - mid).astype(BF16).astype(F32)


def _router_select_kernel(lg_ref, idxl_ref, gl_ref, dstl_ref, idxc_ref, gc_ref, dstc_ref, base_ref, vt_ref):
    ne = N_EXPERTS
    lg = lg_ref[0]
    ex = jnp.exp(lg - jnp.max(lg, axis=0, keepdims=True))
    aff = ex / jnp.sum(ex, axis=0, keepdims=True)
    bits = pltpu.bitcast(aff, I32)
    lane = lax.broadcasted_iota(I32, (ne, S), 1)
    sel = jnp.zeros((ne, S), F32)
    pos = jnp.zeros((ne, S), F32)
    for lo, hi, cap in ((0, CTX, CAP_CTX), (CTX, S, CAP_LAT)):
        vb = jnp.where((lane >= lo) & (lane < hi), bits, -1)
        thr = jnp.zeros((ne, 1), I32)
        for bit in range(30, -1, -1):
            cand = thr | (1 << bit)
            cnt = jnp.sum(jnp.where(vb >= cand, 1.0, 0.0), axis=1, keepdims=True)
            thr = jnp.where(cnt >= cap, cand, thr)
        gt = vb > thr
        eq = jnp.where(vb == thr, 1.0, 0.0)
        need = cap - jnp.sum(jnp.where(gt, 1.0, 0.0), axis=1, keepdims=True)
        eq_rank = _cumsum_lanes(eq) - eq
        s_seg = jnp.where(gt | ((eq > 0.0) & (eq_rank < need)), 1.0, 0.0)
        sel = sel + s_seg
        pos = pos + s_seg * (_cumsum_lanes(s_seg) - s_seg)
    cnt_tok = jnp.broadcast_to(jnp.sum(sel, axis=0, keepdims=True), (ne, S))
    base = _cumsum_lanes(cnt_tok) - cnt_tok
    dest = base + _cumsum_rows_excl(sel)
    base_ref[0] = jnp.concatenate([base[0:1], base[0:1] + cnt_tok[0:1], jnp.zeros((SUBLANE - 2, S), F32)],
                                  axis=0).astype(I32)
    posm = jnp.where(sel > 0.0, pos, -1.0)
    lane_f = lane[0:1].astype(F32)
    idx_hi = jnp.floor(lane_f * (1.0 / 64.0))
    idx_lo = lane_f - 64.0 * idx_hi
    dst_hi = jnp.floor(dest * (1.0 / 128.0))
    dst_lo = dest - 128.0 * dst_hi
    g_hi, g_mid, g_lo = _split3(aff)
    for e in range(ne):
        vt_ref[e] = jnp.concatenate([idx_hi, idx_lo, dst_hi[e:e + 1], dst_lo[e:e + 1], g_hi[e:e + 1],
                                     g_mid[e:e + 1], g_lo[e:e + 1], posm[e:e + 1]], axis=0)
    slot = lax.broadcasted_iota(I32, (LANE, 1), 0).astype(F32)

    def compact(vt, prow, s0):
        onehot = jnp.where(prow == slot + s0, 1.0, 0.0).astype(BF16)
        res = lax.dot_general(vt, onehot, (((1,), (1,)), ((), ())), preferred_element_type=F32)
        return (res[0:1] * 64.0 + res[1:2]).astype(I32), res[4:5] + res[5:6] + res[6:7], \
            (res[2:3] * 128.0 + res[3:4]).astype(I32)

    def per_expert(e, _):
        blk = vt_ref[e]
        prow = blk[SUBLANE - 1:SUBLANE]
        vt = blk.astype(BF16)
        idxc_ref[0, e], gc_ref[0, e], dstc_ref[0, e] = compact(vt[:, :CTX], prow[:, :CTX], 0.0)
        for sc in range(CAP_LAT // LANE):
            cs = slice(sc * LANE, (sc + 1) * LANE)
            idxl_ref[0, e, :, cs], gl_ref[0, e, :, cs], dstl_ref[0, e, :, cs] = compact(
                vt[:, CTX:], prow[:, CTX:], float(sc * LANE))
        return 0

    lax.fori_loop(0, ne, per_expert, 0)


def _router_select(logits):
    ne = N_EXPERTS
    out = lambda n, dt: jax.ShapeDtypeStruct((BATCH, ne, 1, n), dt)
    ospec = lambda n: pl.BlockSpec((1, ne, 1, n), lambda b: (b, 0, 0, 0))
    res = pl.pallas_call(
        _router_select_kernel,
        grid=(BATCH,),
        in_specs=[pl.BlockSpec((1, ne, S), lambda b: (b, 0, 0))],
        out_specs=[ospec(CAP_LAT), ospec(CAP_LAT), ospec(CAP_LAT), ospec(LANE), ospec(LANE), ospec(LANE),
                   pl.BlockSpec((1, SUBLANE, S), lambda b: (b, 0, 0))],
        out_shape=[out(CAP_LAT, I32), out(CAP_LAT, F32), out(CAP_LAT, I32), out(LANE, I32), out(LANE, F32),
                   out(LANE, I32), jax.ShapeDtypeStruct((BATCH, SUBLANE, S), I32)],
        scratch_shapes=[pltpu.VMEM((ne, SUBLANE, S), F32)],
        compiler_params=_cparams(("arbitrary",)),
        name="router_select",
    )(logits)
    return [r.reshape(BATCH, ne, r.shape[-1]) for r in res[:6]] + [res[6]]


def _row_copy(src_hbm, dst_vmem, sem, src_row, dst_row):
    return pltpu.make_async_copy(src_hbm.at[pl.ds(src_row, 1)], dst_vmem.at[pl.ds(dst_row, 1)], sem)


def _ffn_up_kernel(src_ref, x_hbm, nw_ref, sh_ref, sc_ref, wg_ref, wu_ref, o_ref, xs_ref, gbuf_ref, sem):
    j = pl.program_id(1)

    @pl.when(j == 0)
    def _():
        for b in range(BATCH):
            def issue(r, _):
                _row_copy(x_hbm, gbuf_ref, sem, src_ref[0, 0, b * RPS + r], r).start()
                return 0

            def wait(r, _):
                _row_copy(x_hbm, gbuf_ref, sem, 0, r).wait()
                return 0

            lax.fori_loop(0, RPS, issue, 0)
            lax.fori_loop(0, RPS, wait, 0)
            is_ctx = lax.broadcasted_iota(I32, (RPS, 1), 0) < CAP_CTX
            y = _rms(gbuf_ref[...]) * nw_ref[...]
            y = y * (1.0 + _ctx_select(is_ctx, sc_ref[0, b])) + _ctx_select(is_ctx, sh_ref[0, b])
            xs_ref[b * RPS:(b + 1) * RPS, :] = y.astype(BF16)

    xs = xs_ref[...]
    hg = jnp.dot(xs, wg_ref[0].astype(BF16), preferred_element_type=F32)
    hu = jnp.dot(xs, wu_ref[0].astype(BF16), preferred_element_type=F32)
    o_ref[0] = (_silu(hg) * hu).astype(BF16)


def _ffn_up(src_rows, x2d, nw, mt, w_gate, w_up):
    tf = 256
    return pl.pallas_call(
        _ffn_up_kernel,
        grid=(N_EXPERTS, D_EXPERT // tf),
        in_specs=[
            pl.BlockSpec((1, 1, R_EXP), lambda e, j: (e, 0, 0), memory_space=pltpu.SMEM),
            pl.BlockSpec(memory_space=pl.ANY),
            pl.BlockSpec((1, D), lambda e, j: (0, 0)),
            pl.BlockSpec((1, BATCH, 2, D), lambda e, j: (3, 0, 0, 0)),
            pl.BlockSpec((1, BATCH, 2, D), lambda e, j: (4, 0, 0, 0)),
            pl.BlockSpec((1, D, tf), lambda e, j: (e, 0, j)),
            pl.BlockSpec((1, D, tf), lambda e, j: (e, 0, j)),
        ],
        out_specs=pl.BlockSpec((1, R_EXP, tf), lambda e, j: (e, 0, j)),
        out_shape=jax.ShapeDtypeStruct((N_EXPERTS, R_EXP, D_EXPERT), BF16),
        scratch_shapes=[pltpu.VMEM((R_EXP, D), BF16), pltpu.VMEM((RPS, D), F32), pltpu.SemaphoreType.DMA(())],
        compiler_params=_cparams(("arbitrary", "arbitrary")),
        name="ffn_up",
    )(src_rows.reshape(N_EXPERTS, 1, R_EXP), x2d, nw.reshape(1, D), mt, mt, w_gate, w_up)


def _ffn_down_kernel(dst_ref, hid_ref, w_ref, g_ref, y_hbm, ybuf_ref, sem):
    i = pl.program_id(1)
    acc = jnp.dot(hid_ref[0], w_ref[0].astype(BF16), preferred_element_type=F32)
    ybuf_ref[...] = acc * g_ref[0]

    def issue(r, _):
        pltpu.make_async_copy(ybuf_ref.at[pl.ds(r, 1)], y_hbm.at[pl.ds(dst_ref[0, 0, i * RPS + r], 1)], sem).start()
        return 0

    def wait(r, _):
        pltpu.make_async_copy(ybuf_ref.at[pl.ds(r, 1)], y_hbm.at[pl.ds(0, 1)], sem).wait()
        return 0

    lax.fori_loop(0, RPS, issue, 0)
    lax.fori_loop(0, RPS, wait, 0)


def _ffn_down(dst_rows, hid, w_down, g_col):
    return pl.pallas_call(
        _ffn_down_kernel,
        grid=(N_EXPERTS, BATCH),
        in_specs=[
            pl.BlockSpec((1, 1, R_EXP), lambda e, i: (e, 0, 0), memory_space=pltpu.SMEM),
            pl.BlockSpec((1, RPS, D_EXPERT), lambda e, i: (e, i, 0)),
            pl.BlockSpec((1, D_EXPERT, D), lambda e, i: (e, 0, 0)),
            pl.BlockSpec((1, RPS, 1), lambda e, i: (e, i, 0)),
        ],
        out_specs=pl.BlockSpec(memory_space=pl.ANY),
        out_shape=jax.ShapeDtypeStruct((BATCH * PAIRS, D), F32),
        scratch_shapes=[pltpu.VMEM((RPS, D), F32), pltpu.SemaphoreType.DMA(())],
        compiler_params=_cparams(("arbitrary", "arbitrary")),
        name="ffn_down",
    )(dst_rows.reshape(N_EXPERTS, 1, R_EXP), hid, w_down, g_col)


def _combine_kernel(cs_ref, ce_ref, ys_hbm, x_ref, base_ref, basen_ref, g_ref, o_ref, buf_ref, acc_ref, sem):
    b = pl.program_id(0)
    t = pl.program_id(1)
    acc_ref[...] = jnp.zeros((TT, D), F32)
    base = base_ref[0]
    basen = basen_ref[0]
    lane = lax.broadcasted_iota(I32, (1, TT), 1)

    def body(c, _):
        cp = pltpu.make_async_copy(ys_hbm.at[pl.ds(b * PAIRS + c * TT, TT)], buf_ref, sem)
        cp.start()
        cp.wait()
        r = c * TT + lane
        onehot = jnp.where((base <= r) & (r < basen), 1.0, 0.0).astype(BF16)
        y = buf_ref[...]
        y_hi = y.astype(BF16)
        y_lo = (y - y_hi.astype(F32)).astype(BF16)
        acc_ref[...] += (jnp.dot(onehot, y_hi, preferred_element_type=F32)
                         + jnp.dot(onehot, y_lo, preferred_element_type=F32))
        return 0

    lax.fori_loop(cs_ref[b, t], ce_ref[b, t], body, 0)
    gate = jnp.where(t == 0, g_ref[0, 0, 0:1, :], g_ref[0, 0, 1:2, :])
    o_ref[0] = x_ref[0] + gate * acc_ref[...]


def _combine(cs, ce, ys, x, base_col, basen_col, mt):
    grid_spec = pltpu.PrefetchScalarGridSpec(
        num_scalar_prefetch=2,
        grid=(BATCH, N_TT),
        in_specs=[
            pl.BlockSpec(memory_space=pl.ANY),
            pl.BlockSpec((1, TT, D), lambda b, t, *_: (b, t, 0)),
            pl.BlockSpec((1, TT, 1), lambda b, t, *_: (b, t, 0)),
            pl.BlockSpec((1, TT, 1), lambda b, t, *_: (b, t, 0)),
            pl.BlockSpec((1, 1, 2, D), lambda b, t, *_: (5, b, 0, 0)),
        ],
        out_specs=pl.BlockSpec((1, TT, D), lambda b, t, *_: (b, t, 0)),
        scratch_shapes=[pltpu.VMEM((TT, D), F32), pltpu.VMEM((TT, D), F32), pltpu.SemaphoreType.DMA(())],
    )
    return pl.pallas_call(
        _combine_kernel,
        grid_spec=grid_spec,
        out_shape=jax.ShapeDtypeStruct((BATCH, S, D), F32),
        compiler_params=_cparams(("arbitrary", "arbitrary")),
        name="moe_combine",
    )(cs, ce, ys, x, base_col, basen_col, mt)


def _moe(x, mt, norm_w, w_router, w_gate, w_up, w_down):
    logits = _router_logits(x, norm_w, mt, w_router)
    idx_l, g_l, dst_l, idx_c, g_c, dst_c, bases = _router_select(logits)
    boff = jnp.arange(BATCH, dtype=I32)[:, None, None]

    def rows(c, l, off):
        r = jnp.concatenate([c[:, :, :CAP_CTX], l], axis=2) + off
        return jnp.transpose(r, (1, 0, 2)).reshape(N_EXPERTS, R_EXP)

    src_rows = rows(idx_c, idx_l, boff * S)
    dst_rows = rows(dst_c, dst_l, boff * PAIRS)
    g_col = rows(g_c, g_l, 0.0).reshape(N_EXPERTS, R_EXP, 1)
    hid = _ffn_up(src_rows, x.reshape(BATCH * S, D), norm_w, mt, w_gate, w_up)
    ys = _ffn_down(dst_rows, hid, w_down, g_col)
    base, basen = bases[:, 0, :], bases[:, 1, :]
    cs = base[:, ::TT] // TT
    ce = (basen[:, TT - 1::TT] + TT - 1) // TT
    return _combine(cs, ce, ys, x, base.reshape(BATCH, S, 1), basen.reshape(BATCH, S, 1), mt)


def kernel(x, c, ctx, c_ctx, ada_w, ada_b, norm_mix_w, norm_ffn_w,
           ev_w_in, ev_conv_w, ev_conv_b, ev_ra_w, ev_ra_b, ev_ix_w, ev_ix_b, ev_lambda, ev_sink, ev_w_out,
           od_w_in, od_conv_w, od_conv_b, od_gate_b, od_hnorm_w, od_w_out,
           moe_router, moe_w_gate, moe_w_up, moe_w_down, final_norm_w):
    assert x.shape == (BATCH, SEQ, D) and ctx.shape == (BATCH, CTX, D)
    cc = jnp.zeros((SUBLANE, D), F32).at[:BATCH].set(c).at[BATCH].set(c_ctx)
    mod = _modulation(cc, ada_w, ada_b)
    rope = _rope_tables()
    xs = jnp.concatenate([ctx, x], axis=1)
    for layer in range(DEPTH):
        mt = _mod_table(mod[layer])
        i = layer // 2
        if layer % 2 == 0:
            xs = _even_mixer(xs, mt, norm_mix_w[layer], ev_w_in[i], ev_conv_w[i], ev_conv_b[i], ev_ra_w[i],
                             ev_ra_b[i], ev_ix_w[i], ev_ix_b[i], ev_lambda[i], ev_sink[i], ev_w_out[i], rope)
        else:
            xs = _odd_mixer(xs, mt, norm_mix_w[layer], od_w_in[i], od_conv_w[i], od_conv_b[i], od_gate_b[i],
                            od_hnorm_w[i], od_w_out[i])
        xs = _moe(xs, mt, norm_ffn_w[layer], moe_router[layer], moe_w_gate[layer], moe_w_up[layer],
                  moe_w_down[layer])
    return _final_norm(xs, final_norm_w)
```

```python
import functools

import jax
import jax.numpy as jnp
from jax import lax
from jax.experimental import pallas as pl
from jax.experimental.pallas import tpu as pltpu

F32 = jnp.float32
BF16 = jnp.bfloat16
I32 = jnp.int32

D = 2048
BATCH = 4
SEQ = 4096
CTX = 256
S = CTX + SEQ
DEPTH = 4
N_MOD = 6
EPS = 1e-6
GRID_W = 64

LRU_W = 1024
LRU_BLOCKS = 8
LRU_BLOCK = 128
LRU_C = 8.0
CONV_W = 4
ATT_HEADS = 8
KV_HEADS = 2
GROUP = ATT_HEADS // KV_HEADS
HEAD_DIM = 128
ATT_W = ATT_HEADS * HEAD_DIM
KV_W = KV_HEADS * HEAD_DIM
WINDOW = 128
ROPE_PAIRS = HEAD_DIM // 4
ROPE_BASE = 10000.0
EVEN_IN = 2 * LRU_W + ATT_W + 2 * KV_W

M_HEADS = 8
M_DK = 128
M_DV = 256
M_QK = M_HEADS * M_DK
M_V = M_HEADS * M_DV
ODD_MAIN = 2 * M_QK + 2 * M_V
N_GATES = 4 * M_HEADS

N_EXPERTS = 16
EC_FACTOR = 2
D_EXPERT = 1536
CAP_LAT = EC_FACTOR * SEQ // N_EXPERTS
CAP_CTX = EC_FACTOR * CTX // N_EXPERTS
RPS = CAP_CTX + CAP_LAT
R_EXP = BATCH * RPS
PAIRS = N_EXPERTS * RPS

LANE = 128
SUBLANE = 8
VMEM_LIMIT = 56 * 1024 * 1024
TM = S // 4
TILES_PER_SAMPLE = S // TM
TN = 512
TT = 256
N_TT = S // TT
CHUNK = 128
N_CHUNK = S // CHUNK
CTX_CHUNKS = CTX // CHUNK


def _cparams(sem, vmem=VMEM_LIMIT):
    return pltpu.CompilerParams(dimension_semantics=sem, vmem_limit_bytes=vmem)


def _sigmoid(x):
    return 1.0 / (1.0 + jnp.exp(-x))


def _silu(x):
    return x * _sigmoid(x)


def _softplus(x):
    return jnp.maximum(x, 0.0) + jnp.log1p(jnp.exp(-jnp.abs(x)))


def _log_sigmoid(x):
    return -_softplus(-x)


def _gelu_tanh(x):
    return 0.5 * x * (1.0 + jnp.tanh(0.7978845608028654 * (x + 0.044715 * (x * x * x))))


def _rms(x):
    return x * lax.rsqrt(jnp.mean(x * x, axis=-1, keepdims=True) + EPS)


def _ctx_select(is_ctx, tab):
    return jnp.where(is_ctx, tab[0:1, :], tab[1:2, :])


def _mod_kernel(c_ref, w_ref, b_ref, o_ref):
    a = _silu(c_ref[...]).astype(BF16)
    o_ref[0] = jnp.dot(a, w_ref[0].astype(BF16), preferred_element_type=F32) + b_ref[0]


def _modulation(cc, ada_w, ada_b):
    tn = 1024
    return pl.pallas_call(
        _mod_kernel,
        grid=(DEPTH, N_MOD * D // tn),
        in_specs=[
            pl.BlockSpec((SUBLANE, D), lambda l, j: (0, 0)),
            pl.BlockSpec((1, D, tn), lambda l, j: (l, 0, j)),
            pl.BlockSpec((1, 1, tn), lambda l, j: (l, 0, j)),
        ],
        out_specs=pl.BlockSpec((1, SUBLANE, tn), lambda l, j: (l, 0, j)),
        out_shape=jax.ShapeDtypeStruct((DEPTH, SUBLANE, N_MOD * D), F32),
        compiler_params=_cparams(("arbitrary", "arbitrary")),
        name="adaln_mod",
    )(cc, ada_w, ada_b.reshape(DEPTH, 1, N_MOD * D))


def _mod_table(mod_layer):
    m = mod_layer.reshape(SUBLANE, N_MOD, D)
    lat = jnp.transpose(m[:BATCH], (1, 0, 2))
    ctx = jnp.broadcast_to(m[BATCH][:, None, :], (N_MOD, BATCH, D))
    return jnp.stack([ctx, lat], axis=2)


def _norm_mod_rows(x, nw, sh_tab, sc_tab, row0):
    rows = x.shape[0]
    is_ctx = (row0 + lax.broadcasted_iota(I32, (rows, 1), 0)) < CTX
    y = _rms(x) * nw
    return y * (1.0 + _ctx_select(is_ctx, sc_tab)) + _ctx_select(is_ctx, sh_tab)


def _in_proj_kernel(x_ref, nw_ref, sh_ref, sc_ref, w_ref, o_ref, xn_ref):
    i = pl.program_id(0)
    j = pl.program_id(1)

    @pl.when(j == 0)
    def _():
        row0 = (i % TILES_PER_SAMPLE) * TM
        xn_ref[...] = _norm_mod_rows(x_ref[0], nw_ref[...], sh_ref[0, 0], sc_ref[0, 0], row0).astype(BF16)

    o_ref[0] = jnp.dot(xn_ref[...], w_ref[...].astype(BF16), preferred_element_type=F32)


def _in_proj(x, nw, mt, w, li, n_out):
    return pl.pallas_call(
        _in_proj_kernel,
        grid=(BATCH * TILES_PER_SAMPLE, n_out // TN),
        in_specs=[
            pl.BlockSpec((1, TM, D), lambda i, j: (i // TILES_PER_SAMPLE, i % TILES_PER_SAMPLE, 0)),
            pl.BlockSpec((1, D), lambda i, j: (0, 0)),
            pl.BlockSpec((1, 1, 2, D), lambda i, j: (0, i // TILES_PER_SAMPLE, 0, 0)),
            pl.BlockSpec((1, 1, 2, D), lambda i, j: (1, i // TILES_PER_SAMPLE, 0, 0)),
            pl.BlockSpec((None, D, TN), lambda i, j: (li, 0, j)),
        ],
        out_specs=pl.BlockSpec((1, TM, TN), lambda i, j: (i // TILES_PER_SAMPLE, i % TILES_PER_SAMPLE, j)),
        out_shape=jax.ShapeDtypeStruct((BATCH, S, n_out), F32),
        scratch_shapes=[pltpu.VMEM((TM, D), BF16)],
        compiler_params=_cparams(("arbitrary", "arbitrary")),
        name="in_proj",
    )(x, nw.reshape(1, D), mt, mt, w)


def _gate_proj_kernel(x_ref, nw_ref, sh_ref, sc_ref, w_ref, o_ref):
    t = pl.program_id(1)
    xn = _norm_mod_rows(x_ref[0], nw_ref[...], sh_ref[0, 0], sc_ref[0, 0], t * TT).astype(BF16)
    o_ref[0] = jnp.dot(xn, w_ref[...].astype(BF16), preferred_element_type=F32)


def _gate_proj(x, nw, mt, wg):
    return pl.pallas_call(
        _gate_proj_kernel,
        grid=(BATCH, N_TT),
        in_specs=[
            pl.BlockSpec((1, TT, D), lambda b, t: (b, t, 0)),
            pl.BlockSpec((1, D), lambda b, t: (0, 0)),
            pl.BlockSpec((1, 1, 2, D), lambda b, t: (0, b, 0, 0)),
            pl.BlockSpec((1, 1, 2, D), lambda b, t: (1, b, 0, 0)),
            pl.BlockSpec((D, LANE), lambda b, t: (0, 0)),
        ],
        out_specs=pl.BlockSpec((1, TT, LANE), lambda b, t: (b, t, 0)),
        out_shape=jax.ShapeDtypeStruct((BATCH, S, LANE), F32),
        compiler_params=_cparams(("arbitrary", "arbitrary")),
        name="gate_proj",
    )(x, nw.reshape(1, D), mt, mt, wg)


def _residual_epilogue(i, x_ref, g_ref, acc, o_ref):
    tm = acc.shape[0]
    row0 = (i % (S // tm)) * tm
    is_ctx = (row0 + lax.broadcasted_iota(I32, (tm, 1), 0)) < CTX
    o_ref[0] = x_ref[0] + _ctx_select(is_ctx, g_ref[0, 0]) * acc


def _out_proj_even_kernel(a1_ref, a2_ref, x_ref, g_ref, w_ref, o_ref, a_ref):
    i = pl.program_id(0)
    j = pl.program_id(1)

    @pl.when(j == 0)
    def _():
        a_ref[:, :LRU_W] = a1_ref[0].astype(BF16)
        a_ref[:, LRU_W:] = a2_ref[0].astype(BF16)

    acc = jnp.dot(a_ref[...], w_ref[...].astype(BF16), preferred_element_type=F32)
    _residual_epilogue(i, x_ref, g_ref, acc, o_ref)


def _out_proj_odd_kernel(h_ref, og_ref, hw_ref, x_ref, g_ref, w_ref, o_ref, a_ref):
    i = pl.program_id(0)
    j = pl.program_id(1)

    @pl.when(j == 0)
    def _():
        for h in range(M_HEADS):
            sl = slice(h * M_DV, (h + 1) * M_DV)
            hn = _rms(h_ref[0, :, sl]) * hw_ref[:, sl] * _sigmoid(og_ref[0, :, sl])
            a_ref[:, sl] = hn.astype(BF16)

    acc = jnp.dot(a_ref[...], w_ref[...].astype(BF16), preferred_element_type=F32)
    _residual_epilogue(i, x_ref, g_ref, acc, o_ref)


def _row_spec(tm, width, col=None):
    tps = S // tm
    if col is None:
        return pl.BlockSpec((1, tm, width), lambda i, j: (i // tps, i % tps, j))
    return pl.BlockSpec((1, tm, width), lambda i, j: (i // tps, i % tps, col))


def _out_proj_even(a1, a2, x, mt, w, li):
    tm = TM
    return pl.pallas_call(
        _out_proj_even_kernel,
        grid=(BATCH * S // tm, D // TN),
        in_specs=[
            _row_spec(tm, LRU_W, 0),
            _row_spec(tm, ATT_W, 0),
            _row_spec(tm, TN),
            pl.BlockSpec((1, 1, 2, TN), lambda i, j: (2, i // (S // tm), 0, j)),
            pl.BlockSpec((None, D, TN), lambda i, j: (li, 0, j)),
        ],
        out_specs=_row_spec(tm, TN),
        out_shape=jax.ShapeDtypeStruct((BATCH, S, D), F32),
        scratch_shapes=[pltpu.VMEM((tm, D), BF16)],
        compiler_params=_cparams(("arbitrary", "arbitrary")),
        name="out_proj_even",
    )(a1, a2, x, mt, w)


def _out_proj_odd(h, p_odd, hnorm_w, x, mt, w, li):
    tm = TM // 2
    return pl.pallas_call(
        _out_proj_odd_kernel,
        grid=(BATCH * S // tm, D // TN),
        in_specs=[
            _row_spec(tm, M_V, 0),
            _row_spec(tm, M_V, (2 * M_QK + M_V) // M_V),
            pl.BlockSpec((1, M_V), lambda i, j: (0, 0)),
            _row_spec(tm, TN),
            pl.BlockSpec((1, 1, 2, TN), lambda i, j: (2, i // (S // tm), 0, j)),
            pl.BlockSpec((None, D, TN), lambda i, j: (li, 0, j)),
        ],
        out_specs=_row_spec(tm, TN),
        out_shape=jax.ShapeDtypeStruct((BATCH, S, D), F32),
        scratch_shapes=[pltpu.VMEM((tm, D), BF16)],
        compiler_params=_cparams(("arbitrary", "arbitrary")),
        name="out_proj_odd",
    )(h, p_odd, hnorm_w.reshape(1, M_V), x, mt, w)


def _final_norm_kernel(x_ref, w_ref, o_ref):
    o_ref[0] = _rms(x_ref[0]) * w_ref[...]


def _final_norm(x, w):
    return pl.pallas_call(
        _final_norm_kernel,
        grid=(BATCH, SEQ // TT),
        in_specs=[
            pl.BlockSpec((1, TT, D), lambda b, t: (b, t + CTX // TT, 0)),
            pl.BlockSpec((1, D), lambda b, t: (0, 0)),
        ],
        out_specs=pl.BlockSpec((1, TT, D), lambda b, t: (b, t, 0)),
        out_shape=jax.ShapeDtypeStruct((BATCH, SEQ, D), F32),
        compiler_params=_cparams(("arbitrary", "arbitrary")),
        name="final_norm",
    )(x, w.reshape(1, D))


def _seg_conv(x, cw, cb):
    n = x.shape[0]
    row = lax.broadcasted_iota(I32, (n, 1), 0)
    seg = row < CTX
    y = cb
    for j in range(CONV_W):
        off = j - CONV_W // 2
        if off == 0:
            tap = x
        else:
            src = row + off
            ok = (src >= 0) & (src < n) & ((src < CTX) == seg)
            tap = jnp.where(ok, pltpu.roll(x, (-off) % n, axis=0), 0.0)
        y = y + tap * cw[j:j + 1, :]
    return y


def _bwd_chunk(j):
    return jnp.where(j < CTX_CHUNKS, CTX_CHUNKS - 1 - j, N_CHUNK + CTX_CHUNKS - 1 - j)


def _lin_scan(a, b, reverse):
    t_len = a.shape[0]
    row = lax.broadcasted_iota(I32, a.shape, 0)
    k = 1
    while k < t_len:
        shift = t_len - k if reverse else k
        ok = (row < t_len - k) if reverse else (row >= k)
        a_s = pltpu.roll(a, shift, axis=0)
        b_s = pltpu.roll(b, shift, axis=0)
        b = jnp.where(ok, a * b_s + b, b)
        a = jnp.where(ok, a * a_s, a)
        k *= 2
    return a, b


def _lru_kernel(xa_ref, ya_ref, cw_ref, cb_ref, raw_ref, rab_ref, ixw_ref, ixb_ref, lam_ref, o_ref, xc_ref, hf_ref):
    xc_ref[...] = _seg_conv(xa_ref[0], cw_ref[...], cb_ref[...])

    def gates(x, d):
        xb = x.astype(BF16)
        r = _sigmoid(jnp.dot(xb, raw_ref[d, 0].astype(BF16), preferred_element_type=F32) + rab_ref[d, 0])
        i = _sigmoid(jnp.dot(xb, ixw_ref[d, 0].astype(BF16), preferred_element_type=F32) + ixb_ref[d, 0])
        log_a = (-LRU_C * r) * _softplus(-lam_ref[d, 0])
        a = jnp.exp(log_a)
        return a, jnp.sqrt(-jnp.tanh(log_a) * (a * a + 1.0)) * (i * x)

    hf_ref[...] = jnp.zeros((S, LRU_BLOCK), F32)

    def one(chunk, h, d):
        rows = pl.ds(pl.multiple_of(chunk * CHUNK, CHUNK), CHUNK)
        a, b = gates(xc_ref[rows, :], d)
        a_c, b_c = _lin_scan(a, b, d == 1)
        h_all = b_c + a_c * h
        hf_ref[rows, :] += h_all
        return h_all[0:1, :] if d == 1 else h_all[CHUNK - 1:CHUNK, :]

    def body(j, carry):
        return one(j, carry[0], 0), one(_bwd_chunk(j), carry[1], 1)

    zero = jnp.zeros((1, LRU_BLOCK), F32)
    lax.fori_loop(0, N_CHUNK, body, (zero, zero))
    o_ref[0] = hf_ref[...] * _gelu_tanh(ya_ref[0])


def _lru(p_even, conv_w, conv_b, ra_w, ra_b, ix_w, ix_b, lam):
    nb = LRU_BLOCKS
    blk4 = lambda: pl.BlockSpec((2, 1, LRU_BLOCK, LRU_BLOCK), lambda b, k: (0, k, 0, 0))
    vec4 = lambda: pl.BlockSpec((2, 1, 1, LRU_BLOCK), lambda b, k: (0, k, 0, 0))
    return pl.pallas_call(
        _lru_kernel,
        grid=(BATCH, nb),
        in_specs=[
            pl.BlockSpec((1, S, LRU_BLOCK), lambda b, k: (b, 0, k)),
            pl.BlockSpec((1, S, LRU_BLOCK), lambda b, k: (b, 0, nb + k)),
            pl.BlockSpec((CONV_W, LRU_BLOCK), lambda b, k: (0, k)),
            pl.BlockSpec((1, LRU_BLOCK), lambda b, k: (0, k)),
            blk4(), vec4(), blk4(), vec4(), vec4(),
        ],
        out_specs=pl.BlockSpec((1, S, LRU_BLOCK), lambda b, k: (b, 0, k)),
        out_shape=jax.ShapeDtypeStruct((BATCH, S, LRU_W), F32),
        scratch_shapes=[pltpu.VMEM((S, LRU_BLOCK), F32), pltpu.VMEM((S, LRU_BLOCK), F32)],
        compiler_params=_cparams(("arbitrary", "arbitrary")),
        name="rglru",
    )(p_even, p_even, conv_w, conv_b.reshape(1, LRU_W), ra_w, ra_b.reshape(2, nb, 1, LRU_BLOCK),
      ix_w, ix_b.reshape(2, nb, 1, LRU_BLOCK), lam.reshape(2, nb, 1, LRU_BLOCK))


def _rope(x, cos, sin):
    lane = lax.broadcasted_iota(I32, (1, HEAD_DIM), 1)
    first = (lane % (2 * ROPE_PAIRS)) < ROPE_PAIRS
    swapped = jnp.where(first, pltpu.roll(x, HEAD_DIM - ROPE_PAIRS, axis=1), pltpu.roll(x, ROPE_PAIRS, axis=1))
    return x * cos + swapped * sin


def _attn_kernel(sink_ref, q_ref, kp_ref, ko_ref, kn_ref, vp_ref, vo_ref, vn_ref, ck_ref, cv_ref,
                 cq_ref, sq_ref, cp_ref, sp_ref, cn_ref, sn_ref, o_ref):
    t = pl.program_id(1)
    nq = CHUNK
    lat = t >= CTX_CHUNKS
    c_lo = jnp.where(lat, jnp.where(t > CTX_CHUNKS, 0, nq), 0)
    c_hi = jnp.where(lat, jnp.where(t < N_CHUNK - 1, 3 * nq, 2 * nq), 0)
    r = lax.broadcasted_iota(I32, (nq, CTX + 3 * nq), 0)
    c = lax.broadcasted_iota(I32, (nq, CTX + 3 * nq), 1) - CTX
    band_ok = (jnp.abs(c - nq - r) <= WINDOW) & (c >= c_lo) & (c < c_hi)
    valid = (c < 0) | band_ok
    scale = HEAD_DIM ** -0.5
    for g in range(KV_HEADS):
        ks = slice(g * HEAD_DIM, (g + 1) * HEAD_DIM)
        keys = jnp.concatenate([
            ck_ref[0, :, ks],
            _rope(kp_ref[0, :, ks], cp_ref[...], sp_ref[...]),
            _rope(ko_ref[0, :, ks], cq_ref[...], sq_ref[...]),
            _rope(kn_ref[0, :, ks], cn_ref[...], sn_ref[...]),
        ], axis=0).astype(BF16)
        vals = jnp.concatenate([cv_ref[0, :, ks], vp_ref[0, :, ks], vo_ref[0, :, ks], vn_ref[0, :, ks]],
                               axis=0).astype(BF16)
        for hh in range(GROUP):
            h = g * GROUP + hh
            hs = slice(h * HEAD_DIM, (h + 1) * HEAD_DIM)
            qh = _rope(q_ref[0, :, hs], cq_ref[...], sq_ref[...]).astype(BF16)
            s = lax.dot_general(qh, keys, (((1,), (1,)), ((), ())), preferred_element_type=F32) * scale
            s = jnp.where(valid, s, -jnp.inf)
            sink = sink_ref[h]
            m = jnp.maximum(jnp.max(s, axis=-1, keepdims=True), sink)
            p = jnp.exp(s - m)
            den = jnp.sum(p, axis=-1, keepdims=True) + jnp.exp(sink - m)
            p = (p / den).astype(BF16)
            o_ref[0, :, hs] = jnp.dot(p, vals, preferred_element_type=F32)


def _attention(p_even, sink, cos_t, sin_t):
    qc = 2 * LRU_W // ATT_W
    kc = (2 * LRU_W + ATT_W) // KV_W
    vc = kc + 1
    lo, hi = CTX_CHUNKS, N_CHUNK - 1
    prev = lambda t: jnp.clip(t - 1, lo, hi)
    nxt = lambda t: jnp.clip(t + 1, lo, hi)
    own = lambda t: t
    kv = lambda col, f: pl.BlockSpec((1, CHUNK, KV_W), lambda b, t: (b, f(t), col))
    tab = lambda f: pl.BlockSpec((CHUNK, HEAD_DIM), lambda b, t: (f(t), 0))
    return pl.pallas_call(
        _attn_kernel,
        grid=(BATCH, N_CHUNK),
        in_specs=[
            pl.BlockSpec(memory_space=pltpu.SMEM),
            pl.BlockSpec((1, CHUNK, ATT_W), lambda b, t: (b, t, qc)),
            kv(kc, prev), kv(kc, own), kv(kc, nxt),
            kv(vc, prev), kv(vc, own), kv(vc, nxt),
            pl.BlockSpec((1, CTX, KV_W), lambda b, t: (b, 0, kc)),
            pl.BlockSpec((1, CTX, KV_W), lambda b, t: (b, 0, vc)),
            tab(own), tab(own), tab(prev), tab(prev), tab(nxt), tab(nxt),
        ],
        out_specs=pl.BlockSpec((1, CHUNK, ATT_W), lambda b, t: (b, t, 0)),
        out_shape=jax.ShapeDtypeStruct((BATCH, S, ATT_W), F32),
        compiler_params=_cparams(("arbitrary", "arbitrary")),
        name="window_attention",
    )(sink, p_even, p_even, p_even, p_even, p_even, p_even, p_even, p_even, p_even,
      cos_t, sin_t, cos_t, sin_t, cos_t, sin_t)


def _rope_tables():
    inv = jnp.power(ROPE_BASE, -jnp.arange(ROPE_PAIRS, dtype=F32) / ROPE_PAIRS)
    pos = jnp.arange(SEQ)
    row_ang = (pos // GRID_W).astype(F32)[:, None] * inv
    col_ang = (pos % GRID_W).astype(F32)[:, None] * inv
    cos = jnp.concatenate([jnp.cos(row_ang)] * 2 + [jnp.cos(col_ang)] * 2, axis=-1)
    sin = jnp.concatenate([-jnp.sin(row_ang), jnp.sin(row_ang), -jnp.sin(col_ang), jnp.sin(col_ang)], axis=-1)
    cos = jnp.concatenate([jnp.ones((CTX, HEAD_DIM), F32), cos], axis=0)
    sin = jnp.concatenate([jnp.zeros((CTX, HEAD_DIM), F32), sin], axis=0)
    return cos, sin


def _mlstm_step(q, k, v, li, lf, state, reverse):
    c0, n0, m0 = state
    ln = q.shape[0]
    row = lax.broadcasted_iota(I32, (ln, ln), 0)
    col = lax.broadcasted_iota(I32, (ln, ln), 1)
    b = lf
    kk = 1
    while kk < ln:
        ok = (row < ln - kk) if reverse else (row >= kk)
        b = b + jnp.where(ok, pltpu.roll(b, ln - kk if reverse else kk, axis=0), 0.0)
        kk *= 2
    causal = (col >= row) if reverse else (col <= row)
    logw = jnp.where(causal, b - b.T + li.T, -jnp.inf)
    m = jnp.maximum(b + m0, jnp.max(logw, axis=1, keepdims=True))
    w = jnp.exp(logw - m)
    inter = jnp.exp(b + m0 - m)
    qb, kb, vb = q.astype(BF16), k.astype(BF16), v.astype(BF16)
    s = lax.dot_general(qb, kb, (((1,), (1,)), ((), ())), preferred_element_type=F32) * w
    num = jnp.dot(s.astype(BF16), vb, preferred_element_type=F32)
    num = num + jnp.concatenate([inter, inter], axis=1) * jnp.dot(qb, c0.astype(BF16), preferred_element_type=F32)
    den = jnp.sum(s, axis=1, keepdims=True) + inter[:, :1] * jnp.sum(q * n0, axis=1, keepdims=True)
    h = num / jnp.maximum(jnp.abs(den), jnp.exp(-m[:, :1]))
    edge = 0 if reverse else ln - 1
    b_last = b[edge:edge + 1, :]
    log_u = b_last - b + li
    m_new = jnp.maximum(b_last + m0, jnp.max(log_u, axis=0, keepdims=True))
    uk = jnp.exp(log_u - m_new) * k
    decay = jnp.exp(b_last + m0 - m_new)
    c_new = decay[:, :1] * c0 + lax.dot_general(uk.astype(BF16), vb, (((0,), (0,)), ((), ())),
                                                preferred_element_type=F32)
    n_new = decay * n0 + jnp.sum(uk, axis=0, keepdims=True)
    return h, (c_new, n_new, m_new)


def _mlstm_kernel(gb_ref, q_ref, k_ref, v_ref, g_ref, cwq_ref, cbq_ref, cwk_ref, cbk_ref, o_ref, qc_ref, kc_ref):
    hd = pl.program_id(1)
    ln = CHUNK
    qc_ref[...] = _silu(_seg_conv(q_ref[0], cwq_ref[...], cbq_ref[...])) * (M_DK ** -0.5)
    kc_ref[...] = _silu(_seg_conv(k_ref[0], cwk_ref[...], cbk_ref[...]))
    o_ref[0] = jnp.zeros((S, M_DV), F32)
    lane = lax.broadcasted_iota(I32, (1, LANE), 1)

    def gate(gc, ty):
        colv = jnp.sum(jnp.where(lane == ty * M_HEADS + hd, gc, 0.0), axis=1, keepdims=True)
        return jnp.broadcast_to(colv, (ln, ln)) + gb_ref[ty, hd]

    def one(chunk, state, d):
        rows = pl.ds(pl.multiple_of(chunk * ln, ln), ln)
        gc = g_ref[0, rows, :]
        h, state = _mlstm_step(qc_ref[rows, :], kc_ref[rows, :], v_ref[0, rows, :],
                               gate(gc, 2 * d), _log_sigmoid(gate(gc, 2 * d + 1)), state, d == 1)
        o_ref[0, rows, :] += h
        return state

    def body(j, carry):
        sf, sb = carry
        return one(j, sf, 0), one(_bwd_chunk(j), sb, 1)

    zero = (jnp.zeros((M_DK, M_DV), F32), jnp.zeros((1, M_DK), F32), jnp.zeros((1, ln), F32))
    lax.fori_loop(0, N_CHUNK, body, (zero, zero))


def _mlstm(p_odd, gates, conv_w, conv_b, gate_b):
    nh = M_HEADS
    cw = lambda off: pl.BlockSpec((CONV_W, M_DK), lambda b, h: (0, off + h))
    cb = lambda off: pl.BlockSpec((1, M_DK), lambda b, h: (0, off + h))
    conv_b = conv_b.reshape(1, 2 * M_QK)
    return pl.pallas_call(
        _mlstm_kernel,
        grid=(BATCH, nh),
        in_specs=[
            pl.BlockSpec(memory_space=pltpu.SMEM),
            pl.BlockSpec((1, S, M_DK), lambda b, h: (b, 0, h)),
            pl.BlockSpec((1, S, M_DK), lambda b, h: (b, 0, nh + h)),
            pl.BlockSpec((1, S, M_DV), lambda b, h: (b, 0, 2 * M_QK // M_DV + h)),
            pl.BlockSpec((1, S, LANE), lambda b, h: (b, 0, 0)),
            cw(0), cb(0), cw(nh), cb(nh),
        ],
        out_specs=pl.BlockSpec((1, S, M_DV), lambda b, h: (b, 0, h)),
        out_shape=jax.ShapeDtypeStruct((BATCH, S, M_V), F32),
        scratch_shapes=[pltpu.VMEM((S, M_DK), F32), pltpu.VMEM((S, M_DK), F32)],
        compiler_params=_cparams(("arbitrary", "arbitrary")),
        name="mlstm",
    )(gate_b, p_odd, p_odd, p_odd, gates, conv_w, conv_b, conv_w, conv_b)


def _even_mixer(x, mt, norm_w, w_in, li, conv_w, conv_b, ra_w, ra_b, ix_w, ix_b, lam, sink, w_out, rope):
    p = _in_proj(x, norm_w, mt, w_in, li, EVEN_IN)
    a = _lru(p, conv_w, conv_b, ra_w, ra_b, ix_w, ix_b, lam)
    b = _attention(p, sink, rope[0], rope[1])
    return _out_proj_even(a, b, x, mt, w_out, li)


def _odd_mixer(x, mt, norm_w, w_in, li, conv_w, conv_b, gate_b, hnorm_w, w_out):
    p = _in_proj(x, norm_w, mt, w_in, li, ODD_MAIN)
    wg = jnp.pad(w_in[li, :, ODD_MAIN:], ((0, 0), (0, LANE - N_GATES)))
    gates = _gate_proj(x, norm_w, mt, wg)
    h = _mlstm(p, gates, conv_w, conv_b, gate_b)
    return _out_proj_odd(h, p, hnorm_w, x, mt, w_out, li)


def _router_logits_kernel(x_ref, nw_ref, sh_ref, sc_ref, w_ref, o_ref):
    t = pl.program_id(1)
    h = _norm_mod_rows(x_ref[0], nw_ref[...], sh_ref[0, 0], sc_ref[0, 0], t * TT)
    o_ref[0] = lax.dot_general(w_ref[...], h, (((1,), (1,)), ((), ())),
                               precision=lax.Precision.HIGHEST, preferred_element_type=F32)


def _router_logits(x, nw, mt, w_router):
    return pl.pallas_call(
        _router_logits_kernel,
        grid=(BATCH, N_TT),
        in_specs=[
            pl.BlockSpec((1, TT, D), lambda b, t: (b, t, 0)),
            pl.BlockSpec((1, D), lambda b, t: (0, 0)),
            pl.BlockSpec((1, 1, 2, D), lambda b, t: (3, b, 0, 0)),
            pl.BlockSpec((1, 1, 2, D), lambda b, t: (4, b, 0, 0)),
            pl.BlockSpec((N_EXPERTS, D), lambda b, t: (0, 0)),
        ],
        out_specs=pl.BlockSpec((1, N_EXPERTS, TT), lambda b, t: (b, 0, t)),
        out_shape=jax.ShapeDtypeStruct((BATCH, N_EXPERTS, S), F32),
        compiler_params=_cparams(("arbitrary", "arbitrary")),
        name="router_logits",
    )(x, nw.reshape(1, D), mt, mt, w_router.T)


def _cumsum_lanes(x):
    n = x.shape[1]
    lane = lax.broadcasted_iota(I32, x.shape, 1)
    k = 1
    while k < n:
        x = x + jnp.where(lane >= k, pltpu.roll(x, k, axis=1), 0.0)
        k *= 2
    return x


def _cumsum_rows_excl(x):
    n = x.shape[0]
    row = lax.broadcasted_iota(I32, x.shape, 0)
    inc = x
    k = 1
    while k < n:
        inc = inc + jnp.where(row >= k, pltpu.roll(inc, k, axis=0), 0.0)
        k *= 2
    return inc - x


def _split3(x):
    hi = x.astype(BF16).astype(F32)
    r = x - hi
    mid = r.astype(BF16).astype(F32)
    return hi, mid, (r - mid).astype(BF16).astype(F32)


def _router_select_kernel(lg_ref, idxl_ref, gl_ref, dstl_ref, idxc_ref, gc_ref, dstc_ref, base_ref, vt_ref):
    ne = N_EXPERTS
    lg = lg_ref[0]
    ex = jnp.exp(lg - jnp.max(lg, axis=0, keepdims=True))
    aff = ex / jnp.sum(ex, axis=0, keepdims=True)
    bits = pltpu.bitcast(aff, I32)
    lane = lax.broadcasted_iota(I32, (ne, S), 1)
    sel = jnp.zeros((ne, S), F32)
    pos = jnp.zeros((ne, S), F32)
    for lo, hi, cap in ((0, CTX, CAP_CTX), (CTX, S, CAP_LAT)):
        vb = jnp.where((lane >= lo) & (lane < hi), bits, -1)
        thr = jnp.zeros((ne, 1), I32)
        for bit in range(30, -1, -1):
            cand = thr | (1 << bit)
            cnt = jnp.sum(jnp.where(vb >= cand, 1.0, 0.0), axis=1, keepdims=True)
            thr = jnp.where(cnt >= cap, cand, thr)
        gt = vb > thr
        eq = jnp.where(vb == thr, 1.0, 0.0)
        need = cap - jnp.sum(jnp.where(gt, 1.0, 0.0), axis=1, keepdims=True)
        eq_rank = _cumsum_lanes(eq) - eq
        s_seg = jnp.where(gt | ((eq > 0.0) & (eq_rank < need)), 1.0, 0.0)
        sel = sel + s_seg
        pos = pos + s_seg * (_cumsum_lanes(s_seg) - s_seg)
    cnt_tok = jnp.broadcast_to(jnp.sum(sel, axis=0, keepdims=True), (ne, S))
    base = _cumsum_lanes(cnt_tok) - cnt_tok
    dest = base + _cumsum_rows_excl(sel)
    base_ref[0] = jnp.concatenate([base[0:1], base[0:1] + cnt_tok[0:1], jnp.zeros((SUBLANE - 2, S), F32)],
                                  axis=0).astype(I32)
    posm = jnp.where(sel > 0.0, pos, -1.0)
    lane_f = lane[0:1].astype(F32)
    idx_hi = jnp.floor(lane_f * (1.0 / 64.0))
    idx_lo = lane_f - 64.0 * idx_hi
    dst_hi = jnp.floor(dest * (1.0 / 128.0))
    dst_lo = dest - 128.0 * dst_hi
    g_hi, g_mid, g_lo = _split3(aff)
    for e in range(ne):
        vt_ref[e] = jnp.concatenate([idx_hi, idx_lo, dst_hi[e:e + 1], dst_lo[e:e + 1], g_hi[e:e + 1],
                                     g_mid[e:e + 1], g_lo[e:e + 1], posm[e:e + 1]], axis=0)
    slot = lax.broadcasted_iota(I32, (LANE, 1), 0).astype(F32)

    def compact(vt, prow, s0):
        onehot = jnp.where(prow == slot + s0, 1.0, 0.0).astype(BF16)
        res = lax.dot_general(vt, onehot, (((1,), (1,)), ((), ())), preferred_element_type=F32)
        return (res[0:1] * 64.0 + res[1:2]).astype(I32), res[4:5] + res[5:6] + res[6:7], \
            (res[2:3] * 128.0 + res[3:4]).astype(I32)

    def per_expert(e, _):
        blk = vt_ref[e]
        prow = blk[SUBLANE - 1:SUBLANE]
        vt = blk.astype(BF16)
        idxc_ref[0, e], gc_ref[0, e], dstc_ref[0, e] = compact(vt[:, :CTX], prow[:, :CTX], 0.0)
        for sc in range(CAP_LAT // LANE):
            cs = slice(sc * LANE, (sc + 1) * LANE)
            idxl_ref[0, e, :, cs], gl_ref[0, e, :, cs], dstl_ref[0, e, :, cs] = compact(
                vt[:, CTX:], prow[:, CTX:], float(sc * LANE))
        return 0

    lax.fori_loop(0, ne, per_expert, 0)


def _router_select(logits):
    ne = N_EXPERTS
    out = lambda n, dt: jax.ShapeDtypeStruct((BATCH, ne, 1, n), dt)
    ospec = lambda n: pl.BlockSpec((1, ne, 1, n), lambda b: (b, 0, 0, 0))
    res = pl.pallas_call(
        _router_select_kernel,
        grid=(BATCH,),
        in_specs=[pl.BlockSpec((1, ne, S), lambda b: (b, 0, 0))],
        out_specs=[ospec(CAP_LAT), ospec(CAP_LAT), ospec(CAP_LAT), ospec(LANE), ospec(LANE), ospec(LANE),
                   pl.BlockSpec((1, SUBLANE, S), lambda b: (b, 0, 0))],
        out_shape=[out(CAP_LAT, I32), out(CAP_LAT, F32), out(CAP_LAT, I32), out(LANE, I32), out(LANE, F32),
                   out(LANE, I32), jax.ShapeDtypeStruct((BATCH, SUBLANE, S), I32)],
        scratch_shapes=[pltpu.VMEM((ne, SUBLANE, S), F32)],
        compiler_params=_cparams(("arbitrary",)),
        name="router_select",
    )(logits)
    return [r.reshape(BATCH, ne, r.shape[-1]) for r in res[:6]] + [res[6]]


ROW_UNROLL = 8


def _start_rows(n_rows, start_one):
    def body(i, _):
        for u in range(ROW_UNROLL):
            start_one(i * ROW_UNROLL + u)
        return 0

    lax.fori_loop(0, n_rows // ROW_UNROLL, body, 0)


def _ffn_up_kernel(src_ref, x_hbm, nw_ref, sh_ref, sc_ref, wg_ref, wu_ref, o_ref, xs_ref, gbuf_ref, sem):
    j = pl.program_id(1)

    @pl.when(j == 0)
    def _():
        def gather(b):
            slot = b % 2
            _start_rows(RPS, lambda r: pltpu.make_async_copy(
                x_hbm.at[pl.ds(src_ref[0, 0, b * RPS + r], 1)], gbuf_ref.at[slot, pl.ds(r, 1)],
                sem.at[slot]).start())

        gather(0)
        for b in range(BATCH):
            slot = b % 2
            if b + 1 < BATCH:
                gather(b + 1)
            pltpu.make_async_copy(x_hbm.at[pl.ds(0, RPS)], gbuf_ref.at[slot], sem.at[slot]).wait()
            is_ctx = lax.broadcasted_iota(I32, (RPS, 1), 0) < CAP_CTX
            y = _rms(gbuf_ref[slot]) * nw_ref[...]
            y = y * (1.0 + _ctx_select(is_ctx, sc_ref[0, b])) + _ctx_select(is_ctx, sh_ref[0, b])
            xs_ref[b * RPS:(b + 1) * RPS, :] = y.astype(BF16)

    xs = xs_ref[...]
    hg = jnp.dot(xs, wg_ref[0].astype(BF16), preferred_element_type=F32)
    hu = jnp.dot(xs, wu_ref[0].astype(BF16), preferred_element_type=F32)
    o_ref[0] = (_silu(hg) * hu).astype(BF16)


def _ffn_up(src_rows, x2d, nw, mt, w_gate, w_up, li):
    tf = 256
    assert RPS % ROW_UNROLL == 0
    return pl.pallas_call(
        _ffn_up_kernel,
        grid=(N_EXPERTS, D_EXPERT // tf),
        in_specs=[
            pl.BlockSpec((1, 1, R_EXP), lambda e, j: (e, 0, 0), memory_space=pltpu.SMEM),
            pl.BlockSpec(memory_space=pl.ANY),
            pl.BlockSpec((1, D), lambda e, j: (0, 0)),
            pl.BlockSpec((1, BATCH, 2, D), lambda e, j: (3, 0, 0, 0)),
            pl.BlockSpec((1, BATCH, 2, D), lambda e, j: (4, 0, 0, 0)),
            pl.BlockSpec((None, 1, D, tf), lambda e, j: (li, e, 0, j)),
            pl.BlockSpec((None, 1, D, tf), lambda e, j: (li, e, 0, j)),
        ],
        out_specs=pl.BlockSpec((1, R_EXP, tf), lambda e, j: (e, 0, j)),
        out_shape=jax.ShapeDtypeStruct((N_EXPERTS, R_EXP, D_EXPERT), BF16),
        scratch_shapes=[pltpu.VMEM((R_EXP, D), BF16), pltpu.VMEM((2, RPS, D), F32),
                        pltpu.SemaphoreType.DMA((2,))],
        compiler_params=_cparams(("arbitrary", "arbitrary")),
        name="ffn_up",
    )(src_rows.reshape(N_EXPERTS, 1, R_EXP), x2d, nw.reshape(1, D), mt, mt, w_gate, w_up)


def _ffn_down_kernel(dst_ref, hid_ref, w_ref, g_ref, y_hbm, ybuf_ref, sem):
    i = pl.program_id(1)
    step = pl.program_id(0) * BATCH + i
    slot = i % 2
    acc = jnp.dot(hid_ref[0], w_ref[0].astype(BF16), preferred_element_type=F32)
    ybuf_ref[slot] = acc * g_ref[0]

    def drain(s):
        pltpu.make_async_copy(ybuf_ref.at[s], y_hbm.at[pl.ds(0, RPS)], sem.at[s]).wait()

    @pl.when(step > 0)
    def _():
        drain(1 - slot)

    _start_rows(RPS, lambda r: pltpu.make_async_copy(
        ybuf_ref.at[slot, pl.ds(r, 1)], y_hbm.at[pl.ds(dst_ref[0, 0, i * RPS + r], 1)], sem.at[slot]).start())

    @pl.when(step == N_EXPERTS * BATCH - 1)
    def _():
        drain(slot)


def _ffn_down(dst_rows, hid, w_down, g_col, li):
    assert BATCH % 2 == 0
    return pl.pallas_call(
        _ffn_down_kernel,
        grid=(N_EXPERTS, BATCH),
        in_specs=[
            pl.BlockSpec((1, 1, R_EXP), lambda e, i: (e, 0, 0), memory_space=pltpu.SMEM),
            pl.BlockSpec((1, RPS, D_EXPERT), lambda e, i: (e, i, 0)),
            pl.BlockSpec((None, 1, D_EXPERT, D), lambda e, i: (li, e, 0, 0)),
            pl.BlockSpec((1, RPS, 1), lambda e, i: (e, i, 0)),
        ],
        out_specs=pl.BlockSpec(memory_space=pl.ANY),
        out_shape=jax.ShapeDtypeStruct((BATCH * PAIRS, D), F32),
        scratch_shapes=[pltpu.VMEM((2, RPS, D), F32), pltpu.SemaphoreType.DMA((2,))],
        compiler_params=_cparams(("arbitrary", "arbitrary")),
        name="ffn_down",
    )(dst_rows.reshape(N_EXPERTS, 1, R_EXP), hid, w_down, g_col)


def _combine_kernel(cs_ref, ce_ref, ys_hbm, x_ref, base_ref, basen_ref, g_ref, o_ref, buf_ref, acc_ref, sem):
    b = pl.program_id(0)
    t = pl.program_id(1)
    lo = cs_ref[b, t]
    hi = ce_ref[b, t]

    def chunk_copy(c):
        return pltpu.make_async_copy(ys_hbm.at[pl.ds(b * PAIRS + c * TT, TT)], buf_ref.at[c % 2], sem.at[c % 2])

    @pl.when(lo < hi)
    def _():
        chunk_copy(lo).start()

    acc_ref[...] = jnp.zeros((TT, D), F32)
    base = base_ref[0]
    basen = basen_ref[0]
    lane = lax.broadcasted_iota(I32, (1, TT), 1)

    def body(c, _):
        @pl.when(c + 1 < hi)
        def _():
            chunk_copy(c + 1).start()

        chunk_copy(c).wait()
        r = c * TT + lane
        onehot = jnp.where((base <= r) & (r < basen), 1.0, 0.0).astype(BF16)
        y = buf_ref[c % 2]
        y_hi = y.astype(BF16)
        y_lo = (y - y_hi.astype(F32)).astype(BF16)
        acc_ref[...] += (jnp.dot(onehot, y_hi, preferred_element_type=F32)
                         + jnp.dot(onehot, y_lo, preferred_element_type=F32))
        return 0

    lax.fori_loop(lo, hi, body, 0)
    gate = jnp.where(t == 0, g_ref[0, 0, 0:1, :], g_ref[0, 0, 1:2, :])
    o_ref[0] = x_ref[0] + gate * acc_ref[...]


def _combine(cs, ce, ys, x, base_col, basen_col, mt):
    grid_spec = pltpu.PrefetchScalarGridSpec(
        num_scalar_prefetch=2,
        grid=(BATCH, N_TT),
        in_specs=[
            pl.BlockSpec(memory_space=pl.ANY),
            pl.BlockSpec((1, TT, D), lambda b, t, *_: (b, t, 0)),
            pl.BlockSpec((1, TT, 1), lambda b, t, *_: (b, t, 0)),
            pl.BlockSpec((1, TT, 1), lambda b, t, *_: (b, t, 0)),
            pl.BlockSpec((1, 1, 2, D), lambda b, t, *_: (5, b, 0, 0)),
        ],
        out_specs=pl.BlockSpec((1, TT, D), lambda b, t, *_: (b, t, 0)),
        scratch_shapes=[pltpu.VMEM((2, TT, D), F32), pltpu.VMEM((TT, D), F32), pltpu.SemaphoreType.DMA((2,))],
    )
    return pl.pallas_call(
        _combine_kernel,
        grid_spec=grid_spec,
        out_shape=jax.ShapeDtypeStruct((BATCH, S, D), F32),
        compiler_params=_cparams(("arbitrary", "arbitrary")),
        name="moe_combine",
    )(cs, ce, ys, x, base_col, basen_col, mt)


def _moe(x, mt, norm_w, w_router, w_gate, w_up, w_down, li):
    logits = _router_logits(x, norm_w, mt, w_router)
    idx_l, g_l, dst_l, idx_c, g_c, dst_c, bases = _router_select(logits)
    boff = jnp.arange(BATCH, dtype=I32)[:, None, None]

    def rows(c, l, off):
        r = jnp.concatenate([c[:, :, :CAP_CTX], l], axis=2) + off
        return jnp.transpose(r, (1, 0, 2)).reshape(N_EXPERTS, R_EXP)

    src_rows = rows(idx_c, idx_l, boff * S)
    dst_rows = rows(dst_c, dst_l, boff * PAIRS)
    g_col = rows(g_c, g_l, 0.0).reshape(N_EXPERTS, R_EXP, 1)
    hid = _ffn_up(src_rows, x.reshape(BATCH * S, D), norm_w, mt, w_gate, w_up, li)
    ys = _ffn_down(dst_rows, hid, w_down, g_col, li)
    base, basen = bases[:, 0, :], bases[:, 1, :]
    cs = base[:, ::TT] // TT
    ce = (basen[:, TT - 1::TT] + TT - 1) // TT
    return _combine(cs, ce, ys, x, base.reshape(BATCH, S, 1), basen.reshape(BATCH, S, 1), mt)


def kernel(x, c, ctx, c_ctx, ada_w, ada_b, norm_mix_w, norm_ffn_w,
           ev_w_in, ev_conv_w, ev_conv_b, ev_ra_w, ev_ra_b, ev_ix_w, ev_ix_b, ev_lambda, ev_sink, ev_w_out,
           od_w_in, od_conv_w, od_conv_b, od_gate_b, od_hnorm_w, od_w_out,
           moe_router, moe_w_gate, moe_w_up, moe_w_down, final_norm_w):
    assert x.shape == (BATCH, SEQ, D) and ctx.shape == (BATCH, CTX, D)
    cc = jnp.zeros((SUBLANE, D), F32).at[:BATCH].set(c).at[BATCH].set(c_ctx)
    mod = _modulation(cc, ada_w, ada_b)
    rope = _rope_tables()
    xs = jnp.concatenate([ctx, x], axis=1)
    for layer in range(DEPTH):
        mt = _mod_table(mod[layer])
        i = layer // 2
        if layer % 2 == 0:
            xs = _even_mixer(xs, mt, norm_mix_w[layer], ev_w_in, i, ev_conv_w[i], ev_conv_b[i], ev_ra_w[i],
                             ev_ra_b[i], ev_ix_w[i], ev_ix_b[i], ev_lambda[i], ev_sink[i], ev_w_out, rope)
        else:
            xs = _odd_mixer(xs, mt, norm_mix_w[layer], od_w_in, i, od_conv_w[i], od_conv_b[i], od_gate_b[i],
                            od_hnorm_w[i], od_w_out)
        xs = _moe(xs, mt, norm_ffn_w[layer], moe_router[layer], moe_w_gate, moe_w_up, moe_w_down, layer)
    return _final_norm(xs, final_norm_w)
```

```python
import functools

import jax
import jax.numpy as jnp
from jax import lax
from jax.experimental import pallas as pl
from jax.experimental.pallas import tpu as pltpu

F32 = jnp.float32
BF16 = jnp.bfloat16
I32 = jnp.int32

D = 2048
BATCH = 4
SEQ = 4096
CTX = 256
S = CTX + SEQ
DEPTH = 4
N_MOD = 6
EPS = 1e-6
GRID_W = 64

LRU_W = 1024
LRU_BLOCKS = 8
LRU_BLOCK = 128
LRU_C = 8.0
CONV_W = 4
ATT_HEADS = 8
KV_HEADS = 2
GROUP = ATT_HEADS // KV_HEADS
HEAD_DIM = 128
ATT_W = ATT_HEADS * HEAD_DIM
KV_W = KV_HEADS * HEAD_DIM
WINDOW = 128
ROPE_PAIRS = HEAD_DIM // 4
ROPE_BASE = 10000.0
EVEN_IN = 2 * LRU_W + ATT_W + 2 * KV_W

M_HEADS = 8
M_DK = 128
M_DV = 256
M_QK = M_HEADS * M_DK
M_V = M_HEADS * M_DV
ODD_MAIN = 2 * M_QK + 2 * M_V
N_GATES = 4 * M_HEADS

N_EXPERTS = 16
EC_FACTOR = 2
D_EXPERT = 1536
CAP_LAT = EC_FACTOR * SEQ // N_EXPERTS
CAP_CTX = EC_FACTOR * CTX // N_EXPERTS
RPS = CAP_CTX + CAP_LAT
R_EXP = BATCH * RPS
PAIRS = N_EXPERTS * RPS

LANE = 128
SUBLANE = 8
VMEM_LIMIT = 56 * 1024 * 1024
TM = S // 4
TILES_PER_SAMPLE = S // TM
TN = 512
TT = 256
N_TT = S // TT
CHUNK = 128
N_CHUNK = S // CHUNK
CTX_CHUNKS = CTX // CHUNK


def _cparams(sem, vmem=VMEM_LIMIT):
    return pltpu.CompilerParams(dimension_semantics=sem, vmem_limit_bytes=vmem)


def _sigmoid(x):
    return 1.0 / (1.0 + jnp.exp(-x))


def _silu(x):
    return x * _sigmoid(x)


def _softplus(x):
    return jnp.maximum(x, 0.0) + jnp.log1p(jnp.exp(-jnp.abs(x)))


def _log_sigmoid(x):
    return -_softplus(-x)


def _gelu_tanh(x):
    return 0.5 * x * (1.0 + jnp.tanh(0.7978845608028654 * (x + 0.044715 * (x * x * x))))


def _rms(x):
    return x * lax.rsqrt(jnp.mean(x * x, axis=-1, keepdims=True) + EPS)


def _ctx_select(is_ctx, tab):
    return jnp.where(is_ctx, tab[0:1, :], tab[1:2, :])


def _mod_kernel(c_ref, w_ref, b_ref, o_ref):
    a = _silu(c_ref[...]).astype(BF16)
    o_ref[0] = jnp.dot(a, w_ref[0].astype(BF16), preferred_element_type=F32) + b_ref[0]


def _modulation(cc, ada_w, ada_b):
    tn = 1024
    return pl.pallas_call(
        _mod_kernel,
        grid=(DEPTH, N_MOD * D // tn),
        in_specs=[
            pl.BlockSpec((SUBLANE, D), lambda l, j: (0, 0)),
            pl.BlockSpec((1, D, tn), lambda l, j: (l, 0, j)),
            pl.BlockSpec((1, 1, tn), lambda l, j: (l, 0, j)),
        ],
        out_specs=pl.BlockSpec((1, SUBLANE, tn), lambda l, j: (l, 0, j)),
        out_shape=jax.ShapeDtypeStruct((DEPTH, SUBLANE, N_MOD * D), F32),
        compiler_params=_cparams(("arbitrary", "arbitrary")),
        name="adaln_mod",
    )(cc, ada_w, ada_b.reshape(DEPTH, 1, N_MOD * D))


def _mod_table(mod_layer):
    m = mod_layer.reshape(SUBLANE, N_MOD, D)
    lat = jnp.transpose(m[:BATCH], (1, 0, 2))
    ctx = jnp.broadcast_to(m[BATCH][:, None, :], (N_MOD, BATCH, D))
    return jnp.stack([ctx, lat], axis=2)


def _norm_mod_rows(x, nw, sh_tab, sc_tab, row0):
    rows = x.shape[0]
    is_ctx = (row0 + lax.broadcasted_iota(I32, (rows, 1), 0)) < CTX
    y = _rms(x) * nw
    return y * (1.0 + _ctx_select(is_ctx, sc_tab)) + _ctx_select(is_ctx, sh_tab)


def _in_proj_kernel(x_ref, nw_ref, sh_ref, sc_ref, w_ref, o_ref, xn_ref):
    i = pl.program_id(0)
    j = pl.program_id(1)

    @pl.when(j == 0)
    def _():
        row0 = (i % TILES_PER_SAMPLE) * TM
        xn_ref[...] = _norm_mod_rows(x_ref[0], nw_ref[...], sh_ref[0, 0], sc_ref[0, 0], row0).astype(BF16)

    o_ref[0] = jnp.dot(xn_ref[...], w_ref[...].astype(BF16), preferred_element_type=F32)


def _in_proj(x, nw, mt, w, li, n_out):
    return pl.pallas_call(
        _in_proj_kernel,
        grid=(BATCH * TILES_PER_SAMPLE, n_out // TN),
        in_specs=[
            pl.BlockSpec((1, TM, D), lambda i, j: (i // TILES_PER_SAMPLE, i % TILES_PER_SAMPLE, 0)),
            pl.BlockSpec((1, D), lambda i, j: (0, 0)),
            pl.BlockSpec((1, 1, 2, D), lambda i, j: (0, i // TILES_PER_SAMPLE, 0, 0)),
            pl.BlockSpec((1, 1, 2, D), lambda i, j: (1, i // TILES_PER_SAMPLE, 0, 0)),
            pl.BlockSpec((None, D, TN), lambda i, j: (li, 0, j)),
        ],
        out_specs=pl.BlockSpec((1, TM, TN), lambda i, j: (i // TILES_PER_SAMPLE, i % TILES_PER_SAMPLE, j)),
        out_shape=jax.ShapeDtypeStruct((BATCH, S, n_out), F32),
        scratch_shapes=[pltpu.VMEM((TM, D), BF16)],
        compiler_params=_cparams(("arbitrary", "arbitrary")),
        name="in_proj",
    )(x, nw.reshape(1, D), mt, mt, w)


def _gate_proj_kernel(x_ref, nw_ref, sh_ref, sc_ref, w_ref, o_ref):
    t = pl.program_id(1)
    xn = _norm_mod_rows(x_ref[0], nw_ref[...], sh_ref[0, 0], sc_ref[0, 0], t * TT).astype(BF16)
    o_ref[0] = jnp.dot(xn, w_ref[...].astype(BF16), preferred_element_type=F32)


def _gate_proj(x, nw, mt, wg):
    return pl.pallas_call(
        _gate_proj_kernel,
        grid=(BATCH, N_TT),
        in_specs=[
            pl.BlockSpec((1, TT, D), lambda b, t: (b, t, 0)),
            pl.BlockSpec((1, D), lambda b, t: (0, 0)),
            pl.BlockSpec((1, 1, 2, D), lambda b, t: (0, b, 0, 0)),
            pl.BlockSpec((1, 1, 2, D), lambda b, t: (1, b, 0, 0)),
            pl.BlockSpec((D, LANE), lambda b, t: (0, 0)),
        ],
        out_specs=pl.BlockSpec((1, TT, LANE), lambda b, t: (b, t, 0)),
        out_shape=jax.ShapeDtypeStruct((BATCH, S, LANE), F32),
        compiler_params=_cparams(("arbitrary", "arbitrary")),
        name="gate_proj",
    )(x, nw.reshape(1, D), mt, mt, wg)


def _residual_epilogue(i, x_ref, g_ref, acc, o_ref):
    tm = acc.shape[0]
    row0 = (i % (S // tm)) * tm
    is_ctx = (row0 + lax.broadcasted_iota(I32, (tm, 1), 0)) < CTX
    o_ref[0] = x_ref[0] + _ctx_select(is_ctx, g_ref[0, 0]) * acc


def _out_proj_even_kernel(a1_ref, a2_ref, x_ref, g_ref, w_ref, o_ref, a_ref):
    i = pl.program_id(0)
    j = pl.program_id(1)

    @pl.when(j == 0)
    def _():
        a_ref[:, :LRU_W] = a1_ref[0].astype(BF16)
        a_ref[:, LRU_W:] = a2_ref[0].astype(BF16)

    acc = jnp.dot(a_ref[...], w_ref[...].astype(BF16), preferred_element_type=F32)
    _residual_epilogue(i, x_ref, g_ref, acc, o_ref)


def _out_proj_odd_kernel(h_ref, og_ref, hw_ref, x_ref, g_ref, w_ref, o_ref, a_ref):
    i = pl.program_id(0)
    j = pl.program_id(1)

    @pl.when(j == 0)
    def _():
        for h in range(M_HEADS):
            sl = slice(h * M_DV, (h + 1) * M_DV)
            hn = _rms(h_ref[0, :, sl]) * hw_ref[:, sl] * _sigmoid(og_ref[0, :, sl])
            a_ref[:, sl] = hn.astype(BF16)

    acc = jnp.dot(a_ref[...], w_ref[...].astype(BF16), preferred_element_type=F32)
    _residual_epilogue(i, x_ref, g_ref, acc, o_ref)


def _row_spec(tm, width, col=None):
    tps = S // tm
    if col is None:
        return pl.BlockSpec((1, tm, width), lambda i, j: (i // tps, i % tps, j))
    return pl.BlockSpec((1, tm, width), lambda i, j: (i // tps, i % tps, col))


def _out_proj_even(a1, a2, x, mt, w, li):
    tm = TM
    return pl.pallas_call(
        _out_proj_even_kernel,
        grid=(BATCH * S // tm, D // TN),
        in_specs=[
            _row_spec(tm, LRU_W, 0),
            _row_spec(tm, ATT_W, 0),
            _row_spec(tm, TN),
            pl.BlockSpec((1, 1, 2, TN), lambda i, j: (2, i // (S // tm), 0, j)),
            pl.BlockSpec((None, D, TN), lambda i, j: (li, 0, j)),
        ],
        out_specs=_row_spec(tm, TN),
        out_shape=jax.ShapeDtypeStruct((BATCH, S, D), F32),
        scratch_shapes=[pltpu.VMEM((tm, D), BF16)],
        compiler_params=_cparams(("arbitrary", "arbitrary")),
        name="out_proj_even",
    )(a1, a2, x, mt, w)


def _out_proj_odd(h, p_odd, hnorm_w, x, mt, w, li):
    tm = TM // 2
    return pl.pallas_call(
        _out_proj_odd_kernel,
        grid=(BATCH * S // tm, D // TN),
        in_specs=[
            _row_spec(tm, M_V, 0),
            _row_spec(tm, M_V, (2 * M_QK + M_V) // M_V),
            pl.BlockSpec((1, M_V), lambda i, j: (0, 0)),
            _row_spec(tm, TN),
            pl.BlockSpec((1, 1, 2, TN), lambda i, j: (2, i // (S // tm), 0, j)),
            pl.BlockSpec((None, D, TN), lambda i, j: (li, 0, j)),
        ],
        out_specs=_row_spec(tm, TN),
        out_shape=jax.ShapeDtypeStruct((BATCH, S, D), F32),
        scratch_shapes=[pltpu.VMEM((tm, D), BF16)],
        compiler_params=_cparams(("arbitrary", "arbitrary")),
        name="out_proj_odd",
    )(h, p_odd, hnorm_w.reshape(1, M_V), x, mt, w)


def _final_norm_kernel(x_ref, w_ref, o_ref):
    o_ref[0] = _rms(x_ref[0]) * w_ref[...]


def _final_norm(x, w):
    return pl.pallas_call(
        _final_norm_kernel,
        grid=(BATCH, SEQ // TT),
        in_specs=[
            pl.BlockSpec((1, TT, D), lambda b, t: (b, t + CTX // TT, 0)),
            pl.BlockSpec((1, D), lambda b, t: (0, 0)),
        ],
        out_specs=pl.BlockSpec((1, TT, D), lambda b, t: (b, t, 0)),
        out_shape=jax.ShapeDtypeStruct((BATCH, SEQ, D), F32),
        compiler_params=_cparams(("arbitrary", "arbitrary")),
        name="final_norm",
    )(x, w.reshape(1, D))


def _seg_conv(x, cw, cb):
    n = x.shape[0]
    row = lax.broadcasted_iota(I32, (n, 1), 0)
    seg = row < CTX
    y = cb
    for j in range(CONV_W):
        off = j - CONV_W // 2
        if off == 0:
            tap = x
        else:
            src = row + off
            ok = (src >= 0) & (src < n) & ((src < CTX) == seg)
            tap = jnp.where(ok, pltpu.roll(x, (-off) % n, axis=0), 0.0)
        y = y + tap * cw[j:j + 1, :]
    return y


def _bwd_chunk(j):
    return jnp.where(j < CTX_CHUNKS, CTX_CHUNKS - 1 - j, N_CHUNK + CTX_CHUNKS - 1 - j)


def _lin_scan(a, b, reverse):
    t_len = a.shape[0]
    row = lax.broadcasted_iota(I32, a.shape, 0)
    k = 1
    while k < t_len:
        shift = t_len - k if reverse else k
        ok = (row < t_len - k) if reverse else (row >= k)
        a_s = pltpu.roll(a, shift, axis=0)
        b_s = pltpu.roll(b, shift, axis=0)
        b = jnp.where(ok, a * b_s + b, b)
        a = jnp.where(ok, a * a_s, a)
        k *= 2
    return a, b


def _lru_kernel(xa_ref, ya_ref, cw_ref, cb_ref, raw_ref, rab_ref, ixw_ref, ixb_ref, lam_ref, o_ref, xc_ref, hf_ref):
    xc_ref[...] = _seg_conv(xa_ref[0], cw_ref[...], cb_ref[...])

    def gates(x, d):
        xb = x.astype(BF16)
        r = _sigmoid(jnp.dot(xb, raw_ref[d, 0].astype(BF16), preferred_element_type=F32) + rab_ref[d, 0])
        i = _sigmoid(jnp.dot(xb, ixw_ref[d, 0].astype(BF16), preferred_element_type=F32) + ixb_ref[d, 0])
        log_a = (-LRU_C * r) * _softplus(-lam_ref[d, 0])
        a = jnp.exp(log_a)
        return a, jnp.sqrt(-jnp.tanh(log_a) * (a * a + 1.0)) * (i * x)

    hf_ref[...] = jnp.zeros((S, LRU_BLOCK), F32)

    def one(chunk, h, d):
        rows = pl.ds(pl.multiple_of(chunk * CHUNK, CHUNK), CHUNK)
        a, b = gates(xc_ref[rows, :], d)
        a_c, b_c = _lin_scan(a, b, d == 1)
        h_all = b_c + a_c * h
        hf_ref[rows, :] += h_all
        return h_all[0:1, :] if d == 1 else h_all[CHUNK - 1:CHUNK, :]

    def body(j, carry):
        return one(j, carry[0], 0), one(_bwd_chunk(j), carry[1], 1)

    zero = jnp.zeros((1, LRU_BLOCK), F32)
    lax.fori_loop(0, N_CHUNK, body, (zero, zero))
    o_ref[0] = hf_ref[...] * _gelu_tanh(ya_ref[0])


def _lru(p_even, conv_w, conv_b, ra_w, ra_b, ix_w, ix_b, lam):
    nb = LRU_BLOCKS
    blk4 = lambda: pl.BlockSpec((2, 1, LRU_BLOCK, LRU_BLOCK), lambda b, k: (0, k, 0, 0))
    vec4 = lambda: pl.BlockSpec((2, 1, 1, LRU_BLOCK), lambda b, k: (0, k, 0, 0))
    return pl.pallas_call(
        _lru_kernel,
        grid=(BATCH, nb),
        in_specs=[
            pl.BlockSpec((1, S, LRU_BLOCK), lambda b, k: (b, 0, k)),
            pl.BlockSpec((1, S, LRU_BLOCK), lambda b, k: (b, 0, nb + k)),
            pl.BlockSpec((CONV_W, LRU_BLOCK), lambda b, k: (0, k)),
            pl.BlockSpec((1, LRU_BLOCK), lambda b, k: (0, k)),
            blk4(), vec4(), blk4(), vec4(), vec4(),
        ],
        out_specs=pl.BlockSpec((1, S, LRU_BLOCK), lambda b, k: (b, 0, k)),
        out_shape=jax.ShapeDtypeStruct((BATCH, S, LRU_W), F32),
        scratch_shapes=[pltpu.VMEM((S, LRU_BLOCK), F32), pltpu.VMEM((S, LRU_BLOCK), F32)],
        compiler_params=_cparams(("arbitrary", "arbitrary")),
        name="rglru",
    )(p_even, p_even, conv_w, conv_b.reshape(1, LRU_W), ra_w, ra_b.reshape(2, nb, 1, LRU_BLOCK),
      ix_w, ix_b.reshape(2, nb, 1, LRU_BLOCK), lam.reshape(2, nb, 1, LRU_BLOCK))


def _rope(x, cos, sin):
    lane = lax.broadcasted_iota(I32, (1, HEAD_DIM), 1)
    first = (lane % (2 * ROPE_PAIRS)) < ROPE_PAIRS
    swapped = jnp.where(first, pltpu.roll(x, HEAD_DIM - ROPE_PAIRS, axis=1), pltpu.roll(x, ROPE_PAIRS, axis=1))
    return x * cos + swapped * sin


def _attn_kernel(sink_ref, q_ref, kp_ref, ko_ref, kn_ref, vp_ref, vo_ref, vn_ref, ck_ref, cv_ref,
                 cq_ref, sq_ref, cp_ref, sp_ref, cn_ref, sn_ref, o_ref):
    t = pl.program_id(1)
    nq = CHUNK
    lat = t >= CTX_CHUNKS
    c_lo = jnp.where(lat, jnp.where(t > CTX_CHUNKS, 0, nq), 0)
    c_hi = jnp.where(lat, jnp.where(t < N_CHUNK - 1, 3 * nq, 2 * nq), 0)
    r = lax.broadcasted_iota(I32, (nq, CTX + 3 * nq), 0)
    c = lax.broadcasted_iota(I32, (nq, CTX + 3 * nq), 1) - CTX
    band_ok = (jnp.abs(c - nq - r) <= WINDOW) & (c >= c_lo) & (c < c_hi)
    bias = jnp.where((c < 0) | band_ok, 0.0, -jnp.inf)
    bias = jnp.concatenate([bias] * GROUP, axis=0)
    scale = HEAD_DIM ** -0.5
    for g in range(KV_HEADS):
        ks = slice(g * HEAD_DIM, (g + 1) * HEAD_DIM)
        keys = jnp.concatenate([
            ck_ref[0, :, ks],
            _rope(kp_ref[0, :, ks], cp_ref[...], sp_ref[...]),
            _rope(ko_ref[0, :, ks], cq_ref[...], sq_ref[...]),
            _rope(kn_ref[0, :, ks], cn_ref[...], sn_ref[...]),
        ], axis=0).astype(BF16)
        vals = jnp.concatenate([cv_ref[0, :, ks], vp_ref[0, :, ks], vo_ref[0, :, ks], vn_ref[0, :, ks]],
                               axis=0).astype(BF16)
        heads = [g * GROUP + hh for hh in range(GROUP)]
        qg = jnp.concatenate([_rope(q_ref[0, :, h * HEAD_DIM:(h + 1) * HEAD_DIM], cq_ref[...], sq_ref[...])
                              for h in heads], axis=0).astype(BF16)
        sink = jnp.concatenate([jnp.full((nq, 1), sink_ref[h], F32) for h in heads], axis=0)
        s = lax.dot_general(qg, keys, (((1,), (1,)), ((), ())), preferred_element_type=F32) * scale + bias
        m = jnp.maximum(jnp.max(s, axis=-1, keepdims=True), sink)
        p = jnp.exp(s - m)
        den = jnp.sum(p, axis=-1, keepdims=True) + jnp.exp(sink - m)
        out = jnp.dot((p / den).astype(BF16), vals, preferred_element_type=F32)
        for hh, h in enumerate(heads):
            o_ref[0, :, h * HEAD_DIM:(h + 1) * HEAD_DIM] = out[hh * nq:(hh + 1) * nq]


def _attention(p_even, sink, cos_t, sin_t):
    qc = 2 * LRU_W // ATT_W
    kc = (2 * LRU_W + ATT_W) // KV_W
    vc = kc + 1
    lo, hi = CTX_CHUNKS, N_CHUNK - 1
    prev = lambda t: jnp.clip(t - 1, lo, hi)
    nxt = lambda t: jnp.clip(t + 1, lo, hi)
    own = lambda t: t
    kv = lambda col, f: pl.BlockSpec((1, CHUNK, KV_W), lambda b, t: (b, f(t), col))
    tab = lambda f: pl.BlockSpec((CHUNK, HEAD_DIM), lambda b, t: (f(t), 0))
    return pl.pallas_call(
        _attn_kernel,
        grid=(BATCH, N_CHUNK),
        in_specs=[
            pl.BlockSpec(memory_space=pltpu.SMEM),
            pl.BlockSpec((1, CHUNK, ATT_W), lambda b, t: (b, t, qc)),
            kv(kc, prev), kv(kc, own), kv(kc, nxt),
            kv(vc, prev), kv(vc, own), kv(vc, nxt),
            pl.BlockSpec((1, CTX, KV_W), lambda b, t: (b, 0, kc)),
            pl.BlockSpec((1, CTX, KV_W), lambda b, t: (b, 0, vc)),
            tab(own), tab(own), tab(prev), tab(prev), tab(nxt), tab(nxt),
        ],
        out_specs=pl.BlockSpec((1, CHUNK, ATT_W), lambda b, t: (b, t, 0)),
        out_shape=jax.ShapeDtypeStruct((BATCH, S, ATT_W), F32),
        compiler_params=_cparams(("arbitrary", "arbitrary")),
        name="window_attention",
    )(sink, p_even, p_even, p_even, p_even, p_even, p_even, p_even, p_even, p_even,
      cos_t, sin_t, cos_t, sin_t, cos_t, sin_t)


def _rope_tables():
    inv = jnp.power(ROPE_BASE, -jnp.arange(ROPE_PAIRS, dtype=F32) / ROPE_PAIRS)
    pos = jnp.arange(SEQ)
    row_ang = (pos // GRID_W).astype(F32)[:, None] * inv
    col_ang = (pos % GRID_W).astype(F32)[:, None] * inv
    cos = jnp.concatenate([jnp.cos(row_ang)] * 2 + [jnp.cos(col_ang)] * 2, axis=-1)
    sin = jnp.concatenate([-jnp.sin(row_ang), jnp.sin(row_ang), -jnp.sin(col_ang), jnp.sin(col_ang)], axis=-1)
    cos = jnp.concatenate([jnp.ones((CTX, HEAD_DIM), F32), cos], axis=0)
    sin = jnp.concatenate([jnp.zeros((CTX, HEAD_DIM), F32), sin], axis=0)
    return cos, sin


def _mlstm_local(qb, kb, k, vb, li_row, b_row, reverse):
    ln = qb.shape[0]
    row = lax.broadcasted_iota(I32, (ln, ln), 0)
    col = lax.broadcasted_iota(I32, (ln, ln), 1)
    b_r = jnp.broadcast_to(b_row, (ln, ln))
    li_r = jnp.broadcast_to(li_row, (ln, ln))
    b = b_r.T
    li = li_r.T
    causal = (col >= row) if reverse else (col <= row)
    logw = jnp.where(causal, b - b_r + li_r, -jnp.inf)
    m_loc = jnp.max(logw, axis=1, keepdims=True)
    s = lax.dot_general(qb, kb, (((1,), (1,)), ((), ())), preferred_element_type=F32) * jnp.exp(logw - m_loc)
    num = jnp.dot(s.astype(BF16), vb, preferred_element_type=F32)
    den = jnp.sum(s, axis=1, keepdims=True)
    edge = 0 if reverse else ln - 1
    b_last = b[edge:edge + 1, :]
    log_u = b_last - b + li
    mu_loc = jnp.max(log_u, axis=0, keepdims=True)
    uk = jnp.exp(log_u - mu_loc) * k
    kv = lax.dot_general(uk.astype(BF16), vb, (((0,), (0,)), ((), ())), preferred_element_type=F32)
    return dict(qb=qb, b=b, m_loc=m_loc, num=num, den=den, b_last=b_last, mu_loc=mu_loc, kv=kv,
                ksum=jnp.sum(uk, axis=0, keepdims=True))


def _mlstm_apply(loc, state):
    c0, n0, m0 = state
    b, m_loc = loc["b"], loc["m_loc"]
    m = jnp.maximum(b + m0, m_loc)
    intra = jnp.exp(m_loc - m)
    inter = jnp.exp(b + m0 - m)
    qc = jnp.dot(loc["qb"], c0.astype(BF16), preferred_element_type=F32)
    n8 = jnp.broadcast_to(n0, (SUBLANE, n0.shape[1])).astype(BF16)
    qn = lax.dot_general(loc["qb"], n8, (((1,), (1,)), ((), ())), preferred_element_type=F32)[:, :1]
    num = jnp.concatenate([intra, intra], axis=1) * loc["num"] + jnp.concatenate([inter, inter], axis=1) * qc
    den = intra[:, :1] * loc["den"] + inter[:, :1] * qn
    h = num / jnp.maximum(jnp.abs(den), jnp.exp(-m[:, :1]))
    m_new = jnp.maximum(loc["b_last"] + m0, loc["mu_loc"])
    decay = jnp.exp(loc["b_last"] + m0 - m_new)
    grow = jnp.exp(loc["mu_loc"] - m_new)
    c_new = decay[:, :1] * c0 + grow[:, :1] * loc["kv"]
    n_new = decay * n0 + grow * loc["ksum"]
    return h, (c_new, n_new, m_new)


MLSTM_UNROLL = 2


def _mlstm_kernel(gb_ref, q_ref, k_ref, v_ref, g_ref, cwq_ref, cbq_ref, cwk_ref, cbk_ref, o_ref,
                  qb_ref, kb_ref, kc_ref, vb_ref, gr_ref):
    hd = pl.program_id(1)
    ln = CHUNK
    qb_ref[...] = (_silu(_seg_conv(q_ref[0], cwq_ref[...], cbq_ref[...])) * (M_DK ** -0.5)).astype(BF16)
    kc = _silu(_seg_conv(k_ref[0], cwk_ref[...], cbk_ref[...]))
    kc_ref[...] = kc
    kb_ref[...] = kc.astype(BF16)
    vb_ref[...] = v_ref[0].astype(BF16)
    o_ref[0] = jnp.zeros((S, M_DV), F32)

    def rows_of(chunk):
        return pl.ds(pl.multiple_of(chunk * ln, ln), ln)

    sub = lax.broadcasted_iota(I32, (N_GATES, 1), 0)

    def gate_rows(c, _):
        gt = g_ref[0, rows_of(c), :].T[:N_GATES]
        pick = lambda ty: jnp.sum(jnp.where(sub == ty * M_HEADS + hd, gt, 0.0), axis=0, keepdims=True) \
            + gb_ref[ty, hd]
        gr_ref[c] = jnp.concatenate([pick(0), _log_sigmoid(pick(1)), pick(2), _log_sigmoid(pick(3)),
                                     jnp.zeros((SUBLANE - 4, ln), F32)], axis=0)
        return 0

    lax.fori_loop(0, N_CHUNK, gate_rows, 0)
    gr = gr_ref[...]
    lane = lax.broadcasted_iota(I32, gr.shape, 2)
    kind = lax.broadcasted_iota(I32, gr.shape, 1)
    pre = suf = gr
    kk = 1
    while kk < ln:
        pre = pre + jnp.where(lane >= kk, pltpu.roll(pre, kk, axis=2), 0.0)
        suf = suf + jnp.where(lane < ln - kk, pltpu.roll(suf, ln - kk, axis=2), 0.0)
        kk *= 2
    gr_ref[...] = jnp.where(kind == 1, pre, jnp.where(kind == 3, suf, gr))

    def local(chunk, d):
        rows = rows_of(chunk)
        gr = gr_ref[chunk]
        return _mlstm_local(qb_ref[rows, :], kb_ref[rows, :], kc_ref[rows, :], vb_ref[rows, :],
                            gr[2 * d:2 * d + 1], gr[2 * d + 1:2 * d + 2], d == 1)

    def body(i, carry):
        steps = [i * MLSTM_UNROLL + u for u in range(MLSTM_UNROLL)]
        chunks = ([j for j in steps], [_bwd_chunk(j) for j in steps])
        locs = [[local(c, d) for c in chunks[d]] for d in range(2)]
        states = list(carry)
        for u in range(MLSTM_UNROLL):
            for d in range(2):
                h, states[d] = _mlstm_apply(locs[d][u], states[d])
                o_ref[0, rows_of(chunks[d][u]), :] += h
        return tuple(states)

    zero = (jnp.zeros((M_DK, M_DV), F32), jnp.zeros((1, M_DK), F32), jnp.zeros((1, ln), F32))
    lax.fori_loop(0, N_CHUNK // MLSTM_UNROLL, body, (zero, zero))


def _mlstm(p_odd, gates, conv_w, conv_b, gate_b):
    nh = M_HEADS
    cw = lambda off: pl.BlockSpec((CONV_W, M_DK), lambda b, h: (0, off + h))
    cb = lambda off: pl.BlockSpec((1, M_DK), lambda b, h: (0, off + h))
    conv_b = conv_b.reshape(1, 2 * M_QK)
    assert N_CHUNK % MLSTM_UNROLL == 0 and CHUNK == M_DK
    return pl.pallas_call(
        _mlstm_kernel,
        grid=(BATCH, nh),
        in_specs=[
            pl.BlockSpec(memory_space=pltpu.SMEM),
            pl.BlockSpec((1, S, M_DK), lambda b, h: (b, 0, h)),
            pl.BlockSpec((1, S, M_DK), lambda b, h: (b, 0, nh + h)),
            pl.BlockSpec((1, S, M_DV), lambda b, h: (b, 0, 2 * M_QK // M_DV + h)),
            pl.BlockSpec((1, S, LANE), lambda b, h: (b, 0, 0)),
            cw(0), cb(0), cw(nh), cb(nh),
        ],
        out_specs=pl.BlockSpec((1, S, M_DV), lambda b, h: (b, 0, h)),
        out_shape=jax.ShapeDtypeStruct((BATCH, S, M_V), F32),
        scratch_shapes=[pltpu.VMEM((S, M_DK), BF16), pltpu.VMEM((S, M_DK), BF16), pltpu.VMEM((S, M_DK), F32),
                        pltpu.VMEM((S, M_DV), BF16), pltpu.VMEM((N_CHUNK, SUBLANE, CHUNK), F32)],
        compiler_params=_cparams(("arbitrary", "arbitrary")),
        name="mlstm",
    )(gate_b, p_odd, p_odd, p_odd, gates, conv_w, conv_b, conv_w, conv_b)


def _even_mixer(x, mt, norm_w, w_in, li, conv_w, conv_b, ra_w, ra_b, ix_w, ix_b, lam, sink, w_out, rope):
    p = _in_proj(x, norm_w, mt, w_in, li, EVEN_IN)
    a = _lru(p, conv_w, conv_b, ra_w, ra_b, ix_w, ix_b, lam)
    b = _attention(p, sink, rope[0], rope[1])
    return _out_proj_even(a, b, x, mt, w_out, li)


def _odd_mixer(x, mt, norm_w, w_in, li, conv_w, conv_b, gate_b, hnorm_w, w_out):
    p = _in_proj(x, norm_w, mt, w_in, li, ODD_MAIN)
    wg = jnp.pad(w_in[li, :, ODD_MAIN:], ((0, 0), (0, LANE - N_GATES)))
    gates = _gate_proj(x, norm_w, mt, wg)
    h = _mlstm(p, gates, conv_w, conv_b, gate_b)
    return _out_proj_odd(h, p, hnorm_w, x, mt, w_out, li)


def _router_logits_kernel(x_ref, nw_ref, sh_ref, sc_ref, w_ref, o_ref):
    t = pl.program_id(1)
    h = _norm_mod_rows(x_ref[0], nw_ref[...], sh_ref[0, 0], sc_ref[0, 0], t * TT)
    o_ref[0] = lax.dot_general(w_ref[...], h, (((1,), (1,)), ((), ())),
                               precision=lax.Precision.HIGHEST, preferred_element_type=F32)


def _router_logits(x, nw, mt, w_router):
    return pl.pallas_call(
        _router_logits_kernel,
        grid=(BATCH, N_TT),
        in_specs=[
            pl.BlockSpec((1, TT, D), lambda b, t: (b, t, 0)),
            pl.BlockSpec((1, D), lambda b, t: (0, 0)),
            pl.BlockSpec((1, 1, 2, D), lambda b, t: (3, b, 0, 0)),
            pl.BlockSpec((1, 1, 2, D), lambda b, t: (4, b, 0, 0)),
            pl.BlockSpec((N_EXPERTS, D), lambda b, t: (0, 0)),
        ],
        out_specs=pl.BlockSpec((1, N_EXPERTS, TT), lambda b, t: (b, 0, t)),
        out_shape=jax.ShapeDtypeStruct((BATCH, N_EXPERTS, S), F32),
        compiler_params=_cparams(("arbitrary", "arbitrary")),
        name="router_logits",
    )(x, nw.reshape(1, D), mt, mt, w_router.T)


def _cumsum_lanes(x):
    n = x.shape[1]
    lane = lax.broadcasted_iota(I32, x.shape, 1)
    k = 1
    while k < n:
        x = x + jnp.where(lane >= k, pltpu.roll(x, k, axis=1), 0.0)
        k *= 2
    return x


def _cumsum_rows_excl(x):
    n = x.shape[0]
    row = lax.broadcasted_iota(I32, x.shape, 0)
    inc = x
    k = 1
    while k < n:
        inc = inc + jnp.where(row >= k, pltpu.roll(inc, k, axis=0), 0.0)
        k *= 2
    return inc - x


def _split3(x):
    hi = x.astype(BF16).astype(F32)
    r = x - hi
    mid = r.astype(BF16).astype(F32)
    return hi, mid, (r - mid).astype(BF16).astype(F32)


def _router_select_kernel(lg_ref, idxl_ref, gl_ref, dstl_ref, idxc_ref, gc_ref, dstc_ref, base_ref, vt_ref):
    ne = N_EXPERTS
    lg = lg_ref[0]
    ex = jnp.exp(lg - jnp.max(lg, axis=0, keepdims=True))
    aff = ex / jnp.sum(ex, axis=0, keepdims=True)
    bits = pltpu.bitcast(aff, I32)
    lane = lax.broadcasted_iota(I32, (ne, S), 1)
    sel = jnp.zeros((ne, S), F32)
    pos = jnp.zeros((ne, S), F32)
    for lo, hi, cap in ((0, CTX, CAP_CTX), (CTX, S, CAP_LAT)):
        vb = jnp.where((lane >= lo) & (lane < hi), bits, -1)
        thr = jnp.zeros((ne, 1), I32)
        for bit in range(30, -1, -1):
            cand = thr | (1 << bit)
            cnt = jnp.sum(jnp.where(vb >= cand, 1.0, 0.0), axis=1, keepdims=True)
            thr = jnp.where(cnt >= cap, cand, thr)
        gt = vb > thr
        eq = jnp.where(vb == thr, 1.0, 0.0)
        need = cap - jnp.sum(jnp.where(gt, 1.0, 0.0), axis=1, keepdims=True)
        eq_rank = _cumsum_lanes(eq) - eq
        s_seg = jnp.where(gt | ((eq > 0.0) & (eq_rank < need)), 1.0, 0.0)
        sel = sel + s_seg
        pos = pos + s_seg * (_cumsum_lanes(s_seg) - s_seg)
    cnt_tok = jnp.broadcast_to(jnp.sum(sel, axis=0, keepdims=True), (ne, S))
    base = _cumsum_lanes(cnt_tok) - cnt_tok
    dest = base + _cumsum_rows_excl(sel)
    base_ref[0] = jnp.concatenate([base[0:1], base[0:1] + cnt_tok[0:1], jnp.zeros((SUBLANE - 2, S), F32)],
                                  axis=0).astype(I32)
    posm = jnp.where(sel > 0.0, pos, -1.0)
    lane_f = lane[0:1].astype(F32)
    idx_hi = jnp.floor(lane_f * (1.0 / 64.0))
    idx_lo = lane_f - 64.0 * idx_hi
    dst_hi = jnp.floor(dest * (1.0 / 128.0))
    dst_lo = dest - 128.0 * dst_hi
    g_hi, g_mid, g_lo = _split3(aff)
    for e in range(ne):
        vt_ref[e] = jnp.concatenate([idx_hi, idx_lo, dst_hi[e:e + 1], dst_lo[e:e + 1], g_hi[e:e + 1],
                                     g_mid[e:e + 1], g_lo[e:e + 1], posm[e:e + 1]], axis=0)
    slot = lax.broadcasted_iota(I32, (LANE, 1), 0).astype(F32)

    def compact(vt, prow, s0):
        onehot = jnp.where(prow == slot + s0, 1.0, 0.0).astype(BF16)
        res = lax.dot_general(vt, onehot, (((1,), (1,)), ((), ())), preferred_element_type=F32)
        return (res[0:1] * 64.0 + res[1:2]).astype(I32), res[4:5] + res[5:6] + res[6:7], \
            (res[2:3] * 128.0 + res[3:4]).astype(I32)

    def per_expert(e, _):
        blk = vt_ref[e]
        prow = blk[SUBLANE - 1:SUBLANE]
        vt = blk.astype(BF16)
        idxc_ref[0, e], gc_ref[0, e], dstc_ref[0, e] = compact(vt[:, :CTX], prow[:, :CTX], 0.0)
        for sc in range(CAP_LAT // LANE):
            cs = slice(sc * LANE, (sc + 1) * LANE)
            idxl_ref[0, e, :, cs], gl_ref[0, e, :, cs], dstl_ref[0, e, :, cs] = compact(
                vt[:, CTX:], prow[:, CTX:], float(sc * LANE))
        return 0

    lax.fori_loop(0, ne, per_expert, 0)


def _router_select(logits):
    ne = N_EXPERTS
    out = lambda n, dt: jax.ShapeDtypeStruct((BATCH, ne, 1, n), dt)
    ospec = lambda n: pl.BlockSpec((1, ne, 1, n), lambda b: (b, 0, 0, 0))
    res = pl.pallas_call(
        _router_select_kernel,
        grid=(BATCH,),
        in_specs=[pl.BlockSpec((1, ne, S), lambda b: (b, 0, 0))],
        out_specs=[ospec(CAP_LAT), ospec(CAP_LAT), ospec(CAP_LAT), ospec(LANE), ospec(LANE), ospec(LANE),
                   pl.BlockSpec((1, SUBLANE, S), lambda b: (b, 0, 0))],
        out_shape=[out(CAP_LAT, I32), out(CAP_LAT, F32), out(CAP_LAT, I32), out(LANE, I32), out(LANE, F32),
                   out(LANE, I32), jax.ShapeDtypeStruct((BATCH, SUBLANE, S), I32)],
        scratch_shapes=[pltpu.VMEM((ne, SUBLANE, S), F32)],
        compiler_params=_cparams(("arbitrary",)),
        name="router_select",
    )(logits)
    return [r.reshape(BATCH, ne, r.shape[-1]) for r in res[:6]] + [res[6]]


ROW_UNROLL = 8


def _start_rows(n_rows, start_one):
    def body(i, _):
        for u in range(ROW_UNROLL):
            start_one(i * ROW_UNROLL + u)
        return 0

    lax.fori_loop(0, n_rows // ROW_UNROLL, body, 0)


def _ffn_up_kernel(src_ref, nsrc_ref, x_hbm, nw_ref, sh_ref, sc_ref, wg_ref, wu_ref, o_ref, xs_ref, gbuf_ref, sem):
    e = pl.program_id(0)
    j = pl.program_id(1)
    cur = e % 2

    def gather(idx_ref, b):
        _start_rows(RPS, lambda r: pltpu.make_async_copy(
            x_hbm.at[pl.ds(idx_ref[0, 0, b * RPS + r], 1)], gbuf_ref.at[b % 2, pl.ds(r, 1)],
            sem.at[b % 2]).start())

    def wait(slot):
        pltpu.make_async_copy(x_hbm.at[pl.ds(0, RPS)], gbuf_ref.at[slot], sem.at[slot]).wait()

    def normalise(b, slot, dst):
        is_ctx = lax.broadcasted_iota(I32, (RPS, 1), 0) < CAP_CTX
        y = _rms(gbuf_ref[slot]) * nw_ref[...]
        y = y * (1.0 + _ctx_select(is_ctx, sc_ref[0, b])) + _ctx_select(is_ctx, sh_ref[0, b])
        row0 = b * RPS if isinstance(b, int) else pl.multiple_of(b * RPS, 32)
        xs_ref[dst, pl.ds(row0, RPS), :] = y.astype(BF16)

    @pl.when((e == 0) & (j == 0))
    def _():
        gather(src_ref, 0)
        for b in range(BATCH):
            if b + 1 < BATCH:
                gather(src_ref, b + 1)
            wait(b % 2)
            normalise(b, b % 2, 0)

    has_next = e + 1 < N_EXPERTS
    for b in range(BATCH):
        @pl.when(has_next & (j == b + 1))
        def _():
            gather(nsrc_ref, b)

        @pl.when(has_next & (j == b + 2))
        def _():
            wait(b % 2)

    b_now = jnp.maximum(j - 2, 0)
    normalise(b_now, jnp.where(j < 2, 1, b_now % 2), 1 - cur)

    xs = xs_ref[cur]
    hg = jnp.dot(xs, wg_ref[0].astype(BF16), preferred_element_type=F32)
    hu = jnp.dot(xs, wu_ref[0].astype(BF16), preferred_element_type=F32)
    o_ref[0] = (_silu(hg) * hu).astype(BF16)


def _ffn_up(src_rows, x2d, nw, mt, w_gate, w_up, li):
    tf = 256
    assert RPS % ROW_UNROLL == 0
    assert D_EXPERT // tf >= BATCH + 2
    src3 = src_rows.reshape(N_EXPERTS, 1, R_EXP)
    return pl.pallas_call(
        _ffn_up_kernel,
        grid=(N_EXPERTS, D_EXPERT // tf),
        in_specs=[
            pl.BlockSpec((1, 1, R_EXP), lambda e, j: (e, 0, 0), memory_space=pltpu.SMEM),
            pl.BlockSpec((1, 1, R_EXP), lambda e, j: (jnp.minimum(e + 1, N_EXPERTS - 1), 0, 0),
                         memory_space=pltpu.SMEM),
            pl.BlockSpec(memory_space=pl.ANY),
            pl.BlockSpec((1, D), lambda e, j: (0, 0)),
            pl.BlockSpec((1, BATCH, 2, D), lambda e, j: (3, 0, 0, 0)),
            pl.BlockSpec((1, BATCH, 2, D), lambda e, j: (4, 0, 0, 0)),
            pl.BlockSpec((None, 1, D, tf), lambda e, j: (li, e, 0, j)),
            pl.BlockSpec((None, 1, D, tf), lambda e, j: (li, e, 0, j)),
        ],
        out_specs=pl.BlockSpec((1, R_EXP, tf), lambda e, j: (e, 0, j)),
        out_shape=jax.ShapeDtypeStruct((N_EXPERTS, R_EXP, D_EXPERT), BF16),
        scratch_shapes=[pltpu.VMEM((2, R_EXP, D), BF16), pltpu.VMEM((2, RPS, D), F32),
                        pltpu.SemaphoreType.DMA((2,))],
        compiler_params=_cparams(("arbitrary", "arbitrary")),
        name="ffn_up",
    )(src3, src3, x2d, nw.reshape(1, D), mt, mt, w_gate, w_up)


def _ffn_down_kernel(dst_ref, hid_ref, w_ref, g_ref, y_hbm, ybuf_ref, sem):
    i = pl.program_id(1)
    step = pl.program_id(0) * BATCH + i
    slot = i % 2
    acc = jnp.dot(hid_ref[0], w_ref[0].astype(BF16), preferred_element_type=F32)
    ybuf_ref[slot] = acc * g_ref[0]

    def drain(s):
        pltpu.make_async_copy(ybuf_ref.at[s], y_hbm.at[pl.ds(0, RPS)], sem.at[s]).wait()

    @pl.when(step > 0)
    def _():
        drain(1 - slot)

    _start_rows(RPS, lambda r: pltpu.make_async_copy(
        ybuf_ref.at[slot, pl.ds(r, 1)], y_hbm.at[pl.ds(dst_ref[0, 0, i * RPS + r], 1)], sem.at[slot]).start())

    @pl.when(step == N_EXPERTS * BATCH - 1)
    def _():
        drain(slot)


def _ffn_down(dst_rows, hid, w_down, g_col, li):
    assert BATCH % 2 == 0
    return pl.pallas_call(
        _ffn_down_kernel,
        grid=(N_EXPERTS, BATCH),
        in_specs=[
            pl.BlockSpec((1, 1, R_EXP), lambda e, i: (e, 0, 0), memory_space=pltpu.SMEM),
            pl.BlockSpec((1, RPS, D_EXPERT), lambda e, i: (e, i, 0)),
            pl.BlockSpec((None, 1, D_EXPERT, D), lambda e, i: (li, e, 0, 0)),
            pl.BlockSpec((1, RPS, 1), lambda e, i: (e, i, 0)),
        ],
        out_specs=pl.BlockSpec(memory_space=pl.ANY),
        out_shape=jax.ShapeDtypeStruct((BATCH * PAIRS, D), F32),
        scratch_shapes=[pltpu.VMEM((2, RPS, D), F32), pltpu.SemaphoreType.DMA((2,))],
        compiler_params=_cparams(("arbitrary", "arbitrary")),
        name="ffn_down",
    )(dst_rows.reshape(N_EXPERTS, 1, R_EXP), hid, w_down, g_col)


def _combine_kernel(cs_ref, ce_ref, ys_hbm, x_ref, base_ref, basen_ref, g_ref, o_ref, buf_ref, acc_ref, sem):
    b = pl.program_id(0)
    t = pl.program_id(1)
    lo = cs_ref[b, t]
    hi = ce_ref[b, t]

    def chunk_copy(c):
        return pltpu.make_async_copy(ys_hbm.at[pl.ds(b * PAIRS + c * TT, TT)], buf_ref.at[c % 2], sem.at[c % 2])

    @pl.when(lo < hi)
    def _():
        chunk_copy(lo).start()

    acc_ref[...] = jnp.zeros((TT, D), F32)
    base = base_ref[0]
    basen = basen_ref[0]
    lane = lax.broadcasted_iota(I32, (1, TT), 1)

    def body(c, _):
        @pl.when(c + 1 < hi)
        def _():
            chunk_copy(c + 1).start()

        chunk_copy(c).wait()
        r = c * TT + lane
        onehot = jnp.where((base <= r) & (r < basen), 1.0, 0.0).astype(BF16)
        y = buf_ref[c % 2]
        y_hi = y.astype(BF16)
        y_lo = (y - y_hi.astype(F32)).astype(BF16)
        acc_ref[...] += (jnp.dot(onehot, y_hi, preferred_element_type=F32)
                         + jnp.dot(onehot, y_lo, preferred_element_type=F32))
        return 0

    lax.fori_loop(lo, hi, body, 0)
    gate = jnp.where(t == 0, g_ref[0, 0, 0:1, :], g_ref[0, 0, 1:2, :])
    o_ref[0] = x_ref[0] + gate * acc_ref[...]


def _combine(cs, ce, ys, x, base_col, basen_col, mt):
    grid_spec = pltpu.PrefetchScalarGridSpec(
        num_scalar_prefetch=2,
        grid=(BATCH, N_TT),
        in_specs=[
            pl.BlockSpec(memory_space=pl.ANY),
            pl.BlockSpec((1, TT, D), lambda b, t, *_: (b, t, 0)),
            pl.BlockSpec((1, TT, 1), lambda b, t, *_: (b, t, 0)),
            pl.BlockSpec((1, TT, 1), lambda b, t, *_: (b, t, 0)),
            pl.BlockSpec((1, 1, 2, D), lambda b, t, *_: (5, b, 0, 0)),
        ],
        out_specs=pl.BlockSpec((1, TT, D), lambda b, t, *_: (b, t, 0)),
        scratch_shapes=[pltpu.VMEM((2, TT, D), F32), pltpu.VMEM((TT, D), F32), pltpu.SemaphoreType.DMA((2,))],
    )
    return pl.pallas_call(
        _combine_kernel,
        grid_spec=grid_spec,
        out_shape=jax.ShapeDtypeStruct((BATCH, S, D), F32),
        compiler_params=_cparams(("arbitrary", "arbitrary")),
        name="moe_combine",
    )(cs, ce, ys, x, base_col, basen_col, mt)


def _moe(x, mt, norm_w, w_router, w_gate, w_up, w_down, li):
    logits = _router_logits(x, norm_w, mt, w_router)
    idx_l, g_l, dst_l, idx_c, g_c, dst_c, bases = _router_select(logits)
    boff = jnp.arange(BATCH, dtype=I32)[:, None, None]

    def rows(c, l, off):
        r = jnp.concatenate([c[:, :, :CAP_CTX], l], axis=2) + off
        return jnp.transpose(r, (1, 0, 2)).reshape(N_EXPERTS, R_EXP)

    src_rows = rows(idx_c, idx_l, boff * S)
    dst_rows = rows(dst_c, dst_l, boff * PAIRS)
    g_col = rows(g_c, g_l, 0.0).reshape(N_EXPERTS, R_EXP, 1)
    hid = _ffn_up(src_rows, x.reshape(BATCH * S, D), norm_w, mt, w_gate, w_up, li)
    ys = _ffn_down(dst_rows, hid, w_down, g_col, li)
    base, basen = bases[:, 0, :], bases[:, 1, :]
    cs = base[:, ::TT] // TT
    ce = (basen[:, TT - 1::TT] + TT - 1) // TT
    return _combine(cs, ce, ys, x, base.reshape(BATCH, S, 1), basen.reshape(BATCH, S, 1), mt)


def kernel(x, c, ctx, c_ctx, ada_w, ada_b, norm_mix_w, norm_ffn_w,
           ev_w_in, ev_conv_w, ev_conv_b, ev_ra_w, ev_ra_b, ev_ix_w, ev_ix_b, ev_lambda, ev_sink, ev_w_out,
           od_w_in, od_conv_w, od_conv_b, od_gate_b, od_hnorm_w, od_w_out,
           moe_router, moe_w_gate, moe_w_up, moe_w_down, final_norm_w):
    assert x.shape == (BATCH, SEQ, D) and ctx.shape == (BATCH, CTX, D)
    cc = jnp.zeros((SUBLANE, D), F32).at[:BATCH].set(c).at[BATCH].set(c_ctx)
    mod = _modulation(cc, ada_w, ada_b)
    rope = _rope_tables()
    ev_w_in, ev_w_out, od_w_in, od_w_out = (w.astype(BF16) for w in (ev_w_in, ev_w_out, od_w_in, od_w_out))
    xs = jnp.concatenate([ctx, x], axis=1)
    for layer in range(DEPTH):
        mt = _mod_table(mod[layer])
        i = layer // 2
        if layer % 2 == 0:
            xs = _even_mixer(xs, mt, norm_mix_w[layer], ev_w_in, i, ev_conv_w[i], ev_conv_b[i], ev_ra_w[i],
                             ev_ra_b[i], ev_ix_w[i], ev_ix_b[i], ev_lambda[i], ev_sink[i], ev_w_out, rope)
        else:
            xs = _odd_mixer(xs, mt, norm_mix_w[layer], od_w_in, i, od_conv_w[i], od_conv_b[i], od_gate_b[i],
                            od_hnorm_w[i], od_w_out)
        xs = _moe(xs, mt, norm_ffn_w[layer], moe_router[layer], moe_w_gate, moe_w_up, moe_w_down, layer)
    return _final_norm(xs, final_norm_w)
```

```python
import functools

import jax
import jax.numpy as jnp
from jax import lax
from jax.experimental import pallas as pl
from jax.experimental.pallas import tpu as pltpu

F32 = jnp.float32
BF16 = jnp.bfloat16
I32 = jnp.int32

D = 2048
BATCH = 4
SEQ = 4096
CTX = 256
S = CTX + SEQ
DEPTH = 4
N_MOD = 6
EPS = 1e-6
GRID_W = 64

LRU_W = 1024
LRU_BLOCKS = 8
LRU_BLOCK = 128
LRU_C = 8.0
CONV_W = 4
ATT_HEADS = 8
KV_HEADS = 2
GROUP = ATT_HEADS // KV_HEADS
HEAD_DIM = 128
ATT_W = ATT_HEADS * HEAD_DIM
KV_W = KV_HEADS * HEAD_DIM
WINDOW = 128
ROPE_PAIRS = HEAD_DIM // 4
ROPE_BASE = 10000.0
EVEN_IN = 2 * LRU_W + ATT_W + 2 * KV_W

M_HEADS = 8
M_DK = 128
M_DV = 256
M_QK = M_HEADS * M_DK
M_V = M_HEADS * M_DV
ODD_MAIN = 2 * M_QK + 2 * M_V
N_GATES = 4 * M_HEADS

N_EXPERTS = 16
EC_FACTOR = 2
D_EXPERT = 1536
CAP_LAT = EC_FACTOR * SEQ // N_EXPERTS
CAP_CTX = EC_FACTOR * CTX // N_EXPERTS
RPS = CAP_CTX + CAP_LAT
R_EXP = BATCH * RPS
PAIRS = N_EXPERTS * RPS

LANE = 128
SUBLANE = 8
VMEM_LIMIT = 56 * 1024 * 1024
TM = S // 4
TILES_PER_SAMPLE = S // TM
TN = 512
TT = 256
N_TT = S // TT
CHUNK = 128
N_CHUNK = S // CHUNK
CTX_CHUNKS = CTX // CHUNK


def _cparams(sem, vmem=VMEM_LIMIT):
    return pltpu.CompilerParams(dimension_semantics=sem, vmem_limit_bytes=vmem)


def _sigmoid(x):
    return 1.0 / (1.0 + jnp.exp(-x))


def _silu(x):
    return x * _sigmoid(x)


def _softplus(x):
    return jnp.maximum(x, 0.0) + jnp.log1p(jnp.exp(-jnp.abs(x)))


def _log_sigmoid(x):
    return -_softplus(-x)


def _gelu_tanh(x):
    return 0.5 * x * (1.0 + jnp.tanh(0.7978845608028654 * (x + 0.044715 * (x * x * x))))


def _rms(x):
    return x * lax.rsqrt(jnp.mean(x * x, axis=-1, keepdims=True) + EPS)


def _ctx_select(is_ctx, tab):
    return jnp.where(is_ctx, tab[0:1, :], tab[1:2, :])


def _mod_kernel(c_ref, w_ref, b_ref, o_ref):
    a = _silu(c_ref[...]).astype(BF16)
    o_ref[0] = jnp.dot(a, w_ref[0].astype(BF16), preferred_element_type=F32) + b_ref[0]


def _modulation(cc, ada_w, ada_b):
    tn = 1024
    return pl.pallas_call(
        _mod_kernel,
        grid=(DEPTH, N_MOD * D // tn),
        in_specs=[
            pl.BlockSpec((SUBLANE, D), lambda l, j: (0, 0)),
            pl.BlockSpec((1, D, tn), lambda l, j: (l, 0, j)),
            pl.BlockSpec((1, 1, tn), lambda l, j: (l, 0, j)),
        ],
        out_specs=pl.BlockSpec((1, SUBLANE, tn), lambda l, j: (l, 0, j)),
        out_shape=jax.ShapeDtypeStruct((DEPTH, SUBLANE, N_MOD * D), F32),
        compiler_params=_cparams(("arbitrary", "arbitrary")),
        name="adaln_mod",
    )(cc, ada_w, ada_b.reshape(DEPTH, 1, N_MOD * D))


def _mod_table(mod_layer):
    m = mod_layer.reshape(SUBLANE, N_MOD, D)
    lat = jnp.transpose(m[:BATCH], (1, 0, 2))
    ctx = jnp.broadcast_to(m[BATCH][:, None, :], (N_MOD, BATCH, D))
    return jnp.stack([ctx, lat], axis=2)


def _norm_mod_rows(x, nw, sh_tab, sc_tab, row0):
    rows = x.shape[0]
    is_ctx = (row0 + lax.broadcasted_iota(I32, (rows, 1), 0)) < CTX
    y = _rms(x) * nw
    return y * (1.0 + _ctx_select(is_ctx, sc_tab)) + _ctx_select(is_ctx, sh_tab)


def _in_proj_kernel(x_ref, nw_ref, sh_ref, sc_ref, w_ref, o_ref, xn_ref):
    i = pl.program_id(0)
    j = pl.program_id(1)

    @pl.when(j == 0)
    def _():
        row0 = (i % TILES_PER_SAMPLE) * TM
        xn_ref[...] = _norm_mod_rows(x_ref[0], nw_ref[...], sh_ref[0, 0], sc_ref[0, 0], row0).astype(BF16)

    o_ref[0] = jnp.dot(xn_ref[...], w_ref[...].astype(BF16), preferred_element_type=F32)


def _in_proj(x, nw, mt, w, li, n_out):
    return pl.pallas_call(
        _in_proj_kernel,
        grid=(BATCH * TILES_PER_SAMPLE, n_out // TN),
        in_specs=[
            pl.BlockSpec((1, TM, D), lambda i, j: (i // TILES_PER_SAMPLE, i % TILES_PER_SAMPLE, 0)),
            pl.BlockSpec((1, D), lambda i, j: (0, 0)),
            pl.BlockSpec((1, 1, 2, D), lambda i, j: (0, i // TILES_PER_SAMPLE, 0, 0)),
            pl.BlockSpec((1, 1, 2, D), lambda i, j: (1, i // TILES_PER_SAMPLE, 0, 0)),
            pl.BlockSpec((None, D, TN), lambda i, j: (li, 0, j)),
        ],
        out_specs=pl.BlockSpec((1, TM, TN), lambda i, j: (i // TILES_PER_SAMPLE, i % TILES_PER_SAMPLE, j)),
        out_shape=jax.ShapeDtypeStruct((BATCH, S, n_out), F32),
        scratch_shapes=[pltpu.VMEM((TM, D), BF16)],
        compiler_params=_cparams(("arbitrary", "arbitrary")),
        name="in_proj",
    )(x, nw.reshape(1, D), mt, mt, w)


def _gate_proj_kernel(x_ref, nw_ref, sh_ref, sc_ref, w_ref, o_ref):
    t = pl.program_id(1)
    xn = _norm_mod_rows(x_ref[0], nw_ref[...], sh_ref[0, 0], sc_ref[0, 0], t * TT).astype(BF16)
    o_ref[0] = jnp.dot(xn, w_ref[...].astype(BF16), preferred_element_type=F32)


def _gate_proj(x, nw, mt, wg):
    return pl.pallas_call(
        _gate_proj_kernel,
        grid=(BATCH, N_TT),
        in_specs=[
            pl.BlockSpec((1, TT, D), lambda b, t: (b, t, 0)),
            pl.BlockSpec((1, D), lambda b, t: (0, 0)),
            pl.BlockSpec((1, 1, 2, D), lambda b, t: (0, b, 0, 0)),
            pl.BlockSpec((1, 1, 2, D), lambda b, t: (1, b, 0, 0)),
            pl.BlockSpec((D, LANE), lambda b, t: (0, 0)),
        ],
        out_specs=pl.BlockSpec((1, TT, LANE), lambda b, t: (b, t, 0)),
        out_shape=jax.ShapeDtypeStruct((BATCH, S, LANE), F32),
        compiler_params=_cparams(("arbitrary", "arbitrary")),
        name="gate_proj",
    )(x, nw.reshape(1, D), mt, mt, wg)


def _residual_epilogue(i, x_ref, g_ref, acc, o_ref):
    tm = acc.shape[0]
    row0 = (i % (S // tm)) * tm
    is_ctx = (row0 + lax.broadcasted_iota(I32, (tm, 1), 0)) < CTX
    o_ref[0] = x_ref[0] + _ctx_select(is_ctx, g_ref[0, 0]) * acc


def _out_proj_even_kernel(a1_ref, a2_ref, x_ref, g_ref, w_ref, o_ref, a_ref):
    i = pl.program_id(0)
    j = pl.program_id(1)

    @pl.when(j == 0)
    def _():
        a_ref[:, :LRU_W] = a1_ref[0].astype(BF16)
        a_ref[:, LRU_W:] = a2_ref[0].astype(BF16)

    acc = jnp.dot(a_ref[...], w_ref[...].astype(BF16), preferred_element_type=F32)
    _residual_epilogue(i, x_ref, g_ref, acc, o_ref)


def _out_proj_odd_kernel(h_ref, og_ref, hw_ref, x_ref, g_ref, w_ref, o_ref, a_ref):
    i = pl.program_id(0)
    j = pl.program_id(1)

    @pl.when(j == 0)
    def _():
        for h in range(M_HEADS):
            sl = slice(h * M_DV, (h + 1) * M_DV)
            hn = _rms(h_ref[0, :, sl]) * hw_ref[:, sl] * _sigmoid(og_ref[0, :, sl])
            a_ref[:, sl] = hn.astype(BF16)

    acc = jnp.dot(a_ref[...], w_ref[...].astype(BF16), preferred_element_type=F32)
    _residual_epilogue(i, x_ref, g_ref, acc, o_ref)


def _row_spec(tm, width, col=None):
    tps = S // tm
    if col is None:
        return pl.BlockSpec((1, tm, width), lambda i, j: (i // tps, i % tps, j))
    return pl.BlockSpec((1, tm, width), lambda i, j: (i // tps, i % tps, col))


def _out_proj_even(a1, a2, x, mt, w, li):
    tm = TM
    return pl.pallas_call(
        _out_proj_even_kernel,
        grid=(BATCH * S // tm, D // TN),
        in_specs=[
            _row_spec(tm, LRU_W, 0),
            _row_spec(tm, ATT_W, 0),
            _row_spec(tm, TN),
            pl.BlockSpec((1, 1, 2, TN), lambda i, j: (2, i // (S // tm), 0, j)),
            pl.BlockSpec((None, D, TN), lambda i, j: (li, 0, j)),
        ],
        out_specs=_row_spec(tm, TN),
        out_shape=jax.ShapeDtypeStruct((BATCH, S, D), F32),
        scratch_shapes=[pltpu.VMEM((tm, D), BF16)],
        compiler_params=_cparams(("arbitrary", "arbitrary")),
        name="out_proj_even",
    )(a1, a2, x, mt, w)


def _out_proj_odd(h, p_odd, hnorm_w, x, mt, w, li):
    tm = TM // 2
    return pl.pallas_call(
        _out_proj_odd_kernel,
        grid=(BATCH * S // tm, D // TN),
        in_specs=[
            _row_spec(tm, M_V, 0),
            _row_spec(tm, M_V, (2 * M_QK + M_V) // M_V),
            pl.BlockSpec((1, M_V), lambda i, j: (0, 0)),
            _row_spec(tm, TN),
            pl.BlockSpec((1, 1, 2, TN), lambda i, j: (2, i // (S // tm), 0, j)),
            pl.BlockSpec((None, D, TN), lambda i, j: (li, 0, j)),
        ],
        out_specs=_row_spec(tm, TN),
        out_shape=jax.ShapeDtypeStruct((BATCH, S, D), F32),
        scratch_shapes=[pltpu.VMEM((tm, D), BF16)],
        compiler_params=_cparams(("arbitrary", "arbitrary")),
        name="out_proj_odd",
    )(h, p_odd, hnorm_w.reshape(1, M_V), x, mt, w)


def _final_norm_kernel(x_ref, w_ref, o_ref):
    o_ref[0] = _rms(x_ref[0]) * w_ref[...]


def _final_norm(x, w):
    return pl.pallas_call(
        _final_norm_kernel,
        grid=(BATCH, SEQ // TT),
        in_specs=[
            pl.BlockSpec((1, TT, D), lambda b, t: (b, t + CTX // TT, 0)),
            pl.BlockSpec((1, D), lambda b, t: (0, 0)),
        ],
        out_specs=pl.BlockSpec((1, TT, D), lambda b, t: (b, t, 0)),
        out_shape=jax.ShapeDtypeStruct((BATCH, SEQ, D), F32),
        compiler_params=_cparams(("arbitrary", "arbitrary")),
        name="final_norm",
    )(x, w.reshape(1, D))


def _seg_conv(x, cw, cb):
    n = x.shape[0]
    row = lax.broadcasted_iota(I32, (n, 1), 0)
    seg = row < CTX
    y = cb
    for j in range(CONV_W):
        off = j - CONV_W // 2
        if off == 0:
            tap = x
        else:
            src = row + off
            ok = (src >= 0) & (src < n) & ((src < CTX) == seg)
            tap = jnp.where(ok, pltpu.roll(x, (-off) % n, axis=0), 0.0)
        y = y + tap * cw[j:j + 1, :]
    return y


def _bwd_chunk(j):
    return jnp.where(j < CTX_CHUNKS, CTX_CHUNKS - 1 - j, N_CHUNK + CTX_CHUNKS - 1 - j)


def _lin_scan(a, b, h_in, reverse):
    t_len, c = a.shape
    ng = t_len // SUBLANE
    a = a.reshape(ng, SUBLANE, c)
    b = b.reshape(ng, SUBLANE, c)
    sub = lax.broadcasted_iota(I32, a.shape, 1)
    k = 1
    while k < SUBLANE:
        shift = SUBLANE - k if reverse else k
        ok = (sub < SUBLANE - k) if reverse else (sub >= k)
        a_s = pltpu.roll(a, shift, axis=1)
        b_s = pltpu.roll(b, shift, axis=1)
        b = jnp.where(ok, a * b_s + b, b)
        a = jnp.where(ok, a * a_s, a)
        k *= 2
    edge = 0 if reverse else SUBLANE - 1
    hs = [None] * ng
    for g in (range(ng - 1, -1, -1) if reverse else range(ng)):
        hs[g] = b[g] + a[g] * h_in
        h_in = hs[g][edge:edge + 1, :]
    return jnp.concatenate(hs, axis=0)


def _lru_kernel(xa_ref, ya_ref, cw_ref, cb_ref, raw_ref, rab_ref, ixw_ref, ixb_ref, lam_ref, o_ref, xc_ref, hf_ref):
    xc_ref[...] = _seg_conv(xa_ref[0], cw_ref[...], cb_ref[...])

    def gates(x, d):
        xb = x.astype(BF16)
        r = _sigmoid(jnp.dot(xb, raw_ref[d, 0].astype(BF16), preferred_element_type=F32) + rab_ref[d, 0])
        i = _sigmoid(jnp.dot(xb, ixw_ref[d, 0].astype(BF16), preferred_element_type=F32) + ixb_ref[d, 0])
        log_a = (-LRU_C * r) * _softplus(-lam_ref[d, 0])
        a = jnp.exp(log_a)
        return a, jnp.sqrt(-jnp.tanh(log_a) * (a * a + 1.0)) * (i * x)

    hf_ref[...] = jnp.zeros((S, LRU_BLOCK), F32)

    def one(chunk, h, d):
        rows = pl.ds(pl.multiple_of(chunk * CHUNK, CHUNK), CHUNK)
        a, b = gates(xc_ref[rows, :], d)
        h_all = _lin_scan(a, b, h, d == 1)
        hf_ref[rows, :] += h_all
        return h_all[0:1, :] if d == 1 else h_all[CHUNK - 1:CHUNK, :]

    def body(j, carry):
        return one(j, carry[0], 0), one(_bwd_chunk(j), carry[1], 1)

    zero = jnp.zeros((1, LRU_BLOCK), F32)
    lax.fori_loop(0, N_CHUNK, body, (zero, zero))
    o_ref[0] = hf_ref[...] * _gelu_tanh(ya_ref[0])


def _lru(p_even, conv_w, conv_b, ra_w, ra_b, ix_w, ix_b, lam):
    nb = LRU_BLOCKS
    blk4 = lambda: pl.BlockSpec((2, 1, LRU_BLOCK, LRU_BLOCK), lambda b, k: (0, k, 0, 0))
    vec4 = lambda: pl.BlockSpec((2, 1, 1, LRU_BLOCK), lambda b, k: (0, k, 0, 0))
    return pl.pallas_call(
        _lru_kernel,
        grid=(BATCH, nb),
        in_specs=[
            pl.BlockSpec((1, S, LRU_BLOCK), lambda b, k: (b, 0, k)),
            pl.BlockSpec((1, S, LRU_BLOCK), lambda b, k: (b, 0, nb + k)),
            pl.BlockSpec((CONV_W, LRU_BLOCK), lambda b, k: (0, k)),
            pl.BlockSpec((1, LRU_BLOCK), lambda b, k: (0, k)),
            blk4(), vec4(), blk4(), vec4(), vec4(),
        ],
        out_specs=pl.BlockSpec((1, S, LRU_BLOCK), lambda b, k: (b, 0, k)),
        out_shape=jax.ShapeDtypeStruct((BATCH, S, LRU_W), F32),
        scratch_shapes=[pltpu.VMEM((S, LRU_BLOCK), F32), pltpu.VMEM((S, LRU_BLOCK), F32)],
        compiler_params=_cparams(("arbitrary", "arbitrary")),
        name="rglru",
    )(p_even, p_even, conv_w, conv_b.reshape(1, LRU_W), ra_w, ra_b.reshape(2, nb, 1, LRU_BLOCK),
      ix_w, ix_b.reshape(2, nb, 1, LRU_BLOCK), lam.reshape(2, nb, 1, LRU_BLOCK))


def _rope(x, cos, sin):
    lane = lax.broadcasted_iota(I32, (1, HEAD_DIM), 1)
    first = (lane % (2 * ROPE_PAIRS)) < ROPE_PAIRS
    swapped = jnp.where(first, pltpu.roll(x, HEAD_DIM - ROPE_PAIRS, axis=1), pltpu.roll(x, ROPE_PAIRS, axis=1))
    return x * cos + swapped * sin


def _attn_kernel(sink_ref, q_ref, kp_ref, ko_ref, kn_ref, vp_ref, vo_ref, vn_ref, ck_ref, cv_ref,
                 cq_ref, sq_ref, cp_ref, sp_ref, cn_ref, sn_ref, o_ref):
    t = pl.program_id(1)
    nq = CHUNK
    lat = t >= CTX_CHUNKS
    c_lo = jnp.where(lat, jnp.where(t > CTX_CHUNKS, 0, nq), 0)
    c_hi = jnp.where(lat, jnp.where(t < N_CHUNK - 1, 3 * nq, 2 * nq), 0)
    r = lax.broadcasted_iota(I32, (nq, CTX + 3 * nq), 0)
    c = lax.broadcasted_iota(I32, (nq, CTX + 3 * nq), 1) - CTX
    band_ok = (jnp.abs(c - nq - r) <= WINDOW) & (c >= c_lo) & (c < c_hi)
    bias = jnp.where((c < 0) | band_ok, 0.0, -jnp.inf)
    bias = jnp.concatenate([bias] * GROUP, axis=0)
    scale = HEAD_DIM ** -0.5
    for g in range(KV_HEADS):
        ks = slice(g * HEAD_DIM, (g + 1) * HEAD_DIM)
        keys = jnp.concatenate([
            ck_ref[0, :, ks],
            _rope(kp_ref[0, :, ks], cp_ref[...], sp_ref[...]),
            _rope(ko_ref[0, :, ks], cq_ref[...], sq_ref[...]),
            _rope(kn_ref[0, :, ks], cn_ref[...], sn_ref[...]),
        ], axis=0).astype(BF16)
        vals = jnp.concatenate([cv_ref[0, :, ks], vp_ref[0, :, ks], vo_ref[0, :, ks], vn_ref[0, :, ks]],
                               axis=0).astype(BF16)
        heads = [g * GROUP + hh for hh in range(GROUP)]
        qg = jnp.concatenate([_rope(q_ref[0, :, h * HEAD_DIM:(h + 1) * HEAD_DIM], cq_ref[...], sq_ref[...])
                              for h in heads], axis=0).astype(BF16)
        sink = jnp.concatenate([jnp.full((nq, 1), sink_ref[h], F32) for h in heads], axis=0)
        s = lax.dot_general(qg, keys, (((1,), (1,)), ((), ())), preferred_element_type=F32) * scale + bias
        m = jnp.maximum(jnp.max(s, axis=-1, keepdims=True), sink)
        p = jnp.exp(s - m)
        den = jnp.sum(p, axis=-1, keepdims=True) + jnp.exp(sink - m)
        out = jnp.dot((p / den).astype(BF16), vals, preferred_element_type=F32)
        for hh, h in enumerate(heads):
            o_ref[0, :, h * HEAD_DIM:(h + 1) * HEAD_DIM] = out[hh * nq:(hh + 1) * nq]


def _attention(p_even, sink, cos_t, sin_t):
    qc = 2 * LRU_W // ATT_W
    kc = (2 * LRU_W + ATT_W) // KV_W
    vc = kc + 1
    lo, hi = CTX_CHUNKS, N_CHUNK - 1
    prev = lambda t: jnp.clip(t - 1, lo, hi)
    nxt = lambda t: jnp.clip(t + 1, lo, hi)
    own = lambda t: t
    kv = lambda col, f: pl.BlockSpec((1, CHUNK, KV_W), lambda b, t: (b, f(t), col))
    tab = lambda f: pl.BlockSpec((CHUNK, HEAD_DIM), lambda b, t: (f(t), 0))
    return pl.pallas_call(
        _attn_kernel,
        grid=(BATCH, N_CHUNK),
        in_specs=[
            pl.BlockSpec(memory_space=pltpu.SMEM),
            pl.BlockSpec((1, CHUNK, ATT_W), lambda b, t: (b, t, qc)),
            kv(kc, prev), kv(kc, own), kv(kc, nxt),
            kv(vc, prev), kv(vc, own), kv(vc, nxt),
            pl.BlockSpec((1, CTX, KV_W), lambda b, t: (b, 0, kc)),
            pl.BlockSpec((1, CTX, KV_W), lambda b, t: (b, 0, vc)),
            tab(own), tab(own), tab(prev), tab(prev), tab(nxt), tab(nxt),
        ],
        out_specs=pl.BlockSpec((1, CHUNK, ATT_W), lambda b, t: (b, t, 0)),
        out_shape=jax.ShapeDtypeStruct((BATCH, S, ATT_W), F32),
        compiler_params=_cparams(("arbitrary", "arbitrary")),
        name="window_attention",
    )(sink, p_even, p_even, p_even, p_even, p_even, p_even, p_even, p_even, p_even,
      cos_t, sin_t, cos_t, sin_t, cos_t, sin_t)


def _rope_tables():
    inv = jnp.power(ROPE_BASE, -jnp.arange(ROPE_PAIRS, dtype=F32) / ROPE_PAIRS)
    pos = jnp.arange(SEQ)
    row_ang = (pos // GRID_W).astype(F32)[:, None] * inv
    col_ang = (pos % GRID_W).astype(F32)[:, None] * inv
    cos = jnp.concatenate([jnp.cos(row_ang)] * 2 + [jnp.cos(col_ang)] * 2, axis=-1)
    sin = jnp.concatenate([-jnp.sin(row_ang), jnp.sin(row_ang), -jnp.sin(col_ang), jnp.sin(col_ang)], axis=-1)
    cos = jnp.concatenate([jnp.ones((CTX, HEAD_DIM), F32), cos], axis=0)
    sin = jnp.concatenate([jnp.zeros((CTX, HEAD_DIM), F32), sin], axis=0)
    return cos, sin


def _mlstm_local(qb, kb, k, vb, li_row, b_row, reverse):
    ln = qb.shape[0]
    row = lax.broadcasted_iota(I32, (ln, ln), 0)
    col = lax.broadcasted_iota(I32, (ln, ln), 1)
    b_r = jnp.broadcast_to(b_row, (ln, ln))
    li_r = jnp.broadcast_to(li_row, (ln, ln))
    b = b_r.T
    li = li_r.T
    causal = (col >= row) if reverse else (col <= row)
    logw = jnp.where(causal, b - b_r + li_r, -jnp.inf)
    m_loc = jnp.max(logw, axis=1, keepdims=True)
    s = lax.dot_general(qb, kb, (((1,), (1,)), ((), ())), preferred_element_type=F32) * jnp.exp(logw - m_loc)
    num = jnp.dot(s.astype(BF16), vb, preferred_element_type=F32)
    den = jnp.sum(s, axis=1, keepdims=True)
    edge = 0 if reverse else ln - 1
    b_last = b[edge:edge + 1, :]
    log_u = b_last - b + li
    mu_loc = jnp.max(log_u, axis=0, keepdims=True)
    uk = jnp.exp(log_u - mu_loc) * k
    kv = lax.dot_general(uk.astype(BF16), vb, (((0,), (0,)), ((), ())), preferred_element_type=F32)
    return dict(qb=qb, b=b, m_loc=m_loc, num=num, den=den, b_last=b_last, mu_loc=mu_loc, kv=kv,
                ksum=jnp.sum(uk, axis=0, keepdims=True))


def _mlstm_apply(loc, state):
    c0, n0, m0 = state
    b, m_loc = loc["b"], loc["m_loc"]
    m = jnp.maximum(b + m0, m_loc)
    intra = jnp.exp(m_loc - m)
    inter = jnp.exp(b + m0 - m)
    qc = jnp.dot(loc["qb"], c0.astype(BF16), preferred_element_type=F32)
    n8 = jnp.broadcast_to(n0, (SUBLANE, n0.shape[1])).astype(BF16)
    qn = lax.dot_general(loc["qb"], n8, (((1,), (1,)), ((), ())), preferred_element_type=F32)[:, :1]
    num = jnp.concatenate([intra, intra], axis=1) * loc["num"] + jnp.concatenate([inter, inter], axis=1) * qc
    den = intra[:, :1] * loc["den"] + inter[:, :1] * qn
    h = num / jnp.maximum(jnp.abs(den), jnp.exp(-m[:, :1]))
    m_new = jnp.maximum(loc["b_last"] + m0, loc["mu_loc"])
    decay = jnp.exp(loc["b_last"] + m0 - m_new)
    grow = jnp.exp(loc["mu_loc"] - m_new)
    c_new = decay[:, :1] * c0 + grow[:, :1] * loc["kv"]
    n_new = decay * n0 + grow * loc["ksum"]
    return h, (c_new, n_new, m_new)


MLSTM_UNROLL = 2


def _mlstm_kernel(gb_ref, q_ref, k_ref, v_ref, g_ref, cwq_ref, cbq_ref, cwk_ref, cbk_ref, o_ref,
                  qb_ref, kb_ref, kc_ref, vb_ref, gr_ref):
    hd = pl.program_id(1)
    ln = CHUNK
    qb_ref[...] = (_silu(_seg_conv(q_ref[0], cwq_ref[...], cbq_ref[...])) * (M_DK ** -0.5)).astype(BF16)
    kc = _silu(_seg_conv(k_ref[0], cwk_ref[...], cbk_ref[...]))
    kc_ref[...] = kc
    kb_ref[...] = kc.astype(BF16)
    vb_ref[...] = v_ref[0].astype(BF16)
    o_ref[0] = jnp.zeros((S, M_DV), F32)

    def rows_of(chunk):
        return pl.ds(pl.multiple_of(chunk * ln, ln), ln)

    sub = lax.broadcasted_iota(I32, (N_GATES, 1), 0)

    def gate_rows(i, _):
        for u in range(MLSTM_UNROLL):
            c = i * MLSTM_UNROLL + u
            gt = g_ref[0, rows_of(c), :].T[:N_GATES]
            pick = lambda ty: jnp.sum(jnp.where(sub == ty * M_HEADS + hd, gt, 0.0), axis=0, keepdims=True) \
                + gb_ref[ty, hd]
            gr_ref[c] = jnp.concatenate([pick(0), _log_sigmoid(pick(1)), pick(2), _log_sigmoid(pick(3)),
                                         jnp.zeros((SUBLANE - 4, ln), F32)], axis=0)
        return 0

    lax.fori_loop(0, N_CHUNK // MLSTM_UNROLL, gate_rows, 0)
    gr = gr_ref[...]
    lane = lax.broadcasted_iota(I32, gr.shape, 2)
    kind = lax.broadcasted_iota(I32, gr.shape, 1)
    pre = suf = gr
    kk = 1
    while kk < ln:
        pre = pre + jnp.where(lane >= kk, pltpu.roll(pre, kk, axis=2), 0.0)
        suf = suf + jnp.where(lane < ln - kk, pltpu.roll(suf, ln - kk, axis=2), 0.0)
        kk *= 2
    gr_ref[...] = jnp.where(kind == 1, pre, jnp.where(kind == 3, suf, gr))

    def local(chunk, d):
        rows = rows_of(chunk)
        gr = gr_ref[chunk]
        return _mlstm_local(qb_ref[rows, :], kb_ref[rows, :], kc_ref[rows, :], vb_ref[rows, :],
                            gr[2 * d:2 * d + 1], gr[2 * d + 1:2 * d + 2], d == 1)

    def body(i, carry):
        steps = [i * MLSTM_UNROLL + u for u in range(MLSTM_UNROLL)]
        chunks = ([j for j in steps], [_bwd_chunk(j) for j in steps])
        locs = [[local(c, d) for c in chunks[d]] for d in range(2)]
        states = list(carry)
        for u in range(MLSTM_UNROLL):
            for d in range(2):
                h, states[d] = _mlstm_apply(locs[d][u], states[d])
                o_ref[0, rows_of(chunks[d][u]), :] += h
        return tuple(states)

    zero = (jnp.zeros((M_DK, M_DV), F32), jnp.zeros((1, M_DK), F32), jnp.zeros((1, ln), F32))
    lax.fori_loop(0, N_CHUNK // MLSTM_UNROLL, body, (zero, zero))


def _mlstm(p_odd, gates, conv_w, conv_b, gate_b):
    nh = M_HEADS
    cw = lambda off: pl.BlockSpec((CONV_W, M_DK), lambda b, h: (0, off + h))
    cb = lambda off: pl.BlockSpec((1, M_DK), lambda b, h: (0, off + h))
    conv_b = conv_b.reshape(1, 2 * M_QK)
    assert N_CHUNK % MLSTM_UNROLL == 0 and CHUNK == M_DK
    return pl.pallas_call(
        _mlstm_kernel,
        grid=(BATCH, nh),
        in_specs=[
            pl.BlockSpec(memory_space=pltpu.SMEM),
            pl.BlockSpec((1, S, M_DK), lambda b, h: (b, 0, h)),
            pl.BlockSpec((1, S, M_DK), lambda b, h: (b, 0, nh + h)),
            pl.BlockSpec((1, S, M_DV), lambda b, h: (b, 0, 2 * M_QK // M_DV + h)),
            pl.BlockSpec((1, S, LANE), lambda b, h: (b, 0, 0)),
            cw(0), cb(0), cw(nh), cb(nh),
        ],
        out_specs=pl.BlockSpec((1, S, M_DV), lambda b, h: (b, 0, h)),
        out_shape=jax.ShapeDtypeStruct((BATCH, S, M_V), F32),
        scratch_shapes=[pltpu.VMEM((S, M_DK), BF16), pltpu.VMEM((S, M_DK), BF16), pltpu.VMEM((S, M_DK), F32),
                        pltpu.VMEM((S, M_DV), BF16), pltpu.VMEM((N_CHUNK, SUBLANE, CHUNK), F32)],
        compiler_params=_cparams(("arbitrary", "arbitrary")),
        name="mlstm",
    )(gate_b, p_odd, p_odd, p_odd, gates, conv_w, conv_b, conv_w, conv_b)


def _even_mixer(x, mt, norm_w, w_in, li, conv_w, conv_b, ra_w, ra_b, ix_w, ix_b, lam, sink, w_out, rope):
    p = _in_proj(x, norm_w, mt, w_in, li, EVEN_IN)
    a = _lru(p, conv_w, conv_b, ra_w, ra_b, ix_w, ix_b, lam)
    b = _attention(p, sink, rope[0], rope[1])
    return _out_proj_even(a, b, x, mt, w_out, li)


def _odd_mixer(x, mt, norm_w, w_in, li, conv_w, conv_b, gate_b, hnorm_w, w_out):
    p = _in_proj(x, norm_w, mt, w_in, li, ODD_MAIN)
    wg = jnp.pad(w_in[li, :, ODD_MAIN:], ((0, 0), (0, LANE - N_GATES)))
    gates = _gate_proj(x, norm_w, mt, wg)
    h = _mlstm(p, gates, conv_w, conv_b, gate_b)
    return _out_proj_odd(h, p, hnorm_w, x, mt, w_out, li)


def _router_logits_kernel(x_ref, nw_ref, sh_ref, sc_ref, w_ref, o_ref):
    t = pl.program_id(1)
    h = _norm_mod_rows(x_ref[0], nw_ref[...], sh_ref[0, 0], sc_ref[0, 0], t * TT)

    def split(a):
        hi = a.astype(BF16)
        return hi, (a - hi.astype(F32)).astype(BF16)

    nt = lambda a, b: lax.dot_general(a, b, (((1,), (1,)), ((), ())), preferred_element_type=F32)
    (w_hi, w_lo), (h_hi, h_lo) = split(w_ref[...]), split(h)
    o_ref[0] = nt(w_hi, h_hi) + (nt(w_hi, h_lo) + nt(w_lo, h_hi))


def _router_logits(x, nw, mt, w_router):
    return pl.pallas_call(
        _router_logits_kernel,
        grid=(BATCH, N_TT),
        in_specs=[
            pl.BlockSpec((1, TT, D), lambda b, t: (b, t, 0)),
            pl.BlockSpec((1, D), lambda b, t: (0, 0)),
            pl.BlockSpec((1, 1, 2, D), lambda b, t: (3, b, 0, 0)),
            pl.BlockSpec((1, 1, 2, D), lambda b, t: (4, b, 0, 0)),
            pl.BlockSpec((N_EXPERTS, D), lambda b, t: (0, 0)),
        ],
        out_specs=pl.BlockSpec((1, N_EXPERTS, TT), lambda b, t: (b, 0, t)),
        out_shape=jax.ShapeDtypeStruct((BATCH, N_EXPERTS, S), F32),
        compiler_params=_cparams(("arbitrary", "arbitrary")),
        name="router_logits",
    )(x, nw.reshape(1, D), mt, mt, w_router.T)


def _cumsum_lanes(x):
    n = x.shape[1]
    lane = lax.broadcasted_iota(I32, x.shape, 1)
    k = 1
    while k < n:
        x = x + jnp.where(lane >= k, pltpu.roll(x, k, axis=1), 0.0)
        k *= 2
    return x


def _cumsum_rows_excl(x):
    n = x.shape[0]
    row = lax.broadcasted_iota(I32, x.shape, 0)
    inc = x
    k = 1
    while k < n:
        inc = inc + jnp.where(row >= k, pltpu.roll(inc, k, axis=0), 0.0)
        k *= 2
    return inc - x


def _split3(x):
    hi = x.astype(BF16).astype(F32)
    r = x - hi
    mid = r.astype(BF16).astype(F32)
    return hi, mid, (r - mid).astype(BF16).astype(F32)


def _router_select_kernel(lg_ref, idxl_ref, gl_ref, dstl_ref, idxc_ref, gc_ref, dstc_ref, base_ref, vt_ref):
    ne = N_EXPERTS
    lg = lg_ref[0]
    ex = jnp.exp(lg - jnp.max(lg, axis=0, keepdims=True))
    aff = ex / jnp.sum(ex, axis=0, keepdims=True)
    bits = pltpu.bitcast(aff, I32)
    lane = lax.broadcasted_iota(I32, (ne, S), 1)
    sel = jnp.zeros((ne, S), F32)
    pos = jnp.zeros((ne, S), F32)
    for lo, hi, cap in ((0, CTX, CAP_CTX), (CTX, S, CAP_LAT)):
        vb = jnp.where((lane >= lo) & (lane < hi), bits, -1)
        thr = jnp.zeros((ne, 1), I32)
        for bit in range(30, -1, -1):
            cand = thr | (1 << bit)
            cnt = jnp.sum(jnp.where(vb >= cand, 1.0, 0.0), axis=1, keepdims=True)
            thr = jnp.where(cnt >= cap, cand, thr)
        gt = vb > thr
        eq = jnp.where(vb == thr, 1.0, 0.0)
        need = cap - jnp.sum(jnp.where(gt, 1.0, 0.0), axis=1, keepdims=True)
        eq_rank = _cumsum_lanes(eq) - eq
        s_seg = jnp.where(gt | ((eq > 0.0) & (eq_rank < need)), 1.0, 0.0)
        sel = sel + s_seg
        pos = pos + s_seg * (_cumsum_lanes(s_seg) - s_seg)
    cnt_tok = jnp.broadcast_to(jnp.sum(sel, axis=0, keepdims=True), (ne, S))
    base = _cumsum_lanes(cnt_tok) - cnt_tok
    dest = base + _cumsum_rows_excl(sel)
    base_ref[0] = jnp.concatenate([base[0:1], base[0:1] + cnt_tok[0:1], jnp.zeros((SUBLANE - 2, S), F32)],
                                  axis=0).astype(I32)
    posm = jnp.where(sel > 0.0, pos, -1.0)
    lane_f = lane[0:1].astype(F32)
    idx_hi = jnp.floor(lane_f * (1.0 / 64.0))
    idx_lo = lane_f - 64.0 * idx_hi
    dst_hi = jnp.floor(dest * (1.0 / 128.0))
    dst_lo = dest - 128.0 * dst_hi
    g_hi, g_mid, g_lo = _split3(aff)
    for e in range(ne):
        vt_ref[e] = jnp.concatenate([idx_hi, idx_lo, dst_hi[e:e + 1], dst_lo[e:e + 1], g_hi[e:e + 1],
                                     g_mid[e:e + 1], g_lo[e:e + 1], posm[e:e + 1]], axis=0)
    slot = lax.broadcasted_iota(I32, (LANE, 1), 0).astype(F32)

    def compact(vt, prow, s0):
        onehot = jnp.where(prow == slot + s0, 1.0, 0.0).astype(BF16)
        res = lax.dot_general(vt, onehot, (((1,), (1,)), ((), ())), preferred_element_type=F32)
        return (res[0:1] * 64.0 + res[1:2]).astype(I32), res[4:5] + res[5:6] + res[6:7], \
            (res[2:3] * 128.0 + res[3:4]).astype(I32)

    def per_expert(e, _):
        blk = vt_ref[e]
        prow = blk[SUBLANE - 1:SUBLANE]
        vt = blk.astype(BF16)
        idxc_ref[0, e], gc_ref[0, e], dstc_ref[0, e] = compact(vt[:, :CTX], prow[:, :CTX], 0.0)
        for sc in range(CAP_LAT // LANE):
            cs = slice(sc * LANE, (sc + 1) * LANE)
            idxl_ref[0, e, :, cs], gl_ref[0, e, :, cs], dstl_ref[0, e, :, cs] = compact(
                vt[:, CTX:], prow[:, CTX:], float(sc * LANE))
        return 0

    lax.fori_loop(0, ne, per_expert, 0)


def _router_select(logits):
    ne = N_EXPERTS
    out = lambda n, dt: jax.ShapeDtypeStruct((BATCH, ne, 1, n), dt)
    ospec = lambda n: pl.BlockSpec((1, ne, 1, n), lambda b: (b, 0, 0, 0))
    res = pl.pallas_call(
        _router_select_kernel,
        grid=(BATCH,),
        in_specs=[pl.BlockSpec((1, ne, S), lambda b: (b, 0, 0))],
        out_specs=[ospec(CAP_LAT), ospec(CAP_LAT), ospec(CAP_LAT), ospec(LANE), ospec(LANE), ospec(LANE),
                   pl.BlockSpec((1, SUBLANE, S), lambda b: (b, 0, 0))],
        out_shape=[out(CAP_LAT, I32), out(CAP_LAT, F32), out(CAP_LAT, I32), out(LANE, I32), out(LANE, F32),
                   out(LANE, I32), jax.ShapeDtypeStruct((BATCH, SUBLANE, S), I32)],
        scratch_shapes=[pltpu.VMEM((ne, SUBLANE, S), F32)],
        compiler_params=_cparams(("arbitrary",)),
        name="router_select",
    )(logits)
    return [r.reshape(BATCH, ne, r.shape[-1]) for r in res[:6]] + [res[6]]


ROW_UNROLL = 8


def _start_rows(n_rows, start_one):
    def body(i, _):
        for u in range(ROW_UNROLL):
            start_one(i * ROW_UNROLL + u)
        return 0

    lax.fori_loop(0, n_rows // ROW_UNROLL, body, 0)


def _ffn_up_kernel(src_ref, x_hbm, nw_ref, sh_ref, sc_ref, wg_ref, wu_ref, o_ref, xs_ref, gbuf_ref, sem):
    j = pl.program_id(1)

    @pl.when(j == 0)
    def _():
        def gather(b):
            slot = b % 2
            _start_rows(RPS, lambda r: pltpu.make_async_copy(
                x_hbm.at[pl.ds(src_ref[0, 0, b * RPS + r], 1)], gbuf_ref.at[slot, pl.ds(r, 1)],
                sem.at[slot]).start())

        gather(0)
        for b in range(BATCH):
            slot = b % 2
            if b + 1 < BATCH:
                gather(b + 1)
            pltpu.make_async_copy(x_hbm.at[pl.ds(0, RPS)], gbuf_ref.at[slot], sem.at[slot]).wait()
            is_ctx = lax.broadcasted_iota(I32, (RPS, 1), 0) < CAP_CTX
            y = _rms(gbuf_ref[slot]) * nw_ref[...]
            y = y * (1.0 + _ctx_select(is_ctx, sc_ref[0, b])) + _ctx_select(is_ctx, sh_ref[0, b])
            xs_ref[b * RPS:(b + 1) * RPS, :] = y.astype(BF16)

    xs = xs_ref[...]
    hg = jnp.dot(xs, wg_ref[0].astype(BF16), preferred_element_type=F32)
    hu = jnp.dot(xs, wu_ref[0].astype(BF16), preferred_element_type=F32)
    o_ref[0] = (_silu(hg) * hu).astype(BF16)


def _ffn_up(src_rows, x2d, nw, mt, w_gate, w_up, li):
    tf = 256
    assert RPS % ROW_UNROLL == 0
    return pl.pallas_call(
        _ffn_up_kernel,
        grid=(N_EXPERTS, D_EXPERT // tf),
        in_specs=[
            pl.BlockSpec((1, 1, R_EXP), lambda e, j: (e, 0, 0), memory_space=pltpu.SMEM),
            pl.BlockSpec(memory_space=pl.ANY),
            pl.BlockSpec((1, D), lambda e, j: (0, 0)),
            pl.BlockSpec((1, BATCH, 2, D), lambda e, j: (3, 0, 0, 0)),
            pl.BlockSpec((1, BATCH, 2, D), lambda e, j: (4, 0, 0, 0)),
            pl.BlockSpec((None, 1, D, tf), lambda e, j: (li, e, 0, j)),
            pl.BlockSpec((None, 1, D, tf), lambda e, j: (li, e, 0, j)),
        ],
        out_specs=pl.BlockSpec((1, R_EXP, tf), lambda e, j: (e, 0, j)),
        out_shape=jax.ShapeDtypeStruct((N_EXPERTS, R_EXP, D_EXPERT), BF16),
        scratch_shapes=[pltpu.VMEM((R_EXP, D), BF16), pltpu.VMEM((2, RPS, D), F32),
                        pltpu.SemaphoreType.DMA((2,))],
        compiler_params=_cparams(("arbitrary", "arbitrary")),
        name="ffn_up",
    )(src_rows.reshape(N_EXPERTS, 1, R_EXP), x2d, nw.reshape(1, D), mt, mt, w_gate, w_up)


def _ffn_down_kernel(dst_ref, hid_ref, w_ref, g_ref, y_hbm, ybuf_ref, sem):
    i = pl.program_id(1)
    step = pl.program_id(0) * BATCH + i
    slot = i % 2
    acc = jnp.dot(hid_ref[0], w_ref[0].astype(BF16), preferred_element_type=F32)
    ybuf_ref[slot] = acc * g_ref[0]

    def drain(s):
        pltpu.make_async_copy(ybuf_ref.at[s], y_hbm.at[pl.ds(0, RPS)], sem.at[s]).wait()

    @pl.when(step > 0)
    def _():
        drain(1 - slot)

    _start_rows(RPS, lambda r: pltpu.make_async_copy(
        ybuf_ref.at[slot, pl.ds(r, 1)], y_hbm.at[pl.ds(dst_ref[0, 0, i * RPS + r], 1)], sem.at[slot]).start())

    @pl.when(step == N_EXPERTS * BATCH - 1)
    def _():
        drain(slot)


def _ffn_down(dst_rows, hid, w_down, g_col, li):
    assert BATCH % 2 == 0
    return pl.pallas_call(
        _ffn_down_kernel,
        grid=(N_EXPERTS, BATCH),
        in_specs=[
            pl.BlockSpec((1, 1, R_EXP), lambda e, i: (e, 0, 0), memory_space=pltpu.SMEM),
            pl.BlockSpec((1, RPS, D_EXPERT), lambda e, i: (e, i, 0)),
            pl.BlockSpec((None, 1, D_EXPERT, D), lambda e, i: (li, e, 0, 0)),
            pl.BlockSpec((1, RPS, 1), lambda e, i: (e, i, 0)),
        ],
        out_specs=pl.BlockSpec(memory_space=pl.ANY),
        out_shape=jax.ShapeDtypeStruct((BATCH * PAIRS, D), F32),
        scratch_shapes=[pltpu.VMEM((2, RPS, D), F32), pltpu.SemaphoreType.DMA((2,))],
        compiler_params=_cparams(("arbitrary", "arbitrary")),
        name="ffn_down",
    )(dst_rows.reshape(N_EXPERTS, 1, R_EXP), hid, w_down, g_col)


def _combine_kernel(cs_ref, ce_ref, ys_hbm, x_ref, base_ref, basen_ref, g_ref, o_ref, buf_ref, acc_ref, sem):
    b = pl.program_id(0)
    t = pl.program_id(1)
    lo = cs_ref[b, t]
    hi = ce_ref[b, t]

    def chunk_copy(bb, c):
        return pltpu.make_async_copy(ys_hbm.at[pl.ds(bb * PAIRS + c * TT, TT)], buf_ref.at[c % 2], sem.at[c % 2])

    @pl.when((b == 0) & (t == 0) & (lo < hi))
    def _():
        chunk_copy(b, lo).start()

    acc_ref[...] = jnp.zeros((TT, D), F32)
    base = base_ref[0]
    basen = basen_ref[0]
    lane = lax.broadcasted_iota(I32, (1, TT), 1)

    def body(c, _):
        @pl.when(c + 1 < hi)
        def _():
            chunk_copy(b, c + 1).start()

        chunk_copy(b, c).wait()
        r = c * TT + lane
        onehot = jnp.where((base <= r) & (r < basen), 1.0, 0.0).astype(BF16)
        y = buf_ref[c % 2]
        y_hi = y.astype(BF16)
        y_lo = (y - y_hi.astype(F32)).astype(BF16)
        acc_ref[...] += (jnp.dot(onehot, y_hi, preferred_element_type=F32)
                         + jnp.dot(onehot, y_lo, preferred_element_type=F32))
        return 0

    lax.fori_loop(lo, hi, body, 0)

    wrap = t + 1 == N_TT
    nb = jnp.where(wrap, b + 1, b)
    nt = jnp.where(wrap, 0, t + 1)

    @pl.when(nb < BATCH)
    def _():
        nlo = cs_ref[nb, nt]

        @pl.when(nlo < ce_ref[nb, nt])
        def _():
            chunk_copy(nb, nlo).start()

    gate = jnp.where(t == 0, g_ref[0, 0, 0:1, :], g_ref[0, 0, 1:2, :])
    o_ref[0] = x_ref[0] + gate * acc_ref[...]


def _combine(cs, ce, ys, x, base_col, basen_col, mt):
    grid_spec = pltpu.PrefetchScalarGridSpec(
        num_scalar_prefetch=2,
        grid=(BATCH, N_TT),
        in_specs=[
            pl.BlockSpec(memory_space=pl.ANY),
            pl.BlockSpec((1, TT, D), lambda b, t, *_: (b, t, 0)),
            pl.BlockSpec((1, TT, 1), lambda b, t, *_: (b, t, 0)),
            pl.BlockSpec((1, TT, 1), lambda b, t, *_: (b, t, 0)),
            pl.BlockSpec((1, 1, 2, D), lambda b, t, *_: (5, b, 0, 0)),
        ],
        out_specs=pl.BlockSpec((1, TT, D), lambda b, t, *_: (b, t, 0)),
        scratch_shapes=[pltpu.VMEM((2, TT, D), F32), pltpu.VMEM((TT, D), F32), pltpu.SemaphoreType.DMA((2,))],
    )
    return pl.pallas_call(
        _combine_kernel,
        grid_spec=grid_spec,
        out_shape=jax.ShapeDtypeStruct((BATCH, S, D), F32),
        compiler_params=_cparams(("arbitrary", "arbitrary")),
        name="moe_combine",
    )(cs, ce, ys, x, base_col, basen_col, mt)


def _moe(x, mt, norm_w, w_router, w_gate, w_up, w_down, li):
    logits = _router_logits(x, norm_w, mt, w_router)
    idx_l, g_l, dst_l, idx_c, g_c, dst_c, bases = _router_select(logits)
    boff = jnp.arange(BATCH, dtype=I32)[:, None, None]

    def rows(c, l, off):
        r = jnp.concatenate([c[:, :, :CAP_CTX], l], axis=2) + off
        return jnp.transpose(r, (1, 0, 2)).reshape(N_EXPERTS, R_EXP)

    src_rows = rows(idx_c, idx_l, boff * S)
    dst_rows = rows(dst_c, dst_l, boff * PAIRS)
    g_col = rows(g_c, g_l, 0.0).reshape(N_EXPERTS, R_EXP, 1)
    hid = _ffn_up(src_rows, x.reshape(BATCH * S, D), norm_w, mt, w_gate, w_up, li)
    ys = _ffn_down(dst_rows, hid, w_down, g_col, li)
    base, basen = bases[:, 0, :], bases[:, 1, :]
    cs = base[:, ::TT] // TT
    ce = (basen[:, TT - 1::TT] + TT - 1) // TT
    return _combine(cs, ce, ys, x, base.reshape(BATCH, S, 1), basen.reshape(BATCH, S, 1), mt)


def kernel(x, c, ctx, c_ctx, ada_w, ada_b, norm_mix_w, norm_ffn_w,
           ev_w_in, ev_conv_w, ev_conv_b, ev_ra_w, ev_ra_b, ev_ix_w, ev_ix_b, ev_lambda, ev_sink, ev_w_out,
           od_w_in, od_conv_w, od_conv_b, od_gate_b, od_hnorm_w, od_w_out,
           moe_router, moe_w_gate, moe_w_up, moe_w_down, final_norm_w):
    assert x.shape == (BATCH, SEQ, D) and ctx.shape == (BATCH, CTX, D)
    cc = jnp.zeros((SUBLANE, D), F32).at[:BATCH].set(c).at[BATCH].set(c_ctx)
    mod = _modulation(cc, ada_w, ada_b)
    rope = _rope_tables()
    ev_w_in, ev_w_out, od_w_in, od_w_out = (w.astype(BF16) for w in (ev_w_in, ev_w_out, od_w_in, od_w_out))
    xs = jnp.concatenate([ctx, x], axis=1)
    for layer in range(DEPTH):
        mt = _mod_table(mod[layer])
        i = layer // 2
        if layer % 2 == 0:
            xs = _even_mixer(xs, mt, norm_mix_w[layer], ev_w_in, i, ev_conv_w[i], ev_conv_b[i], ev_ra_w[i],
                             ev_ra_b[i], ev_ix_w[i], ev_ix_b[i], ev_lambda[i], ev_sink[i], ev_w_out, rope)
        else:
            xs = _odd_mixer(xs, mt, norm_mix_w[layer], od_w_in, i, od_conv_w[i], od_conv_b[i], od_gate_b[i],
                            od_hnorm_w[i], od_w_out)
        xs = _moe(xs, mt, norm_ffn_w[layer], moe_router[layer], moe_w_gate, moe_w_up, moe_w_down, layer)
    return _final_norm(xs, final_norm_w)
```

```python
import functools

import jax
import jax.numpy as jnp
from jax import lax
from jax.experimental import pallas as pl
from jax.experimental.pallas import tpu as pltpu

F32 = jnp.float32
BF16 = jnp.bfloat16
I32 = jnp.int32

D = 2048
BATCH = 4
SEQ = 4096
CTX = 256
S = CTX + SEQ
DEPTH = 4
N_MOD = 6
EPS = 1e-6
GRID_W = 64

LRU_W = 1024
LRU_BLOCKS = 8
LRU_BLOCK = 128
LRU_C = 8.0
CONV_W = 4
ATT_HEADS = 8
KV_HEADS = 2
GROUP = ATT_HEADS // KV_HEADS
HEAD_DIM = 128
ATT_W = ATT_HEADS * HEAD_DIM
KV_W = KV_HEADS * HEAD_DIM
WINDOW = 128
ROPE_PAIRS = HEAD_DIM // 4
ROPE_BASE = 10000.0
EVEN_IN = 2 * LRU_W + ATT_W + 2 * KV_W

M_HEADS = 8
M_DK = 128
M_DV = 256
M_QK = M_HEADS * M_DK
M_V = M_HEADS * M_DV
ODD_MAIN = 2 * M_QK + 2 * M_V
N_GATES = 4 * M_HEADS

N_EXPERTS = 16
EC_FACTOR = 2
D_EXPERT = 1536
CAP_LAT = EC_FACTOR * SEQ // N_EXPERTS
CAP_CTX = EC_FACTOR * CTX // N_EXPERTS
RPS = CAP_CTX + CAP_LAT
R_EXP = BATCH * RPS
PAIRS = N_EXPERTS * RPS

LANE = 128
SUBLANE = 8
VMEM_LIMIT = 56 * 1024 * 1024
TM = S // 4
TILES_PER_SAMPLE = S // TM
TN = 512
TT = 256
N_TT = S // TT
CHUNK = 128
N_CHUNK = S // CHUNK
CTX_CHUNKS = CTX // CHUNK


def _cparams(sem, vmem=VMEM_LIMIT):
    return pltpu.CompilerParams(dimension_semantics=sem, vmem_limit_bytes=vmem)


def _sigmoid(x):
    return 1.0 / (1.0 + jnp.exp(-x))


def _silu(x):
    return x * _sigmoid(x)


def _softplus(x):
    return jnp.maximum(x, 0.0) + jnp.log1p(jnp.exp(-jnp.abs(x)))


def _log_sigmoid(x):
    return -_softplus(-x)


def _gelu_tanh(x):
    return 0.5 * x * (1.0 + jnp.tanh(0.7978845608028654 * (x + 0.044715 * (x * x * x))))


def _rms(x):
    return x * lax.rsqrt(jnp.mean(x * x, axis=-1, keepdims=True) + EPS)


def _ctx_select(is_ctx, tab):
    return jnp.where(is_ctx, tab[0:1, :], tab[1:2, :])


def _mod_kernel(c_ref, w_ref, b_ref, o_ref):
    a = _silu(c_ref[...]).astype(BF16)
    o_ref[0] = jnp.dot(a, w_ref[0].astype(BF16), preferred_element_type=F32) + b_ref[0]


def _modulation(cc, ada_w, ada_b):
    tn = 1024
    return pl.pallas_call(
        _mod_kernel,
        grid=(DEPTH, N_MOD * D // tn),
        in_specs=[
            pl.BlockSpec((SUBLANE, D), lambda l, j: (0, 0)),
            pl.BlockSpec((1, D, tn), lambda l, j: (l, 0, j)),
            pl.BlockSpec((1, 1, tn), lambda l, j: (l, 0, j)),
        ],
        out_specs=pl.BlockSpec((1, SUBLANE, tn), lambda l, j: (l, 0, j)),
        out_shape=jax.ShapeDtypeStruct((DEPTH, SUBLANE, N_MOD * D), F32),
        compiler_params=_cparams(("arbitrary", "arbitrary")),
        name="adaln_mod",
    )(cc, ada_w, ada_b.reshape(DEPTH, 1, N_MOD * D))


def _mod_table(mod_layer):
    m = mod_layer.reshape(SUBLANE, N_MOD, D)
    lat = jnp.transpose(m[:BATCH], (1, 0, 2))
    ctx = jnp.broadcast_to(m[BATCH][:, None, :], (N_MOD, BATCH, D))
    return jnp.stack([ctx, lat], axis=2)


def _norm_mod_rows(x, nw, sh_tab, sc_tab, row0):
    rows = x.shape[0]
    is_ctx = (row0 + lax.broadcasted_iota(I32, (rows, 1), 0)) < CTX
    y = _rms(x) * nw
    return y * (1.0 + _ctx_select(is_ctx, sc_tab)) + _ctx_select(is_ctx, sh_tab)


def _in_proj_kernel(x_ref, nw_ref, sh_ref, sc_ref, w_ref, *rest):
    i = pl.program_id(0)
    j = pl.program_id(1)
    o_ref, xn_ref = rest[-3 if len(rest) == 4 else 0], rest[-1]

    @pl.when(j == 0)
    def _():
        row0 = (i % TILES_PER_SAMPLE) * TM
        xn = _norm_mod_rows(x_ref[0], nw_ref[...], sh_ref[0, 0], sc_ref[0, 0], row0).astype(BF16)
        xn_ref[...] = xn
        if len(rest) == 4:
            rest[2][0] = jnp.dot(xn, rest[0][...].astype(BF16), preferred_element_type=F32)

    o_ref[0] = jnp.dot(xn_ref[...], w_ref[...].astype(BF16), preferred_element_type=F32)


def _in_proj(x, nw, mt, w, li, n_out, w_extra=None):
    tps = TILES_PER_SAMPLE
    row = lambda width, col: pl.BlockSpec((1, TM, width), lambda i, j: (i // tps, i % tps, col(j)))
    in_specs = [
        row(D, lambda j: 0),
        pl.BlockSpec((1, D), lambda i, j: (0, 0)),
        pl.BlockSpec((1, 1, 2, D), lambda i, j: (0, i // tps, 0, 0)),
        pl.BlockSpec((1, 1, 2, D), lambda i, j: (1, i // tps, 0, 0)),
        pl.BlockSpec((None, D, TN), lambda i, j: (li, 0, j)),
    ]
    out_specs = [row(TN, lambda j: j)]
    out_shape = [jax.ShapeDtypeStruct((BATCH, S, n_out), F32)]
    args = [x, nw.reshape(1, D), mt, mt, w]
    if w_extra is not None:
        in_specs.append(pl.BlockSpec((D, LANE), lambda i, j: (0, 0)))
        out_specs.append(row(LANE, lambda j: 0))
        out_shape.append(jax.ShapeDtypeStruct((BATCH, S, LANE), F32))
        args.append(w_extra)
    res = pl.pallas_call(
        _in_proj_kernel,
        grid=(BATCH * tps, n_out // TN),
        in_specs=in_specs,
        out_specs=out_specs,
        out_shape=out_shape,
        scratch_shapes=[pltpu.VMEM((TM, D), BF16)],
        compiler_params=_cparams(("arbitrary", "arbitrary")),
        name="in_proj",
    )(*args)
    return res[0] if w_extra is None else res


def _residual_epilogue(i, x_ref, g_ref, acc, o_ref):
    tm = acc.shape[0]
    row0 = (i % (S // tm)) * tm
    is_ctx = (row0 + lax.broadcasted_iota(I32, (tm, 1), 0)) < CTX
    o_ref[0] = x_ref[0] + _ctx_select(is_ctx, g_ref[0, 0]) * acc


def _out_proj_even_kernel(a1_ref, a2_ref, x_ref, g_ref, w_ref, o_ref, a_ref):
    i = pl.program_id(0)
    j = pl.program_id(1)

    @pl.when(j == 0)
    def _():
        a_ref[:, :LRU_W] = a1_ref[0].astype(BF16)
        a_ref[:, LRU_W:] = a2_ref[0].astype(BF16)

    acc = jnp.dot(a_ref[...], w_ref[...].astype(BF16), preferred_element_type=F32)
    _residual_epilogue(i, x_ref, g_ref, acc, o_ref)


def _out_proj_odd_kernel(h_ref, og_ref, hw_ref, x_ref, g_ref, w_ref, o_ref, a_ref):
    i = pl.program_id(0)
    j = pl.program_id(1)

    @pl.when(j == 0)
    def _():
        for h in range(M_HEADS):
            sl = slice(h * M_DV, (h + 1) * M_DV)
            hn = _rms(h_ref[0, :, sl]) * hw_ref[:, sl] * _sigmoid(og_ref[0, :, sl])
            a_ref[:, sl] = hn.astype(BF16)

    acc = jnp.dot(a_ref[...], w_ref[...].astype(BF16), preferred_element_type=F32)
    _residual_epilogue(i, x_ref, g_ref, acc, o_ref)


def _row_spec(tm, width, col=None):
    tps = S // tm
    if col is None:
        return pl.BlockSpec((1, tm, width), lambda i, j: (i // tps, i % tps, j))
    return pl.BlockSpec((1, tm, width), lambda i, j: (i // tps, i % tps, col))


def _out_proj_even(a1, a2, x, mt, w, li):
    tm = TM
    return pl.pallas_call(
        _out_proj_even_kernel,
        grid=(BATCH * S // tm, D // TN),
        in_specs=[
            _row_spec(tm, LRU_W, 0),
            _row_spec(tm, ATT_W, 0),
            _row_spec(tm, TN),
            pl.BlockSpec((1, 1, 2, TN), lambda i, j: (2, i // (S // tm), 0, j)),
            pl.BlockSpec((None, D, TN), lambda i, j: (li, 0, j)),
        ],
        out_specs=_row_spec(tm, TN),
        out_shape=jax.ShapeDtypeStruct((BATCH, S, D), F32),
        scratch_shapes=[pltpu.VMEM((tm, D), BF16)],
        compiler_params=_cparams(("arbitrary", "arbitrary")),
        name="out_proj_even",
    )(a1, a2, x, mt, w)


def _out_proj_odd(h, p_odd, hnorm_w, x, mt, w, li):
    tm = TM // 2
    return pl.pallas_call(
        _out_proj_odd_kernel,
        grid=(BATCH * S // tm, D // TN),
        in_specs=[
            _row_spec(tm, M_V, 0),
            _row_spec(tm, M_V, (2 * M_QK + M_V) // M_V),
            pl.BlockSpec((1, M_V), lambda i, j: (0, 0)),
            _row_spec(tm, TN),
            pl.BlockSpec((1, 1, 2, TN), lambda i, j: (2, i // (S // tm), 0, j)),
            pl.BlockSpec((None, D, TN), lambda i, j: (li, 0, j)),
        ],
        out_specs=_row_spec(tm, TN),
        out_shape=jax.ShapeDtypeStruct((BATCH, S, D), F32),
        scratch_shapes=[pltpu.VMEM((tm, D), BF16)],
        compiler_params=_cparams(("arbitrary", "arbitrary")),
        name="out_proj_odd",
    )(h, p_odd, hnorm_w.reshape(1, M_V), x, mt, w)


def _seg_conv(x, cw, cb):
    n = x.shape[0]
    row = lax.broadcasted_iota(I32, (n, 1), 0)
    seg = row < CTX
    y = cb
    for j in range(CONV_W):
        off = j - CONV_W // 2
        if off == 0:
            tap = x
        else:
            src = row + off
            ok = (src >= 0) & (src < n) & ((src < CTX) == seg)
            tap = jnp.where(ok, pltpu.roll(x, (-off) % n, axis=0), 0.0)
        y = y + tap * cw[j:j + 1, :]
    return y


def _bwd_chunk(j):
    return jnp.where(j < CTX_CHUNKS, CTX_CHUNKS - 1 - j, N_CHUNK + CTX_CHUNKS - 1 - j)


def _lin_scan(a, b, h_in, reverse):
    t_len, c = a.shape
    ng = t_len // SUBLANE
    a = a.reshape(ng, SUBLANE, c)
    b = b.reshape(ng, SUBLANE, c)
    sub = lax.broadcasted_iota(I32, a.shape, 1)
    k = 1
    while k < SUBLANE:
        shift = SUBLANE - k if reverse else k
        ok = (sub < SUBLANE - k) if reverse else (sub >= k)
        a_s = pltpu.roll(a, shift, axis=1)
        b_s = pltpu.roll(b, shift, axis=1)
        b = jnp.where(ok, a * b_s + b, b)
        a = jnp.where(ok, a * a_s, a)
        k *= 2
    edge = 0 if reverse else SUBLANE - 1
    hs = [None] * ng
    for g in (range(ng - 1, -1, -1) if reverse else range(ng)):
        hs[g] = b[g] + a[g] * h_in
        h_in = hs[g][edge:edge + 1, :]
    return jnp.concatenate(hs, axis=0)


LRU_UNROLL = 2


def _lru_kernel(xa_ref, ya_ref, cw_ref, cb_ref, raw_ref, rab_ref, ixw_ref, ixb_ref, lam_ref, o_ref, xc_ref, hf_ref):
    xc_ref[...] = _seg_conv(xa_ref[0], cw_ref[...], cb_ref[...])

    def gates(x, d):
        xb = x.astype(BF16)
        r = _sigmoid(jnp.dot(xb, raw_ref[d, 0].astype(BF16), preferred_element_type=F32) + rab_ref[d, 0])
        i = _sigmoid(jnp.dot(xb, ixw_ref[d, 0].astype(BF16), preferred_element_type=F32) + ixb_ref[d, 0])
        log_a = (-LRU_C * r) * _softplus(-lam_ref[d, 0])
        a = jnp.exp(log_a)
        return a, jnp.sqrt(-jnp.tanh(log_a) * (a * a + 1.0)) * (i * x)

    hf_ref[...] = jnp.zeros((S, LRU_BLOCK), F32)

    def one(chunk, h, d):
        rows = pl.ds(pl.multiple_of(chunk * CHUNK, CHUNK), CHUNK)
        a, b = gates(xc_ref[rows, :], d)
        h_all = _lin_scan(a, b, h, d == 1)
        hf_ref[rows, :] += h_all
        return h_all[0:1, :] if d == 1 else h_all[CHUNK - 1:CHUNK, :]

    def body(i, carry):
        hf, hb = carry
        for u in range(LRU_UNROLL):
            j = i * LRU_UNROLL + u
            hf, hb = one(j, hf, 0), one(_bwd_chunk(j), hb, 1)
        return hf, hb

    zero = jnp.zeros((1, LRU_BLOCK), F32)
    lax.fori_loop(0, N_CHUNK // LRU_UNROLL, body, (zero, zero))
    o_ref[0] = (hf_ref[...] * _gelu_tanh(ya_ref[0])).astype(o_ref.dtype)


def _lru(p_even, conv_w, conv_b, ra_w, ra_b, ix_w, ix_b, lam):
    nb = LRU_BLOCKS
    blk4 = lambda: pl.BlockSpec((2, 1, LRU_BLOCK, LRU_BLOCK), lambda b, k: (0, k, 0, 0))
    vec4 = lambda: pl.BlockSpec((2, 1, 1, LRU_BLOCK), lambda b, k: (0, k, 0, 0))
    return pl.pallas_call(
        _lru_kernel,
        grid=(BATCH, nb),
        in_specs=[
            pl.BlockSpec((1, S, LRU_BLOCK), lambda b, k: (b, 0, k)),
            pl.BlockSpec((1, S, LRU_BLOCK), lambda b, k: (b, 0, nb + k)),
            pl.BlockSpec((CONV_W, LRU_BLOCK), lambda b, k: (0, k)),
            pl.BlockSpec((1, LRU_BLOCK), lambda b, k: (0, k)),
            blk4(), vec4(), blk4(), vec4(), vec4(),
        ],
        out_specs=pl.BlockSpec((1, S, LRU_BLOCK), lambda b, k: (b, 0, k)),
        out_shape=jax.ShapeDtypeStruct((BATCH, S, LRU_W), BF16),
        scratch_shapes=[pltpu.VMEM((S, LRU_BLOCK), F32), pltpu.VMEM((S, LRU_BLOCK), F32)],
        compiler_params=_cparams(("arbitrary", "arbitrary")),
        name="rglru",
    )(p_even, p_even, conv_w, conv_b.reshape(1, LRU_W), ra_w, ra_b.reshape(2, nb, 1, LRU_BLOCK),
      ix_w, ix_b.reshape(2, nb, 1, LRU_BLOCK), lam.reshape(2, nb, 1, LRU_BLOCK))


def _rope(x, cos, sin):
    lane = lax.broadcasted_iota(I32, (1, HEAD_DIM), 1)
    first = (lane % (2 * ROPE_PAIRS)) < ROPE_PAIRS
    swapped = jnp.where(first, pltpu.roll(x, HEAD_DIM - ROPE_PAIRS, axis=1), pltpu.roll(x, ROPE_PAIRS, axis=1))
    return x * cos + swapped * sin


def _attn_kernel(sink_ref, q_ref, kp_ref, ko_ref, kn_ref, vp_ref, vo_ref, vn_ref, ck_ref, cv_ref,
                 cq_ref, sq_ref, cp_ref, sp_ref, cn_ref, sn_ref, o_ref):
    t = pl.program_id(1)
    nq = CHUNK
    lat = t >= CTX_CHUNKS
    c_lo = jnp.where(lat, jnp.where(t > CTX_CHUNKS, 0, nq), 0)
    c_hi = jnp.where(lat, jnp.where(t < N_CHUNK - 1, 3 * nq, 2 * nq), 0)
    r = lax.broadcasted_iota(I32, (nq, CTX + 3 * nq), 0)
    c = lax.broadcasted_iota(I32, (nq, CTX + 3 * nq), 1) - CTX
    band_ok = (jnp.abs(c - nq - r) <= WINDOW) & (c >= c_lo) & (c < c_hi)
    bias = jnp.where((c < 0) | band_ok, 0.0, -jnp.inf)
    bias = jnp.concatenate([bias] * GROUP, axis=0)
    scale = HEAD_DIM ** -0.5
    for g in range(KV_HEADS):
        ks = slice(g * HEAD_DIM, (g + 1) * HEAD_DIM)
        keys = jnp.concatenate([
            ck_ref[0, :, ks],
            _rope(kp_ref[0, :, ks], cp_ref[...], sp_ref[...]),
            _rope(ko_ref[0, :, ks], cq_ref[...], sq_ref[...]),
            _rope(kn_ref[0, :, ks], cn_ref[...], sn_ref[...]),
        ], axis=0).astype(BF16)
        vals = jnp.concatenate([cv_ref[0, :, ks], vp_ref[0, :, ks], vo_ref[0, :, ks], vn_ref[0, :, ks]],
                               axis=0).astype(BF16)
        heads = [g * GROUP + hh for hh in range(GROUP)]
        qg = jnp.concatenate([_rope(q_ref[0, :, h * HEAD_DIM:(h + 1) * HEAD_DIM], cq_ref[...], sq_ref[...])
                              for h in heads], axis=0).astype(BF16)
        sink = jnp.concatenate([jnp.full((nq, 1), sink_ref[h], F32) for h in heads], axis=0)
        s = lax.dot_general(qg, keys, (((1,), (1,)), ((), ())), preferred_element_type=F32) * scale + bias
        m = jnp.maximum(jnp.max(s, axis=-1, keepdims=True), sink)
        p = jnp.exp(s - m)
        den = jnp.sum(p, axis=-1, keepdims=True) + jnp.exp(sink - m)
        out = jnp.dot((p / den).astype(BF16), vals, preferred_element_type=F32)
        for hh, h in enumerate(heads):
            o_ref[0, :, h * HEAD_DIM:(h + 1) * HEAD_DIM] = out[hh * nq:(hh + 1) * nq].astype(o_ref.dtype)


def _attention(p_even, sink, cos_t, sin_t):
    qc = 2 * LRU_W // ATT_W
    kc = (2 * LRU_W + ATT_W) // KV_W
    vc = kc + 1
    lo, hi = CTX_CHUNKS, N_CHUNK - 1
    prev = lambda t: jnp.clip(t - 1, lo, hi)
    nxt = lambda t: jnp.clip(t + 1, lo, hi)
    own = lambda t: t
    kv = lambda col, f: pl.BlockSpec((1, CHUNK, KV_W), lambda b, t: (b, f(t), col))
    tab = lambda f: pl.BlockSpec((CHUNK, HEAD_DIM), lambda b, t: (f(t), 0))
    return pl.pallas_call(
        _attn_kernel,
        grid=(BATCH, N_CHUNK),
        in_specs=[
            pl.BlockSpec(memory_space=pltpu.SMEM),
            pl.BlockSpec((1, CHUNK, ATT_W), lambda b, t: (b, t, qc)),
            kv(kc, prev), kv(kc, own), kv(kc, nxt),
            kv(vc, prev), kv(vc, own), kv(vc, nxt),
            pl.BlockSpec((1, CTX, KV_W), lambda b, t: (b, 0, kc)),
            pl.BlockSpec((1, CTX, KV_W), lambda b, t: (b, 0, vc)),
            tab(own), tab(own), tab(prev), tab(prev), tab(nxt), tab(nxt),
        ],
        out_specs=pl.BlockSpec((1, CHUNK, ATT_W), lambda b, t: (b, t, 0)),
        out_shape=jax.ShapeDtypeStruct((BATCH, S, ATT_W), BF16),
        compiler_params=_cparams(("arbitrary", "arbitrary")),
        name="window_attention",
    )(sink, p_even, p_even, p_even, p_even, p_even, p_even, p_even, p_even, p_even,
      cos_t, sin_t, cos_t, sin_t, cos_t, sin_t)


def _rope_tables():
    inv = jnp.power(ROPE_BASE, -jnp.arange(ROPE_PAIRS, dtype=F32) / ROPE_PAIRS)
    pos = jnp.arange(SEQ)
    row_ang = (pos // GRID_W).astype(F32)[:, None] * inv
    col_ang = (pos % GRID_W).astype(F32)[:, None] * inv
    cos = jnp.concatenate([jnp.cos(row_ang)] * 2 + [jnp.cos(col_ang)] * 2, axis=-1)
    sin = jnp.concatenate([-jnp.sin(row_ang), jnp.sin(row_ang), -jnp.sin(col_ang), jnp.sin(col_ang)], axis=-1)
    cos = jnp.concatenate([jnp.ones((CTX, HEAD_DIM), F32), cos], axis=0)
    sin = jnp.concatenate([jnp.zeros((CTX, HEAD_DIM), F32), sin], axis=0)
    return cos, sin


def _mlstm_local(qb, kb, k, vb, li_row, b_row, reverse):
    ln = qb.shape[0]
    row = lax.broadcasted_iota(I32, (ln, ln), 0)
    col = lax.broadcasted_iota(I32, (ln, ln), 1)
    b_r = jnp.broadcast_to(b_row, (ln, ln))
    li_r = jnp.broadcast_to(li_row, (ln, ln))
    b = b_r.T
    li = li_r.T
    causal = (col >= row) if reverse else (col <= row)
    logw = jnp.where(causal, b - b_r + li_r, -jnp.inf)
    m_loc = jnp.max(logw, axis=1, keepdims=True)
    s = lax.dot_general(qb, kb, (((1,), (1,)), ((), ())), preferred_element_type=F32) * jnp.exp(logw - m_loc)
    num = jnp.dot(s.astype(BF16), vb, preferred_element_type=F32)
    den = jnp.sum(s, axis=1, keepdims=True)
    edge = 0 if reverse else ln - 1
    b_last = b[edge:edge + 1, :]
    log_u = b_last - b + li
    mu_loc = jnp.max(log_u, axis=0, keepdims=True)
    uk = jnp.exp(log_u - mu_loc) * k
    kv = lax.dot_general(uk.astype(BF16), vb, (((0,), (0,)), ((), ())), preferred_element_type=F32)
    return dict(qb=qb, b=b, m_loc=m_loc, num=num, den=den, b_last=b_last, mu_loc=mu_loc, kv=kv,
                ksum=jnp.sum(uk, axis=0, keepdims=True))


def _mlstm_apply(loc, state):
    c0, n0, m0 = state
    b, m_loc = loc["b"], loc["m_loc"]
    m = jnp.maximum(b + m0, m_loc)
    intra = jnp.exp(m_loc - m)
    inter = jnp.exp(b + m0 - m)
    qc = jnp.dot(loc["qb"], c0.astype(BF16), preferred_element_type=F32)
    n8 = jnp.broadcast_to(n0, (SUBLANE, n0.shape[1])).astype(BF16)
    qn = lax.dot_general(loc["qb"], n8, (((1,), (1,)), ((), ())), preferred_element_type=F32)[:, :1]
    num = jnp.concatenate([intra, intra], axis=1) * loc["num"] + jnp.concatenate([inter, inter], axis=1) * qc
    den = intra[:, :1] * loc["den"] + inter[:, :1] * qn
    h = num / jnp.maximum(jnp.abs(den), jnp.exp(-m[:, :1]))
    m_new = jnp.maximum(loc["b_last"] + m0, loc["mu_loc"])
    decay = jnp.exp(loc["b_last"] + m0 - m_new)
    grow = jnp.exp(loc["mu_loc"] - m_new)
    c_new = decay[:, :1] * c0 + grow[:, :1] * loc["kv"]
    n_new = decay * n0 + grow * loc["ksum"]
    return h, (c_new, n_new, m_new)


MLSTM_UNROLL = 2


def _mlstm_kernel(gb_ref, q_ref, k_ref, v_ref, g_ref, cwq_ref, cbq_ref, cwk_ref, cbk_ref, o_ref,
                  qb_ref, kb_ref, kc_ref, vb_ref, gr_ref):
    hd = pl.program_id(1)
    ln = CHUNK
    qb_ref[...] = (_silu(_seg_conv(q_ref[0], cwq_ref[...], cbq_ref[...])) * (M_DK ** -0.5)).astype(BF16)
    kc = _silu(_seg_conv(k_ref[0], cwk_ref[...], cbk_ref[...]))
    kc_ref[...] = kc
    kb_ref[...] = kc.astype(BF16)
    vb_ref[...] = v_ref[0].astype(BF16)
    o_ref[0] = jnp.zeros((S, M_DV), F32)

    def rows_of(chunk):
        return pl.ds(pl.multiple_of(chunk * ln, ln), ln)

    sub = lax.broadcasted_iota(I32, (N_GATES, 1), 0)

    def gate_rows(i, _):
        for u in range(MLSTM_UNROLL):
            c = i * MLSTM_UNROLL + u
            gt = g_ref[0, rows_of(c), :].T[:N_GATES]
            pick = lambda ty: jnp.sum(jnp.where(sub == ty * M_HEADS + hd, gt, 0.0), axis=0, keepdims=True) \
                + gb_ref[ty, hd]
            gr_ref[c] = jnp.concatenate([pick(0), _log_sigmoid(pick(1)), pick(2), _log_sigmoid(pick(3)),
                                         jnp.zeros((SUBLANE - 4, ln), F32)], axis=0)
        return 0

    lax.fori_loop(0, N_CHUNK // MLSTM_UNROLL, gate_rows, 0)
    gr = gr_ref[...]
    lane = lax.broadcasted_iota(I32, gr.shape, 2)
    kind = lax.broadcasted_iota(I32, gr.shape, 1)
    pre = suf = gr
    kk = 1
    while kk < ln:
        pre = pre + jnp.where(lane >= kk, pltpu.roll(pre, kk, axis=2), 0.0)
        suf = suf + jnp.where(lane < ln - kk, pltpu.roll(suf, ln - kk, axis=2), 0.0)
        kk *= 2
    gr_ref[...] = jnp.where(kind == 1, pre, jnp.where(kind == 3, suf, gr))

    def local(chunk, d):
        rows = rows_of(chunk)
        gr = gr_ref[chunk]
        return _mlstm_local(qb_ref[rows, :], kb_ref[rows, :], kc_ref[rows, :], vb_ref[rows, :],
                            gr[2 * d:2 * d + 1], gr[2 * d + 1:2 * d + 2], d == 1)

    def body(i, carry):
        steps = [i * MLSTM_UNROLL + u for u in range(MLSTM_UNROLL)]
        chunks = ([j for j in steps], [_bwd_chunk(j) for j in steps])
        locs = [[local(c, d) for c in chunks[d]] for d in range(2)]
        states = list(carry)
        for u in range(MLSTM_UNROLL):
            for d in range(2):
                h, states[d] = _mlstm_apply(locs[d][u], states[d])
                o_ref[0, rows_of(chunks[d][u]), :] += h
        return tuple(states)

    zero = (jnp.zeros((M_DK, M_DV), F32), jnp.zeros((1, M_DK), F32), jnp.zeros((1, ln), F32))
    lax.fori_loop(0, N_CHUNK // MLSTM_UNROLL, body, (zero, zero))


def _mlstm(p_odd, gates, conv_w, conv_b, gate_b):
    nh = M_HEADS
    cw = lambda off: pl.BlockSpec((CONV_W, M_DK), lambda b, h: (0, off + h))
    cb = lambda off: pl.BlockSpec((1, M_DK), lambda b, h: (0, off + h))
    conv_b = conv_b.reshape(1, 2 * M_QK)
    assert N_CHUNK % MLSTM_UNROLL == 0 and CHUNK == M_DK
    return pl.pallas_call(
        _mlstm_kernel,
        grid=(BATCH, nh),
        in_specs=[
            pl.BlockSpec(memory_space=pltpu.SMEM),
            pl.BlockSpec((1, S, M_DK), lambda b, h: (b, 0, h)),
            pl.BlockSpec((1, S, M_DK), lambda b, h: (b, 0, nh + h)),
            pl.BlockSpec((1, S, M_DV), lambda b, h: (b, 0, 2 * M_QK // M_DV + h)),
            pl.BlockSpec((1, S, LANE), lambda b, h: (b, 0, 0)),
            cw(0), cb(0), cw(nh), cb(nh),
        ],
        out_specs=pl.BlockSpec((1, S, M_DV), lambda b, h: (b, 0, h)),
        out_shape=jax.ShapeDtypeStruct((BATCH, S, M_V), F32),
        scratch_shapes=[pltpu.VMEM((S, M_DK), BF16), pltpu.VMEM((S, M_DK), BF16), pltpu.VMEM((S, M_DK), F32),
                        pltpu.VMEM((S, M_DV), BF16), pltpu.VMEM((N_CHUNK, SUBLANE, CHUNK), F32)],
        compiler_params=_cparams(("arbitrary", "arbitrary")),
        name="mlstm",
    )(gate_b, p_odd, p_odd, p_odd, gates, conv_w, conv_b, conv_w, conv_b)


def _even_mixer(x, mt, norm_w, w_in, li, conv_w, conv_b, ra_w, ra_b, ix_w, ix_b, lam, sink, w_out, rope):
    p = _in_proj(x, norm_w, mt, w_in, li, EVEN_IN)
    a = _lru(p, conv_w, conv_b, ra_w, ra_b, ix_w, ix_b, lam)
    b = _attention(p, sink, rope[0], rope[1])
    return _out_proj_even(a, b, x, mt, w_out, li)


def _odd_mixer(x, mt, norm_w, w_in, li, conv_w, conv_b, gate_b, hnorm_w, w_out):
    wg = jnp.pad(w_in[li, :, ODD_MAIN:], ((0, 0), (0, LANE - N_GATES)))
    p, gates = _in_proj(x, norm_w, mt, w_in, li, ODD_MAIN, w_extra=wg)
    h = _mlstm(p, gates, conv_w, conv_b, gate_b)
    return _out_proj_odd(h, p, hnorm_w, x, mt, w_out, li)


def _router_logits_kernel(x_ref, nw_ref, sh_ref, sc_ref, w_ref, o_ref):
    t = pl.program_id(1)
    h = _norm_mod_rows(x_ref[0], nw_ref[...], sh_ref[0, 0], sc_ref[0, 0], t * TT)

    def split(a):
        hi = a.astype(BF16)
        return hi, (a - hi.astype(F32)).astype(BF16)

    nt = lambda a, b: lax.dot_general(a, b, (((1,), (1,)), ((), ())), preferred_element_type=F32)
    (w_hi, w_lo), (h_hi, h_lo) = split(w_ref[...]), split(h)
    o_ref[0] = nt(w_hi, h_hi) + (nt(w_hi, h_lo) + nt(w_lo, h_hi))


def _router_logits(x, nw, mt, w_router):
    return pl.pallas_call(
        _router_logits_kernel,
        grid=(BATCH, N_TT),
        in_specs=[
            pl.BlockSpec((1, TT, D), lambda b, t: (b, t, 0)),
            pl.BlockSpec((1, D), lambda b, t: (0, 0)),
            pl.BlockSpec((1, 1, 2, D), lambda b, t: (3, b, 0, 0)),
            pl.BlockSpec((1, 1, 2, D), lambda b, t: (4, b, 0, 0)),
            pl.BlockSpec((N_EXPERTS, D), lambda b, t: (0, 0)),
        ],
        out_specs=pl.BlockSpec((1, N_EXPERTS, TT), lambda b, t: (b, 0, t)),
        out_shape=jax.ShapeDtypeStruct((BATCH, N_EXPERTS, S), F32),
        compiler_params=_cparams(("arbitrary", "arbitrary")),
        name="router_logits",
    )(x, nw.reshape(1, D), mt, mt, w_router.T)


def _cumsum_lanes(x):
    n = x.shape[1]
    lane = lax.broadcasted_iota(I32, x.shape, 1)
    k = 1
    while k < n:
        x = x + jnp.where(lane >= k, pltpu.roll(x, k, axis=1), 0.0)
        k *= 2
    return x


def _cumsum_rows_excl(x):
    n = x.shape[0]
    row = lax.broadcasted_iota(I32, x.shape, 0)
    inc = x
    k = 1
    while k < n:
        inc = inc + jnp.where(row >= k, pltpu.roll(inc, k, axis=0), 0.0)
        k *= 2
    return inc - x


def _split3(x):
    hi = x.astype(BF16).astype(F32)
    r = x - hi
    mid = r.astype(BF16).astype(F32)
    return hi, mid, (r - mid).astype(BF16).astype(F32)


def _router_select_kernel(lg_ref, idxl_ref, gl_ref, dstl_ref, idxc_ref, gc_ref, dstc_ref, base_ref, vt_ref):
    ne = N_EXPERTS
    lg = lg_ref[0]
    ex = jnp.exp(lg - jnp.max(lg, axis=0, keepdims=True))
    aff = ex / jnp.sum(ex, axis=0, keepdims=True)
    bits = pltpu.bitcast(aff, I32)
    lane = lax.broadcasted_iota(I32, (ne, S), 1)
    sel = jnp.zeros((ne, S), F32)
    pos = jnp.zeros((ne, S), F32)
    for lo, hi, cap in ((0, CTX, CAP_CTX), (CTX, S, CAP_LAT)):
        vb = jnp.where((lane >= lo) & (lane < hi), bits, -1)
        thr = jnp.zeros((ne, 1), I32)
        for bit in range(30, -1, -1):
            cand = thr | (1 << bit)
            cnt = jnp.sum(jnp.where(vb >= cand, 1.0, 0.0), axis=1, keepdims=True)
            thr = jnp.where(cnt >= cap, cand, thr)
        gt = vb > thr
        eq = jnp.where(vb == thr, 1.0, 0.0)
        need = cap - jnp.sum(jnp.where(gt, 1.0, 0.0), axis=1, keepdims=True)
        eq_rank = _cumsum_lanes(eq) - eq
        s_seg = jnp.where(gt | ((eq > 0.0) & (eq_rank < need)), 1.0, 0.0)
        sel = sel + s_seg
        pos = pos + s_seg * (_cumsum_lanes(s_seg) - s_seg)
    cnt_tok = jnp.broadcast_to(jnp.sum(sel, axis=0, keepdims=True), (ne, S))
    base = _cumsum_lanes(cnt_tok) - cnt_tok
    dest = base + _cumsum_rows_excl(sel)
    base_ref[0] = jnp.concatenate([base[0:1], base[0:1] + cnt_tok[0:1], jnp.zeros((SUBLANE - 2, S), F32)],
                                  axis=0).astype(I32)
    posm = jnp.where(sel > 0.0, pos, -1.0)
    lane_f = lane[0:1].astype(F32)
    idx_hi = jnp.floor(lane_f * (1.0 / 64.0))
    idx_lo = lane_f - 64.0 * idx_hi
    dst_hi = jnp.floor(dest * (1.0 / 128.0))
    dst_lo = dest - 128.0 * dst_hi
    g_hi, g_mid, g_lo = _split3(aff)
    for e in range(ne):
        vt_ref[e] = jnp.concatenate([idx_hi, idx_lo, dst_hi[e:e + 1], dst_lo[e:e + 1], g_hi[e:e + 1],
                                     g_mid[e:e + 1], g_lo[e:e + 1], posm[e:e + 1]], axis=0)
    slot = lax.broadcasted_iota(I32, (LANE, 1), 0).astype(F32)

    def compact(vt, prow, s0):
        onehot = jnp.where(prow == slot + s0, 1.0, 0.0).astype(BF16)
        res = lax.dot_general(vt, onehot, (((1,), (1,)), ((), ())), preferred_element_type=F32)
        return (res[0:1] * 64.0 + res[1:2]).astype(I32), res[4:5] + res[5:6] + res[6:7], \
            (res[2:3] * 128.0 + res[3:4]).astype(I32)

    def per_expert(e, _):
        blk = vt_ref[e]
        prow = blk[SUBLANE - 1:SUBLANE]
        vt = blk.astype(BF16)
        idxc_ref[0, e], gc_ref[0, e], dstc_ref[0, e] = compact(vt[:, :CTX], prow[:, :CTX], 0.0)
        for sc in range(CAP_LAT // LANE):
            cs = slice(sc * LANE, (sc + 1) * LANE)
            idxl_ref[0, e, :, cs], gl_ref[0, e, :, cs], dstl_ref[0, e, :, cs] = compact(
                vt[:, CTX:], prow[:, CTX:], float(sc * LANE))
        return 0

    lax.fori_loop(0, ne, per_expert, 0)


def _router_select(logits):
    ne = N_EXPERTS
    out = lambda n, dt: jax.ShapeDtypeStruct((BATCH, ne, 1, n), dt)
    ospec = lambda n: pl.BlockSpec((1, ne, 1, n), lambda b: (b, 0, 0, 0))
    res = pl.pallas_call(
        _router_select_kernel,
        grid=(BATCH,),
        in_specs=[pl.BlockSpec((1, ne, S), lambda b: (b, 0, 0))],
        out_specs=[ospec(CAP_LAT), ospec(CAP_LAT), ospec(CAP_LAT), ospec(LANE), ospec(LANE), ospec(LANE),
                   pl.BlockSpec((1, SUBLANE, S), lambda b: (b, 0, 0))],
        out_shape=[out(CAP_LAT, I32), out(CAP_LAT, F32), out(CAP_LAT, I32), out(LANE, I32), out(LANE, F32),
                   out(LANE, I32), jax.ShapeDtypeStruct((BATCH, SUBLANE, S), I32)],
        scratch_shapes=[pltpu.VMEM((ne, SUBLANE, S), F32)],
        compiler_params=_cparams(("arbitrary",)),
        name="router_select",
    )(logits)
    return [r.reshape(BATCH, ne, r.shape[-1]) for r in res[:6]] + [res[6]]


ROW_UNROLL = 8


def _start_rows(n_rows, start_one):
    def body(i, _):
        for u in range(ROW_UNROLL):
            start_one(i * ROW_UNROLL + u)
        return 0

    lax.fori_loop(0, n_rows // ROW_UNROLL, body, 0)


def _ffn_up_kernel(src_ref, x_hbm, nw_ref, sh_ref, sc_ref, wg_ref, wu_ref, o_ref, xs_ref, gbuf_ref, sem):
    j = pl.program_id(1)

    @pl.when(j == 0)
    def _():
        def gather(b):
            slot = b % 2
            _start_rows(RPS, lambda r: pltpu.make_async_copy(
                x_hbm.at[pl.ds(src_ref[0, 0, b * RPS + r], 1)], gbuf_ref.at[slot, pl.ds(r, 1)],
                sem.at[slot]).start())

        gather(0)
        for b in range(BATCH):
            slot = b % 2
            if b + 1 < BATCH:
                gather(b + 1)
            pltpu.make_async_copy(x_hbm.at[pl.ds(0, RPS)], gbuf_ref.at[slot], sem.at[slot]).wait()
            is_ctx = lax.broadcasted_iota(I32, (RPS, 1), 0) < CAP_CTX
            y = _rms(gbuf_ref[slot]) * nw_ref[...]
            y = y * (1.0 + _ctx_select(is_ctx, sc_ref[0, b])) + _ctx_select(is_ctx, sh_ref[0, b])
            xs_ref[b * RPS:(b + 1) * RPS, :] = y.astype(BF16)

    xs = xs_ref[...]
    hg = jnp.dot(xs, wg_ref[0].astype(BF16), preferred_element_type=F32)
    hu = jnp.dot(xs, wu_ref[0].astype(BF16), preferred_element_type=F32)
    o_ref[0] = (_silu(hg) * hu).astype(BF16)


def _ffn_up(src_rows, x2d, nw, mt, w_gate, w_up, li):
    tf = 256
    assert RPS % ROW_UNROLL == 0
    return pl.pallas_call(
        _ffn_up_kernel,
        grid=(N_EXPERTS, D_EXPERT // tf),
        in_specs=[
            pl.BlockSpec((1, 1, R_EXP), lambda e, j: (e, 0, 0), memory_space=pltpu.SMEM),
            pl.BlockSpec(memory_space=pl.ANY),
            pl.BlockSpec((1, D), lambda e, j: (0, 0)),
            pl.BlockSpec((1, BATCH, 2, D), lambda e, j: (3, 0, 0, 0)),
            pl.BlockSpec((1, BATCH, 2, D), lambda e, j: (4, 0, 0, 0)),
            pl.BlockSpec((None, 1, D, tf), lambda e, j: (li, e, 0, j)),
            pl.BlockSpec((None, 1, D, tf), lambda e, j: (li, e, 0, j)),
        ],
        out_specs=pl.BlockSpec((1, R_EXP, tf), lambda e, j: (e, 0, j)),
        out_shape=jax.ShapeDtypeStruct((N_EXPERTS, R_EXP, D_EXPERT), BF16),
        scratch_shapes=[pltpu.VMEM((R_EXP, D), BF16), pltpu.VMEM((2, RPS, D), F32),
                        pltpu.SemaphoreType.DMA((2,))],
        compiler_params=_cparams(("arbitrary", "arbitrary")),
        name="ffn_up",
    )(src_rows.reshape(N_EXPERTS, 1, R_EXP), x2d, nw.reshape(1, D), mt, mt, w_gate, w_up)


def _ffn_down_kernel(dst_ref, hid_ref, w_ref, g_ref, y_hbm, ybuf_ref, sem):
    i = pl.program_id(1)
    step = pl.program_id(0) * BATCH + i
    slot = i % 2
    acc = jnp.dot(hid_ref[0], w_ref[0].astype(BF16), preferred_element_type=F32)
    ybuf_ref[slot] = acc * g_ref[0]

    def drain(s):
        pltpu.make_async_copy(ybuf_ref.at[s], y_hbm.at[pl.ds(0, RPS)], sem.at[s]).wait()

    @pl.when(step > 0)
    def _():
        drain(1 - slot)

    _start_rows(RPS, lambda r: pltpu.make_async_copy(
        ybuf_ref.at[slot, pl.ds(r, 1)], y_hbm.at[pl.ds(dst_ref[0, 0, i * RPS + r], 1)], sem.at[slot]).start())

    @pl.when(step == N_EXPERTS * BATCH - 1)
    def _():
        drain(slot)


def _ffn_down(dst_rows, hid, w_down, g_col, li):
    assert BATCH % 2 == 0
    return pl.pallas_call(
        _ffn_down_kernel,
        grid=(N_EXPERTS, BATCH),
        in_specs=[
            pl.BlockSpec((1, 1, R_EXP), lambda e, i: (e, 0, 0), memory_space=pltpu.SMEM),
            pl.BlockSpec((1, RPS, D_EXPERT), lambda e, i: (e, i, 0)),
            pl.BlockSpec((None, 1, D_EXPERT, D), lambda e, i: (li, e, 0, 0)),
            pl.BlockSpec((1, RPS, 1), lambda e, i: (e, i, 0)),
        ],
        out_specs=pl.BlockSpec(memory_space=pl.ANY),
        out_shape=jax.ShapeDtypeStruct((BATCH * PAIRS, D), F32),
        scratch_shapes=[pltpu.VMEM((2, RPS, D), F32), pltpu.SemaphoreType.DMA((2,))],
        compiler_params=_cparams(("arbitrary", "arbitrary")),
        name="ffn_down",
    )(dst_rows.reshape(N_EXPERTS, 1, R_EXP), hid, w_down, g_col)


def _combine_kernel(cs_ref, ce_ref, ys_hbm, x_ref, base_ref, basen_ref, g_ref, *rest):
    fw_ref = rest[0] if len(rest) == 5 else None
    o_ref, buf_ref, acc_ref, sem = rest[-4:]
    b = pl.program_id(0)
    t = pl.program_id(1)
    lo = cs_ref[b, t]
    hi = ce_ref[b, t]

    def chunk_copy(bb, c):
        return pltpu.make_async_copy(ys_hbm.at[pl.ds(bb * PAIRS + c * TT, TT)], buf_ref.at[c % 2], sem.at[c % 2])

    @pl.when((b == 0) & (t == 0) & (lo < hi))
    def _():
        chunk_copy(b, lo).start()

    acc_ref[...] = jnp.zeros((TT, D), F32)
    base = base_ref[0]
    basen = basen_ref[0]
    lane = lax.broadcasted_iota(I32, (1, TT), 1)

    def body(c, _):
        @pl.when(c + 1 < hi)
        def _():
            chunk_copy(b, c + 1).start()

        chunk_copy(b, c).wait()
        r = c * TT + lane
        onehot = jnp.where((base <= r) & (r < basen), 1.0, 0.0).astype(BF16)
        y = buf_ref[c % 2]
        y_hi = y.astype(BF16)
        y_lo = (y - y_hi.astype(F32)).astype(BF16)
        acc_ref[...] += (jnp.dot(onehot, y_hi, preferred_element_type=F32)
                         + jnp.dot(onehot, y_lo, preferred_element_type=F32))
        return 0

    lax.fori_loop(lo, hi, body, 0)

    wrap = t + 1 == N_TT
    nb = jnp.where(wrap, b + 1, b)
    nt = jnp.where(wrap, 0, t + 1)

    @pl.when(nb < BATCH)
    def _():
        nlo = cs_ref[nb, nt]

        @pl.when(nlo < ce_ref[nb, nt])
        def _():
            chunk_copy(nb, nlo).start()

    gate = jnp.where(t == 0, g_ref[0, 0, 0:1, :], g_ref[0, 0, 1:2, :])
    y = x_ref[0] + gate * acc_ref[...]
    o_ref[0] = y if fw_ref is None else _rms(y) * fw_ref[...]


def _combine(cs, ce, ys, x, base_col, basen_col, mt, final_w=None):
    in_specs = [
        pl.BlockSpec(memory_space=pl.ANY),
        pl.BlockSpec((1, TT, D), lambda b, t, *_: (b, t, 0)),
        pl.BlockSpec((1, TT, 1), lambda b, t, *_: (b, t, 0)),
        pl.BlockSpec((1, TT, 1), lambda b, t, *_: (b, t, 0)),
        pl.BlockSpec((1, 1, 2, D), lambda b, t, *_: (5, b, 0, 0)),
    ]
    args = [cs, ce, ys, x, base_col, basen_col, mt]
    if final_w is None:
        out_spec = pl.BlockSpec((1, TT, D), lambda b, t, *_: (b, t, 0))
        out_rows = S
    else:
        assert CTX == TT
        in_specs.append(pl.BlockSpec((1, D), lambda b, t, *_: (0, 0)))
        args.append(final_w.reshape(1, D))
        out_spec = pl.BlockSpec((1, TT, D), lambda b, t, *_: (b, jnp.maximum(t - 1, 0), 0))
        out_rows = SEQ
    grid_spec = pltpu.PrefetchScalarGridSpec(
        num_scalar_prefetch=2,
        grid=(BATCH, N_TT),
        in_specs=in_specs,
        out_specs=out_spec,
        scratch_shapes=[pltpu.VMEM((2, TT, D), F32), pltpu.VMEM((TT, D), F32), pltpu.SemaphoreType.DMA((2,))],
    )
    return pl.pallas_call(
        _combine_kernel,
        grid_spec=grid_spec,
        out_shape=jax.ShapeDtypeStruct((BATCH, out_rows, D), F32),
        compiler_params=_cparams(("arbitrary", "arbitrary")),
        name="moe_combine",
    )(*args)


def _moe(x, mt, norm_w, w_router, w_gate, w_up, w_down, li, final_w=None):
    logits = _router_logits(x, norm_w, mt, w_router)
    idx_l, g_l, dst_l, idx_c, g_c, dst_c, bases = _router_select(logits)
    boff = jnp.arange(BATCH, dtype=I32)[:, None, None]

    def rows(c, l, off):
        r = jnp.concatenate([c[:, :, :CAP_CTX], l], axis=2) + off
        return jnp.transpose(r, (1, 0, 2)).reshape(N_EXPERTS, R_EXP)

    src_rows = rows(idx_c, idx_l, boff * S)
    dst_rows = rows(dst_c, dst_l, boff * PAIRS)
    g_col = rows(g_c, g_l, 0.0).reshape(N_EXPERTS, R_EXP, 1)
    hid = _ffn_up(src_rows, x.reshape(BATCH * S, D), norm_w, mt, w_gate, w_up, li)
    ys = _ffn_down(dst_rows, hid, w_down, g_col, li)
    base, basen = bases[:, 0, :], bases[:, 1, :]
    cs = base[:, ::TT] // TT
    ce = (basen[:, TT - 1::TT] + TT - 1) // TT
    return _combine(cs, ce, ys, x, base.reshape(BATCH, S, 1), basen.reshape(BATCH, S, 1), mt, final_w)


def kernel(x, c, ctx, c_ctx, ada_w, ada_b, norm_mix_w, norm_ffn_w,
           ev_w_in, ev_conv_w, ev_conv_b, ev_ra_w, ev_ra_b, ev_ix_w, ev_ix_b, ev_lambda, ev_sink, ev_w_out,
           od_w_in, od_conv_w, od_conv_b, od_gate_b, od_hnorm_w, od_w_out,
           moe_router, moe_w_gate, moe_w_up, moe_w_down, final_norm_w):
    assert x.shape == (BATCH, SEQ, D) and ctx.shape == (BATCH, CTX, D)
    cc = jnp.zeros((SUBLANE, D), F32).at[:BATCH].set(c).at[BATCH].set(c_ctx)
    mod = _modulation(cc, ada_w, ada_b)
    rope = _rope_tables()
    ev_w_in, ev_w_out, od_w_in, od_w_out = (w.astype(BF16) for w in (ev_w_in, ev_w_out, od_w_in, od_w_out))
    xs = jnp.concatenate([ctx, x], axis=1)
    for layer in range(DEPTH):
        mt = _mod_table(mod[layer])
        i = layer // 2
        if layer % 2 == 0:
            xs = _even_mixer(xs, mt, norm_mix_w[layer], ev_w_in, i, ev_conv_w[i], ev_conv_b[i], ev_ra_w[i],
                             ev_ra_b[i], ev_ix_w[i], ev_ix_b[i], ev_lambda[i], ev_sink[i], ev_w_out, rope)
        else:
            xs = _odd_mixer(xs, mt, norm_mix_w[layer], od_w_in, i, od_conv_w[i], od_conv_b[i], od_gate_b[i],
                            od_hnorm_w[i], od_w_out)
        xs = _moe(xs, mt, norm_ffn_w[layer], moe_router[layer], moe_w_gate, moe_w_up, moe_w_down, layer,
                  final_norm_w if layer == DEPTH - 1 else None)
    return xs
```

```python
import functools

import jax
import jax.numpy as jnp
from jax import lax
from jax.experimental import pallas as pl
from jax.experimental.pallas import tpu as pltpu

F32 = jnp.float32
BF16 = jnp.bfloat16
I32 = jnp.int32

D = 2048
BATCH = 4
SEQ = 4096
CTX = 256
S = CTX + SEQ
DEPTH = 4
N_MOD = 6
EPS = 1e-6
GRID_W = 64

LRU_W = 1024
LRU_BLOCKS = 8
LRU_BLOCK = 128
LRU_C = 8.0
CONV_W = 4
ATT_HEADS = 8
KV_HEADS = 2
GROUP = ATT_HEADS // KV_HEADS
HEAD_DIM = 128
ATT_W = ATT_HEADS * HEAD_DIM
KV_W = KV_HEADS * HEAD_DIM
WINDOW = 128
ROPE_PAIRS = HEAD_DIM // 4
ROPE_BASE = 10000.0
EVEN_IN = 2 * LRU_W + ATT_W + 2 * KV_W

M_HEADS = 8
M_DK = 128
M_DV = 256
M_QK = M_HEADS * M_DK
M_V = M_HEADS * M_DV
ODD_MAIN = 2 * M_QK + 2 * M_V
N_GATES = 4 * M_HEADS

N_EXPERTS = 16
EC_FACTOR = 2
D_EXPERT = 1536
CAP_LAT = EC_FACTOR * SEQ // N_EXPERTS
CAP_CTX = EC_FACTOR * CTX // N_EXPERTS
RPS = CAP_CTX + CAP_LAT
R_EXP = BATCH * RPS
PAIRS = N_EXPERTS * RPS

LANE = 128
SUBLANE = 8
VMEM_LIMIT = 56 * 1024 * 1024
TM = S // 4
TILES_PER_SAMPLE = S // TM
TN = 512
TT = 256
N_TT = S // TT
CHUNK = 128
N_CHUNK = S // CHUNK
CTX_CHUNKS = CTX // CHUNK


def _cparams(sem, vmem=VMEM_LIMIT):
    return pltpu.CompilerParams(dimension_semantics=sem, vmem_limit_bytes=vmem)


def _sigmoid(x):
    return 1.0 / (1.0 + jnp.exp(-x))


def _silu(x):
    return x * _sigmoid(x)


def _softplus(x):
    return jnp.maximum(x, 0.0) + jnp.log1p(jnp.exp(-jnp.abs(x)))


def _log_sigmoid(x):
    return -_softplus(-x)


def _gelu_tanh(x):
    return 0.5 * x * (1.0 + jnp.tanh(0.7978845608028654 * (x + 0.044715 * (x * x * x))))


def _rms(x):
    return x * lax.rsqrt(jnp.mean(x * x, axis=-1, keepdims=True) + EPS)


def _ctx_select(is_ctx, tab):
    return jnp.where(is_ctx, tab[0:1, :], tab[1:2, :])


def _mod_kernel(c_ref, w_ref, b_ref, o_ref):
    a = _silu(c_ref[...]).astype(BF16)
    o_ref[0] = jnp.dot(a, w_ref[0].astype(BF16), preferred_element_type=F32) + b_ref[0]


def _modulation(cc, ada_w, ada_b):
    tn = 1024
    return pl.pallas_call(
        _mod_kernel,
        grid=(DEPTH, N_MOD * D // tn),
        in_specs=[
            pl.BlockSpec((SUBLANE, D), lambda l, j: (0, 0)),
            pl.BlockSpec((1, D, tn), lambda l, j: (l, 0, j)),
            pl.BlockSpec((1, 1, tn), lambda l, j: (l, 0, j)),
        ],
        out_specs=pl.BlockSpec((1, SUBLANE, tn), lambda l, j: (l, 0, j)),
        out_shape=jax.ShapeDtypeStruct((DEPTH, SUBLANE, N_MOD * D), F32),
        compiler_params=_cparams(("arbitrary", "arbitrary")),
        name="adaln_mod",
    )(cc, ada_w, ada_b.reshape(DEPTH, 1, N_MOD * D))


def _mod_table(mod_layer):
    m = mod_layer.reshape(SUBLANE, N_MOD, D)
    lat = jnp.transpose(m[:BATCH], (1, 0, 2))
    ctx = jnp.broadcast_to(m[BATCH][:, None, :], (N_MOD, BATCH, D))
    return jnp.stack([ctx, lat], axis=2)


def _norm_mod_rows(x, nw, sh_tab, sc_tab, row0):
    rows = x.shape[0]
    is_ctx = (row0 + lax.broadcasted_iota(I32, (rows, 1), 0)) < CTX
    y = _rms(x) * nw
    return y * (1.0 + _ctx_select(is_ctx, sc_tab)) + _ctx_select(is_ctx, sh_tab)


def _in_proj_kernel(x_ref, nw_ref, sh_ref, sc_ref, w_ref, *rest):
    i = pl.program_id(0)
    j = pl.program_id(1)
    o_ref, xn_ref = rest[-3 if len(rest) == 4 else 0], rest[-1]

    @pl.when(j == 0)
    def _():
        row0 = (i % TILES_PER_SAMPLE) * TM
        xn = _norm_mod_rows(x_ref[0], nw_ref[...], sh_ref[0, 0], sc_ref[0, 0], row0).astype(BF16)
        xn_ref[...] = xn
        if len(rest) == 4:
            rest[2][0] = jnp.dot(xn, rest[0][...].astype(BF16), preferred_element_type=F32)

    o_ref[0] = jnp.dot(xn_ref[...], w_ref[...].astype(BF16), preferred_element_type=F32)


def _in_proj(x, nw, mt, w, li, n_out, w_extra=None):
    tps = TILES_PER_SAMPLE
    tn = 2 * TN if n_out % (2 * TN) == 0 else TN
    row = lambda width, col: pl.BlockSpec((1, TM, width), lambda i, j: (i // tps, i % tps, col(j)))
    in_specs = [
        row(D, lambda j: 0),
        pl.BlockSpec((1, D), lambda i, j: (0, 0)),
        pl.BlockSpec((1, 1, 2, D), lambda i, j: (0, i // tps, 0, 0)),
        pl.BlockSpec((1, 1, 2, D), lambda i, j: (1, i // tps, 0, 0)),
        pl.BlockSpec((None, D, tn), lambda i, j: (li, 0, j)),
    ]
    out_specs = [row(tn, lambda j: j)]
    out_shape = [jax.ShapeDtypeStruct((BATCH, S, n_out), F32)]
    args = [x, nw.reshape(1, D), mt, mt, w]
    if w_extra is not None:
        in_specs.append(pl.BlockSpec((D, LANE), lambda i, j: (0, 0)))
        out_specs.append(row(LANE, lambda j: 0))
        out_shape.append(jax.ShapeDtypeStruct((BATCH, S, LANE), F32))
        args.append(w_extra)
    res = pl.pallas_call(
        _in_proj_kernel,
        grid=(BATCH * tps, n_out // tn),
        in_specs=in_specs,
        out_specs=out_specs,
        out_shape=out_shape,
        scratch_shapes=[pltpu.VMEM((TM, D), BF16)],
        compiler_params=_cparams(("arbitrary", "arbitrary")),
        name="in_proj",
    )(*args)
    return res[0] if w_extra is None else res


def _residual_epilogue(i, x_ref, g_ref, acc, o_ref):
    tm = acc.shape[0]
    row0 = (i % (S // tm)) * tm
    is_ctx = (row0 + lax.broadcasted_iota(I32, (tm, 1), 0)) < CTX
    o_ref[0] = x_ref[0] + _ctx_select(is_ctx, g_ref[0, 0]) * acc


def _out_proj_even_kernel(a1_ref, a2_ref, x_ref, g_ref, w_ref, o_ref, a_ref):
    i = pl.program_id(0)
    j = pl.program_id(1)

    @pl.when(j == 0)
    def _():
        a_ref[:, :LRU_W] = a1_ref[0].astype(BF16)
        a_ref[:, LRU_W:] = a2_ref[0].astype(BF16)

    acc = jnp.dot(a_ref[...], w_ref[...].astype(BF16), preferred_element_type=F32)
    _residual_epilogue(i, x_ref, g_ref, acc, o_ref)


def _out_proj_odd_kernel(h_ref, og_ref, hw_ref, x_ref, g_ref, w_ref, o_ref, a_ref):
    i = pl.program_id(0)
    j = pl.program_id(1)

    @pl.when(j == 0)
    def _():
        for h in range(M_HEADS):
            sl = slice(h * M_DV, (h + 1) * M_DV)
            hn = _rms(h_ref[0, :, sl]) * hw_ref[:, sl] * _sigmoid(og_ref[0, :, sl])
            a_ref[:, sl] = hn.astype(BF16)

    acc = jnp.dot(a_ref[...], w_ref[...].astype(BF16), preferred_element_type=F32)
    _residual_epilogue(i, x_ref, g_ref, acc, o_ref)


def _row_spec(tm, width, col=None):
    tps = S // tm
    if col is None:
        return pl.BlockSpec((1, tm, width), lambda i, j: (i // tps, i % tps, j))
    return pl.BlockSpec((1, tm, width), lambda i, j: (i // tps, i % tps, col))


def _out_proj_even(a1, a2, x, mt, w, li):
    tm = TM
    return pl.pallas_call(
        _out_proj_even_kernel,
        grid=(BATCH * S // tm, D // TN),
        in_specs=[
            _row_spec(tm, LRU_W, 0),
            _row_spec(tm, ATT_W, 0),
            _row_spec(tm, TN),
            pl.BlockSpec((1, 1, 2, TN), lambda i, j: (2, i // (S // tm), 0, j)),
            pl.BlockSpec((None, D, TN), lambda i, j: (li, 0, j)),
        ],
        out_specs=_row_spec(tm, TN),
        out_shape=jax.ShapeDtypeStruct((BATCH, S, D), F32),
        scratch_shapes=[pltpu.VMEM((tm, D), BF16)],
        compiler_params=_cparams(("arbitrary", "arbitrary")),
        name="out_proj_even",
    )(a1, a2, x, mt, w)


def _out_proj_odd(h, p_odd, hnorm_w, x, mt, w, li):
    tm = TM // 2
    return pl.pallas_call(
        _out_proj_odd_kernel,
        grid=(BATCH * S // tm, D // TN),
        in_specs=[
            _row_spec(tm, M_V, 0),
            _row_spec(tm, M_V, (2 * M_QK + M_V) // M_V),
            pl.BlockSpec((1, M_V), lambda i, j: (0, 0)),
            _row_spec(tm, TN),
            pl.BlockSpec((1, 1, 2, TN), lambda i, j: (2, i // (S // tm), 0, j)),
            pl.BlockSpec((None, D, TN), lambda i, j: (li, 0, j)),
        ],
        out_specs=_row_spec(tm, TN),
        out_shape=jax.ShapeDtypeStruct((BATCH, S, D), F32),
        scratch_shapes=[pltpu.VMEM((tm, D), BF16)],
        compiler_params=_cparams(("arbitrary", "arbitrary")),
        name="out_proj_odd",
    )(h, p_odd, hnorm_w.reshape(1, M_V), x, mt, w)


def _seg_conv(x, cw, cb):
    n = x.shape[0]
    row = lax.broadcasted_iota(I32, (n, 1), 0)
    seg = row < CTX
    y = cb
    for j in range(CONV_W):
        off = j - CONV_W // 2
        if off == 0:
            tap = x
        else:
            src = row + off
            ok = (src >= 0) & (src < n) & ((src < CTX) == seg)
            tap = jnp.where(ok, pltpu.roll(x, (-off) % n, axis=0), 0.0)
        y = y + tap * cw[j:j + 1, :]
    return y


def _bwd_chunk(j):
    return jnp.where(j < CTX_CHUNKS, CTX_CHUNKS - 1 - j, N_CHUNK + CTX_CHUNKS - 1 - j)


def _lin_scan(a, b, h_in, reverse):
    t_len, c = a.shape
    ng = t_len // SUBLANE
    a = a.reshape(ng, SUBLANE, c)
    b = b.reshape(ng, SUBLANE, c)
    sub = lax.broadcasted_iota(I32, a.shape, 1)
    k = 1
    while k < SUBLANE:
        shift = SUBLANE - k if reverse else k
        ok = (sub < SUBLANE - k) if reverse else (sub >= k)
        a_s = pltpu.roll(a, shift, axis=1)
        b_s = pltpu.roll(b, shift, axis=1)
        b = jnp.where(ok, a * b_s + b, b)
        a = jnp.where(ok, a * a_s, a)
        k *= 2
    edge = 0 if reverse else SUBLANE - 1
    hs = [None] * ng
    for g in (range(ng - 1, -1, -1) if reverse else range(ng)):
        hs[g] = b[g] + a[g] * h_in
        h_in = hs[g][edge:edge + 1, :]
    return jnp.concatenate(hs, axis=0)


LRU_UNROLL = 2


def _lru_kernel(xa_ref, ya_ref, cw_ref, cb_ref, raw_ref, rab_ref, ixw_ref, ixb_ref, lam_ref, o_ref, xc_ref, hf_ref):
    xc_ref[...] = _seg_conv(xa_ref[0], cw_ref[...], cb_ref[...])

    def gates(x, d):
        xb = x.astype(BF16)
        r = _sigmoid(jnp.dot(xb, raw_ref[d, 0].astype(BF16), preferred_element_type=F32) + rab_ref[d, 0])
        i = _sigmoid(jnp.dot(xb, ixw_ref[d, 0].astype(BF16), preferred_element_type=F32) + ixb_ref[d, 0])
        log_a = (-LRU_C * r) * _softplus(-lam_ref[d, 0])
        a = jnp.exp(log_a)
        return a, jnp.sqrt(-jnp.tanh(log_a) * (a * a + 1.0)) * (i * x)

    hf_ref[...] = jnp.zeros((S, LRU_BLOCK), F32)

    def one(chunk, h, d):
        rows = pl.ds(pl.multiple_of(chunk * CHUNK, CHUNK), CHUNK)
        a, b = gates(xc_ref[rows, :], d)
        h_all = _lin_scan(a, b, h, d == 1)
        hf_ref[rows, :] += h_all
        return h_all[0:1, :] if d == 1 else h_all[CHUNK - 1:CHUNK, :]

    def body(i, carry):
        hf, hb = carry
        for u in range(LRU_UNROLL):
            j = i * LRU_UNROLL + u
            hf, hb = one(j, hf, 0), one(_bwd_chunk(j), hb, 1)
        return hf, hb

    zero = jnp.zeros((1, LRU_BLOCK), F32)
    lax.fori_loop(0, N_CHUNK // LRU_UNROLL, body, (zero, zero))
    o_ref[0] = (hf_ref[...] * _gelu_tanh(ya_ref[0])).astype(o_ref.dtype)


def _lru(p_even, conv_w, conv_b, ra_w, ra_b, ix_w, ix_b, lam):
    nb = LRU_BLOCKS
    blk4 = lambda: pl.BlockSpec((2, 1, LRU_BLOCK, LRU_BLOCK), lambda b, k: (0, k, 0, 0))
    vec4 = lambda: pl.BlockSpec((2, 1, 1, LRU_BLOCK), lambda b, k: (0, k, 0, 0))
    return pl.pallas_call(
        _lru_kernel,
        grid=(BATCH, nb),
        in_specs=[
            pl.BlockSpec((1, S, LRU_BLOCK), lambda b, k: (b, 0, k)),
            pl.BlockSpec((1, S, LRU_BLOCK), lambda b, k: (b, 0, nb + k)),
            pl.BlockSpec((CONV_W, LRU_BLOCK), lambda b, k: (0, k)),
            pl.BlockSpec((1, LRU_BLOCK), lambda b, k: (0, k)),
            blk4(), vec4(), blk4(), vec4(), vec4(),
        ],
        out_specs=pl.BlockSpec((1, S, LRU_BLOCK), lambda b, k: (b, 0, k)),
        out_shape=jax.ShapeDtypeStruct((BATCH, S, LRU_W), BF16),
        scratch_shapes=[pltpu.VMEM((S, LRU_BLOCK), F32), pltpu.VMEM((S, LRU_BLOCK), F32)],
        compiler_params=_cparams(("arbitrary", "arbitrary")),
        name="rglru",
    )(p_even, p_even, conv_w, conv_b.reshape(1, LRU_W), ra_w, ra_b.reshape(2, nb, 1, LRU_BLOCK),
      ix_w, ix_b.reshape(2, nb, 1, LRU_BLOCK), lam.reshape(2, nb, 1, LRU_BLOCK))


def _rope(x, cos, sin):
    lane = lax.broadcasted_iota(I32, (1, HEAD_DIM), 1)
    first = (lane % (2 * ROPE_PAIRS)) < ROPE_PAIRS
    swapped = jnp.where(first, pltpu.roll(x, HEAD_DIM - ROPE_PAIRS, axis=1), pltpu.roll(x, ROPE_PAIRS, axis=1))
    return x * cos + swapped * sin


def _attn_kernel(sink_ref, q_ref, kp_ref, ko_ref, kn_ref, vp_ref, vo_ref, vn_ref, ck_ref, cv_ref,
                 cq_ref, sq_ref, cp_ref, sp_ref, cn_ref, sn_ref, o_ref):
    t = pl.program_id(1)
    nq = CHUNK
    lat = t >= CTX_CHUNKS
    c_lo = jnp.where(lat, jnp.where(t > CTX_CHUNKS, 0, nq), 0)
    c_hi = jnp.where(lat, jnp.where(t < N_CHUNK - 1, 3 * nq, 2 * nq), 0)
    r = lax.broadcasted_iota(I32, (nq, CTX + 3 * nq), 0)
    c = lax.broadcasted_iota(I32, (nq, CTX + 3 * nq), 1) - CTX
    band_ok = (jnp.abs(c - nq - r) <= WINDOW) & (c >= c_lo) & (c < c_hi)
    bias = jnp.where((c < 0) | band_ok, 0.0, -jnp.inf)
    bias = jnp.concatenate([bias] * GROUP, axis=0)
    scale = HEAD_DIM ** -0.5
    for g in range(KV_HEADS):
        ks = slice(g * HEAD_DIM, (g + 1) * HEAD_DIM)
        keys = jnp.concatenate([
            ck_ref[0, :, ks],
            _rope(kp_ref[0, :, ks], cp_ref[...], sp_ref[...]),
            _rope(ko_ref[0, :, ks], cq_ref[...], sq_ref[...]),
            _rope(kn_ref[0, :, ks], cn_ref[...], sn_ref[...]),
        ], axis=0).astype(BF16)
        vals = jnp.concatenate([cv_ref[0, :, ks], vp_ref[0, :, ks], vo_ref[0, :, ks], vn_ref[0, :, ks]],
                               axis=0).astype(BF16)
        heads = [g * GROUP + hh for hh in range(GROUP)]
        qg = jnp.concatenate([_rope(q_ref[0, :, h * HEAD_DIM:(h + 1) * HEAD_DIM], cq_ref[...], sq_ref[...])
                              for h in heads], axis=0).astype(BF16)
        sink = jnp.concatenate([jnp.full((nq, 1), sink_ref[h], F32) for h in heads], axis=0)
        s = lax.dot_general(qg, keys, (((1,), (1,)), ((), ())), preferred_element_type=F32) * scale + bias
        m = jnp.maximum(jnp.max(s, axis=-1, keepdims=True), sink)
        p = jnp.exp(s - m)
        den = jnp.sum(p, axis=-1, keepdims=True) + jnp.exp(sink - m)
        out = jnp.dot((p / den).astype(BF16), vals, preferred_element_type=F32)
        for hh, h in enumerate(heads):
            o_ref[0, :, h * HEAD_DIM:(h + 1) * HEAD_DIM] = out[hh * nq:(hh + 1) * nq].astype(o_ref.dtype)


def _attention(p_even, sink, cos_t, sin_t):
    qc = 2 * LRU_W // ATT_W
    kc = (2 * LRU_W + ATT_W) // KV_W
    vc = kc + 1
    lo, hi = CTX_CHUNKS, N_CHUNK - 1
    prev = lambda t: jnp.clip(t - 1, lo, hi)
    nxt = lambda t: jnp.clip(t + 1, lo, hi)
    own = lambda t: t
    kv = lambda col, f: pl.BlockSpec((1, CHUNK, KV_W), lambda b, t: (b, f(t), col))
    tab = lambda f: pl.BlockSpec((CHUNK, HEAD_DIM), lambda b, t: (f(t), 0))
    return pl.pallas_call(
        _attn_kernel,
        grid=(BATCH, N_CHUNK),
        in_specs=[
            pl.BlockSpec(memory_space=pltpu.SMEM),
            pl.BlockSpec((1, CHUNK, ATT_W), lambda b, t: (b, t, qc)),
            kv(kc, prev), kv(kc, own), kv(kc, nxt),
            kv(vc, prev), kv(vc, own), kv(vc, nxt),
            pl.BlockSpec((1, CTX, KV_W), lambda b, t: (b, 0, kc)),
            pl.BlockSpec((1, CTX, KV_W), lambda b, t: (b, 0, vc)),
            tab(own), tab(own), tab(prev), tab(prev), tab(nxt), tab(nxt),
        ],
        out_specs=pl.BlockSpec((1, CHUNK, ATT_W), lambda b, t: (b, t, 0)),
        out_shape=jax.ShapeDtypeStruct((BATCH, S, ATT_W), BF16),
        compiler_params=_cparams(("arbitrary", "arbitrary")),
        name="window_attention",
    )(sink, p_even, p_even, p_even, p_even, p_even, p_even, p_even, p_even, p_even,
      cos_t, sin_t, cos_t, sin_t, cos_t, sin_t)


def _rope_tables():
    inv = jnp.power(ROPE_BASE, -jnp.arange(ROPE_PAIRS, dtype=F32) / ROPE_PAIRS)
    pos = jnp.arange(SEQ)
    row_ang = (pos // GRID_W).astype(F32)[:, None] * inv
    col_ang = (pos % GRID_W).astype(F32)[:, None] * inv
    cos = jnp.concatenate([jnp.cos(row_ang)] * 2 + [jnp.cos(col_ang)] * 2, axis=-1)
    sin = jnp.concatenate([-jnp.sin(row_ang), jnp.sin(row_ang), -jnp.sin(col_ang), jnp.sin(col_ang)], axis=-1)
    cos = jnp.concatenate([jnp.ones((CTX, HEAD_DIM), F32), cos], axis=0)
    sin = jnp.concatenate([jnp.zeros((CTX, HEAD_DIM), F32), sin], axis=0)
    return cos, sin


def _mlstm_local(qb, kb, k, vb, li_row, b_row, reverse):
    ln = qb.shape[0]
    row = lax.broadcasted_iota(I32, (ln, ln), 0)
    col = lax.broadcasted_iota(I32, (ln, ln), 1)
    b_r = jnp.broadcast_to(b_row, (ln, ln))
    li_r = jnp.broadcast_to(li_row, (ln, ln))
    b = b_r.T
    li = li_r.T
    causal = (col >= row) if reverse else (col <= row)
    logw = jnp.where(causal, b - b_r + li_r, -jnp.inf)
    m_loc = jnp.max(logw, axis=1, keepdims=True)
    s = lax.dot_general(qb, kb, (((1,), (1,)), ((), ())), preferred_element_type=F32) * jnp.exp(logw - m_loc)
    num = jnp.dot(s.astype(BF16), vb, preferred_element_type=F32)
    den = jnp.sum(s, axis=1, keepdims=True)
    edge = 0 if reverse else ln - 1
    b_last = b[edge:edge + 1, :]
    log_u = b_last - b + li
    mu_loc = jnp.max(log_u, axis=0, keepdims=True)
    uk = jnp.exp(log_u - mu_loc) * k
    kv = lax.dot_general(uk.astype(BF16), vb, (((0,), (0,)), ((), ())), preferred_element_type=F32)
    return dict(qb=qb, b=b, m_loc=m_loc, num=num, den=den, b_last=b_last, mu_loc=mu_loc, kv=kv,
                ksum=jnp.sum(uk, axis=0, keepdims=True))


def _mlstm_apply(loc, state):
    c0, n0, m0 = state
    b, m_loc = loc["b"], loc["m_loc"]
    m = jnp.maximum(b + m0, m_loc)
    intra = jnp.exp(m_loc - m)
    inter = jnp.exp(b + m0 - m)
    qc = jnp.dot(loc["qb"], c0.astype(BF16), preferred_element_type=F32)
    n8 = jnp.broadcast_to(n0, (SUBLANE, n0.shape[1])).astype(BF16)
    qn = lax.dot_general(loc["qb"], n8, (((1,), (1,)), ((), ())), preferred_element_type=F32)[:, :1]
    num = jnp.concatenate([intra, intra], axis=1) * loc["num"] + jnp.concatenate([inter, inter], axis=1) * qc
    den = intra[:, :1] * loc["den"] + inter[:, :1] * qn
    h = num / jnp.maximum(jnp.abs(den), jnp.exp(-m[:, :1]))
    m_new = jnp.maximum(loc["b_last"] + m0, loc["mu_loc"])
    decay = jnp.exp(loc["b_last"] + m0 - m_new)
    grow = jnp.exp(loc["mu_loc"] - m_new)
    c_new = decay[:, :1] * c0 + grow[:, :1] * loc["kv"]
    n_new = decay * n0 + grow * loc["ksum"]
    return h, (c_new, n_new, m_new)


MLSTM_UNROLL = 2


def _mlstm_kernel(gb_ref, q_ref, k_ref, v_ref, g_ref, cwq_ref, cbq_ref, cwk_ref, cbk_ref, o_ref,
                  qb_ref, kb_ref, kc_ref, vb_ref, gr_ref):
    hd = pl.program_id(1)
    ln = CHUNK
    qb_ref[...] = (_silu(_seg_conv(q_ref[0], cwq_ref[...], cbq_ref[...])) * (M_DK ** -0.5)).astype(BF16)
    kc = _silu(_seg_conv(k_ref[0], cwk_ref[...], cbk_ref[...]))
    kc_ref[...] = kc
    kb_ref[...] = kc.astype(BF16)
    vb_ref[...] = v_ref[0].astype(BF16)
    o_ref[0] = jnp.zeros((S, M_DV), F32)

    def rows_of(chunk):
        return pl.ds(pl.multiple_of(chunk * ln, ln), ln)

    sub = lax.broadcasted_iota(I32, (N_GATES, 1), 0)

    def gate_rows(i, _):
        for u in range(MLSTM_UNROLL):
            c = i * MLSTM_UNROLL + u
            gt = g_ref[0, rows_of(c), :].T[:N_GATES]
            pick = lambda ty: jnp.sum(jnp.where(sub == ty * M_HEADS + hd, gt, 0.0), axis=0, keepdims=True) \
                + gb_ref[ty, hd]
            gr_ref[c] = jnp.concatenate([pick(0), _log_sigmoid(pick(1)), pick(2), _log_sigmoid(pick(3)),
                                         jnp.zeros((SUBLANE - 4, ln), F32)], axis=0)
        return 0

    lax.fori_loop(0, N_CHUNK // MLSTM_UNROLL, gate_rows, 0)
    gr = gr_ref[...]
    lane = lax.broadcasted_iota(I32, gr.shape, 2)
    kind = lax.broadcasted_iota(I32, gr.shape, 1)
    pre = suf = gr
    kk = 1
    while kk < ln:
        pre = pre + jnp.where(lane >= kk, pltpu.roll(pre, kk, axis=2), 0.0)
        suf = suf + jnp.where(lane < ln - kk, pltpu.roll(suf, ln - kk, axis=2), 0.0)
        kk *= 2
    gr_ref[...] = jnp.where(kind == 1, pre, jnp.where(kind == 3, suf, gr))

    def local(chunk, d):
        rows = rows_of(chunk)
        gr = gr_ref[chunk]
        return _mlstm_local(qb_ref[rows, :], kb_ref[rows, :], kc_ref[rows, :], vb_ref[rows, :],
                            gr[2 * d:2 * d + 1], gr[2 * d + 1:2 * d + 2], d == 1)

    def body(i, carry):
        steps = [i * MLSTM_UNROLL + u for u in range(MLSTM_UNROLL)]
        chunks = ([j for j in steps], [_bwd_chunk(j) for j in steps])
        locs = [[local(c, d) for c in chunks[d]] for d in range(2)]
        states = list(carry)
        for u in range(MLSTM_UNROLL):
            for d in range(2):
                h, states[d] = _mlstm_apply(locs[d][u], states[d])
                o_ref[0, rows_of(chunks[d][u]), :] += h
        return tuple(states)

    zero = (jnp.zeros((M_DK, M_DV), F32), jnp.zeros((1, M_DK), F32), jnp.zeros((1, ln), F32))
    lax.fori_loop(0, N_CHUNK // MLSTM_UNROLL, body, (zero, zero))


def _mlstm(p_odd, gates, conv_w, conv_b, gate_b):
    nh = M_HEADS
    cw = lambda off: pl.BlockSpec((CONV_W, M_DK), lambda b, h: (0, off + h))
    cb = lambda off: pl.BlockSpec((1, M_DK), lambda b, h: (0, off + h))
    conv_b = conv_b.reshape(1, 2 * M_QK)
    assert N_CHUNK % MLSTM_UNROLL == 0 and CHUNK == M_DK
    return pl.pallas_call(
        _mlstm_kernel,
        grid=(BATCH, nh),
        in_specs=[
            pl.BlockSpec(memory_space=pltpu.SMEM),
            pl.BlockSpec((1, S, M_DK), lambda b, h: (b, 0, h)),
            pl.BlockSpec((1, S, M_DK), lambda b, h: (b, 0, nh + h)),
            pl.BlockSpec((1, S, M_DV), lambda b, h: (b, 0, 2 * M_QK // M_DV + h)),
            pl.BlockSpec((1, S, LANE), lambda b, h: (b, 0, 0)),
            cw(0), cb(0), cw(nh), cb(nh),
        ],
        out_specs=pl.BlockSpec((1, S, M_DV), lambda b, h: (b, 0, h)),
        out_shape=jax.ShapeDtypeStruct((BATCH, S, M_V), F32),
        scratch_shapes=[pltpu.VMEM((S, M_DK), BF16), pltpu.VMEM((S, M_DK), BF16), pltpu.VMEM((S, M_DK), F32),
                        pltpu.VMEM((S, M_DV), BF16), pltpu.VMEM((N_CHUNK, SUBLANE, CHUNK), F32)],
        compiler_params=_cparams(("arbitrary", "arbitrary")),
        name="mlstm",
    )(gate_b, p_odd, p_odd, p_odd, gates, conv_w, conv_b, conv_w, conv_b)


def _even_mixer(x, mt, norm_w, w_in, li, conv_w, conv_b, ra_w, ra_b, ix_w, ix_b, lam, sink, w_out, rope):
    p = _in_proj(x, norm_w, mt, w_in, li, EVEN_IN)
    a = _lru(p, conv_w, conv_b, ra_w, ra_b, ix_w, ix_b, lam)
    b = _attention(p, sink, rope[0], rope[1])
    return _out_proj_even(a, b, x, mt, w_out, li)


def _odd_mixer(x, mt, norm_w, w_in, li, conv_w, conv_b, gate_b, hnorm_w, w_out):
    wg = jnp.pad(w_in[li, :, ODD_MAIN:], ((0, 0), (0, LANE - N_GATES)))
    p, gates = _in_proj(x, norm_w, mt, w_in, li, ODD_MAIN, w_extra=wg)
    h = _mlstm(p, gates, conv_w, conv_b, gate_b)
    return _out_proj_odd(h, p, hnorm_w, x, mt, w_out, li)


def _router_logits_kernel(x_ref, nw_ref, sh_ref, sc_ref, w_ref, o_ref):
    t = pl.program_id(1)
    h = _norm_mod_rows(x_ref[0], nw_ref[...], sh_ref[0, 0], sc_ref[0, 0], t * TT)

    def split(a):
        hi = a.astype(BF16)
        return hi, (a - hi.astype(F32)).astype(BF16)

    nt = lambda a, b: lax.dot_general(a, b, (((1,), (1,)), ((), ())), preferred_element_type=F32)
    (w_hi, w_lo), (h_hi, h_lo) = split(w_ref[...]), split(h)
    o_ref[0] = nt(w_hi, h_hi) + (nt(w_hi, h_lo) + nt(w_lo, h_hi))


def _router_logits(x, nw, mt, w_router):
    return pl.pallas_call(
        _router_logits_kernel,
        grid=(BATCH, N_TT),
        in_specs=[
            pl.BlockSpec((1, TT, D), lambda b, t: (b, t, 0)),
            pl.BlockSpec((1, D), lambda b, t: (0, 0)),
            pl.BlockSpec((1, 1, 2, D), lambda b, t: (3, b, 0, 0)),
            pl.BlockSpec((1, 1, 2, D), lambda b, t: (4, b, 0, 0)),
            pl.BlockSpec((N_EXPERTS, D), lambda b, t: (0, 0)),
        ],
        out_specs=pl.BlockSpec((1, N_EXPERTS, TT), lambda b, t: (b, 0, t)),
        out_shape=jax.ShapeDtypeStruct((BATCH, N_EXPERTS, S), F32),
        compiler_params=_cparams(("arbitrary", "arbitrary")),
        name="router_logits",
    )(x, nw.reshape(1, D), mt, mt, w_router.T)


def _cumsum_lanes(x):
    n = x.shape[1]
    lane = lax.broadcasted_iota(I32, x.shape, 1)
    k = 1
    while k < n:
        x = x + jnp.where(lane >= k, pltpu.roll(x, k, axis=1), 0.0)
        k *= 2
    return x


def _cumsum_rows_excl(x):
    n = x.shape[0]
    row = lax.broadcasted_iota(I32, x.shape, 0)
    inc = x
    k = 1
    while k < n:
        inc = inc + jnp.where(row >= k, pltpu.roll(inc, k, axis=0), 0.0)
        k *= 2
    return inc - x


def _split3(x):
    hi = x.astype(BF16).astype(F32)
    r = x - hi
    mid = r.astype(BF16).astype(F32)
    return hi, mid, (r - mid).astype(BF16).astype(F32)


def _router_select_kernel(lg_ref, idxl_ref, gl_ref, dstl_ref, idxc_ref, gc_ref, dstc_ref, base_ref, vt_ref):
    ne = N_EXPERTS
    lg = lg_ref[0]
    ex = jnp.exp(lg - jnp.max(lg, axis=0, keepdims=True))
    aff = ex / jnp.sum(ex, axis=0, keepdims=True)
    bits = pltpu.bitcast(aff, I32)
    lane = lax.broadcasted_iota(I32, (ne, S), 1)
    sel = jnp.zeros((ne, S), F32)
    pos = jnp.zeros((ne, S), F32)
    for lo, hi, cap in ((0, CTX, CAP_CTX), (CTX, S, CAP_LAT)):
        vb = jnp.where((lane >= lo) & (lane < hi), bits, -1)
        thr = jnp.zeros((ne, 1), I32)
        for bit in range(30, -1, -1):
            cand = thr | (1 << bit)
            cnt = jnp.sum(jnp.where(vb >= cand, 1.0, 0.0), axis=1, keepdims=True)
            thr = jnp.where(cnt >= cap, cand, thr)
        gt = vb > thr
        eq = jnp.where(vb == thr, 1.0, 0.0)
        need = cap - jnp.sum(jnp.where(gt, 1.0, 0.0), axis=1, keepdims=True)
        eq_rank = _cumsum_lanes(eq) - eq
        s_seg = jnp.where(gt | ((eq > 0.0) & (eq_rank < need)), 1.0, 0.0)
        sel = sel + s_seg
        pos = pos + s_seg * (_cumsum_lanes(s_seg) - s_seg)
    cnt_tok = jnp.broadcast_to(jnp.sum(sel, axis=0, keepdims=True), (ne, S))
    base = _cumsum_lanes(cnt_tok) - cnt_tok
    dest = base + _cumsum_rows_excl(sel)
    base_ref[0] = jnp.concatenate([base[0:1], base[0:1] + cnt_tok[0:1], jnp.zeros((SUBLANE - 2, S), F32)],
                                  axis=0).astype(I32)
    posm = jnp.where(sel > 0.0, pos, -1.0)
    lane_f = lane[0:1].astype(F32)
    idx_hi = jnp.floor(lane_f * (1.0 / 64.0))
    idx_lo = lane_f - 64.0 * idx_hi
    dst_hi = jnp.floor(dest * (1.0 / 128.0))
    dst_lo = dest - 128.0 * dst_hi
    g_hi, g_mid, g_lo = _split3(aff)
    for e in range(ne):
        vt_ref[e] = jnp.concatenate([idx_hi, idx_lo, dst_hi[e:e + 1], dst_lo[e:e + 1], g_hi[e:e + 1],
                                     g_mid[e:e + 1], g_lo[e:e + 1], posm[e:e + 1]], axis=0)
    slot = lax.broadcasted_iota(I32, (LANE, 1), 0).astype(F32)

    def compact(vt, prow, s0):
        onehot = jnp.where(prow == slot + s0, 1.0, 0.0).astype(BF16)
        res = lax.dot_general(vt, onehot, (((1,), (1,)), ((), ())), preferred_element_type=F32)
        return (res[0:1] * 64.0 + res[1:2]).astype(I32), res[4:5] + res[5:6] + res[6:7], \
            (res[2:3] * 128.0 + res[3:4]).astype(I32)

    def per_expert(e, _):
        blk = vt_ref[e]
        prow = blk[SUBLANE - 1:SUBLANE]
        vt = blk.astype(BF16)
        idxc_ref[0, e], gc_ref[0, e], dstc_ref[0, e] = compact(vt[:, :CTX], prow[:, :CTX], 0.0)
        for sc in range(CAP_LAT // LANE):
            cs = slice(sc * LANE, (sc + 1) * LANE)
            idxl_ref[0, e, :, cs], gl_ref[0, e, :, cs], dstl_ref[0, e, :, cs] = compact(
                vt[:, CTX:], prow[:, CTX:], float(sc * LANE))
        return 0

    lax.fori_loop(0, ne, per_expert, 0)


def _router_select(logits):
    ne = N_EXPERTS
    out = lambda n, dt: jax.ShapeDtypeStruct((BATCH, ne, 1, n), dt)
    ospec = lambda n: pl.BlockSpec((1, ne, 1, n), lambda b: (b, 0, 0, 0))
    res = pl.pallas_call(
        _router_select_kernel,
        grid=(BATCH,),
        in_specs=[pl.BlockSpec((1, ne, S), lambda b: (b, 0, 0))],
        out_specs=[ospec(CAP_LAT), ospec(CAP_LAT), ospec(CAP_LAT), ospec(LANE), ospec(LANE), ospec(LANE),
                   pl.BlockSpec((1, SUBLANE, S), lambda b: (b, 0, 0))],
        out_shape=[out(CAP_LAT, I32), out(CAP_LAT, F32), out(CAP_LAT, I32), out(LANE, I32), out(LANE, F32),
                   out(LANE, I32), jax.ShapeDtypeStruct((BATCH, SUBLANE, S), I32)],
        scratch_shapes=[pltpu.VMEM((ne, SUBLANE, S), F32)],
        compiler_params=_cparams(("arbitrary",)),
        name="router_select",
    )(logits)
    return [r.reshape(BATCH, ne, r.shape[-1]) for r in res[:6]] + [res[6]]


ROW_UNROLL = 16


def _start_rows(n_rows, start_one):
    def body(i, _):
        for u in range(ROW_UNROLL):
            start_one(i * ROW_UNROLL + u)
        return 0

    lax.fori_loop(0, n_rows // ROW_UNROLL, body, 0)


def _ffn_up_kernel(src_ref, x_hbm, nw_ref, sh_ref, sc_ref, wg_ref, wu_ref, o_ref, xs_ref, gbuf_ref, sem):
    j = pl.program_id(1)

    @pl.when(j == 0)
    def _():
        def gather(b):
            slot = b % 2
            _start_rows(RPS, lambda r: pltpu.make_async_copy(
                x_hbm.at[pl.ds(src_ref[0, 0, b * RPS + r], 1)], gbuf_ref.at[slot, pl.ds(r, 1)],
                sem.at[slot]).start())

        gather(0)
        for b in range(BATCH):
            slot = b % 2
            if b + 1 < BATCH:
                gather(b + 1)
            pltpu.make_async_copy(x_hbm.at[pl.ds(0, RPS)], gbuf_ref.at[slot], sem.at[slot]).wait()
            is_ctx = lax.broadcasted_iota(I32, (RPS, 1), 0) < CAP_CTX
            y = _rms(gbuf_ref[slot]) * nw_ref[...]
            y = y * (1.0 + _ctx_select(is_ctx, sc_ref[0, b])) + _ctx_select(is_ctx, sh_ref[0, b])
            xs_ref[b * RPS:(b + 1) * RPS, :] = y.astype(BF16)

    xs = xs_ref[...]
    hg = jnp.dot(xs, wg_ref[0].astype(BF16), preferred_element_type=F32)
    hu = jnp.dot(xs, wu_ref[0].astype(BF16), preferred_element_type=F32)
    o_ref[0] = (_silu(hg) * hu).astype(BF16)


def _ffn_up(src_rows, x2d, nw, mt, w_gate, w_up, li):
    tf = 256
    assert RPS % ROW_UNROLL == 0
    return pl.pallas_call(
        _ffn_up_kernel,
        grid=(N_EXPERTS, D_EXPERT // tf),
        in_specs=[
            pl.BlockSpec((1, 1, R_EXP), lambda e, j: (e, 0, 0), memory_space=pltpu.SMEM),
            pl.BlockSpec(memory_space=pl.ANY),
            pl.BlockSpec((1, D), lambda e, j: (0, 0)),
            pl.BlockSpec((1, BATCH, 2, D), lambda e, j: (3, 0, 0, 0)),
            pl.BlockSpec((1, BATCH, 2, D), lambda e, j: (4, 0, 0, 0)),
            pl.BlockSpec((None, 1, D, tf), lambda e, j: (li, e, 0, j)),
            pl.BlockSpec((None, 1, D, tf), lambda e, j: (li, e, 0, j)),
        ],
        out_specs=pl.BlockSpec((1, R_EXP, tf), lambda e, j: (e, 0, j)),
        out_shape=jax.ShapeDtypeStruct((N_EXPERTS, R_EXP, D_EXPERT), BF16),
        scratch_shapes=[pltpu.VMEM((R_EXP, D), BF16), pltpu.VMEM((2, RPS, D), F32),
                        pltpu.SemaphoreType.DMA((2,))],
        compiler_params=_cparams(("arbitrary", "arbitrary")),
        name="ffn_up",
    )(src_rows.reshape(N_EXPERTS, 1, R_EXP), x2d, nw.reshape(1, D), mt, mt, w_gate, w_up)


def _ffn_down_kernel(dst_ref, hid_ref, w_ref, g_ref, y_hbm, ybuf_ref, sem):
    i = pl.program_id(1)
    step = pl.program_id(0) * BATCH + i
    slot = i % 2
    acc = jnp.dot(hid_ref[0], w_ref[0].astype(BF16), preferred_element_type=F32)
    ybuf_ref[slot] = acc * g_ref[0]

    def drain(s):
        pltpu.make_async_copy(ybuf_ref.at[s], y_hbm.at[pl.ds(0, RPS)], sem.at[s]).wait()

    @pl.when(step > 0)
    def _():
        drain(1 - slot)

    _start_rows(RPS, lambda r: pltpu.make_async_copy(
        ybuf_ref.at[slot, pl.ds(r, 1)], y_hbm.at[pl.ds(dst_ref[0, 0, i * RPS + r], 1)], sem.at[slot]).start())

    @pl.when(step == N_EXPERTS * BATCH - 1)
    def _():
        drain(slot)


def _ffn_down(dst_rows, hid, w_down, g_col, li):
    assert BATCH % 2 == 0
    return pl.pallas_call(
        _ffn_down_kernel,
        grid=(N_EXPERTS, BATCH),
        in_specs=[
            pl.BlockSpec((1, 1, R_EXP), lambda e, i: (e, 0, 0), memory_space=pltpu.SMEM),
            pl.BlockSpec((1, RPS, D_EXPERT), lambda e, i: (e, i, 0)),
            pl.BlockSpec((None, 1, D_EXPERT, D), lambda e, i: (li, e, 0, 0)),
            pl.BlockSpec((1, RPS, 1), lambda e, i: (e, i, 0)),
        ],
        out_specs=pl.BlockSpec(memory_space=pl.ANY),
        out_shape=jax.ShapeDtypeStruct((BATCH * PAIRS, D), F32),
        scratch_shapes=[pltpu.VMEM((2, RPS, D), F32), pltpu.SemaphoreType.DMA((2,))],
        compiler_params=_cparams(("arbitrary", "arbitrary")),
        name="ffn_down",
    )(dst_rows.reshape(N_EXPERTS, 1, R_EXP), hid, w_down, g_col)


def _combine_kernel(cs_ref, ce_ref, ys_hbm, x_ref, base_ref, basen_ref, g_ref, *rest):
    fw_ref = rest[0] if len(rest) == 5 else None
    o_ref, buf_ref, acc_ref, sem = rest[-4:]
    b = pl.program_id(0)
    t = pl.program_id(1)
    lo = cs_ref[b, t]
    hi = ce_ref[b, t]

    def chunk_copy(bb, c):
        return pltpu.make_async_copy(ys_hbm.at[pl.ds(bb * PAIRS + c * TT, TT)], buf_ref.at[c % 2], sem.at[c % 2])

    @pl.when((b == 0) & (t == 0) & (lo < hi))
    def _():
        chunk_copy(b, lo).start()

    acc_ref[...] = jnp.zeros((TT, D), F32)
    base = base_ref[0]
    basen = basen_ref[0]
    lane = lax.broadcasted_iota(I32, (1, TT), 1)

    def body(c, _):
        @pl.when(c + 1 < hi)
        def _():
            chunk_copy(b, c + 1).start()

        chunk_copy(b, c).wait()
        r = c * TT + lane
        onehot = jnp.where((base <= r) & (r < basen), 1.0, 0.0).astype(BF16)
        y = buf_ref[c % 2]
        y_hi = y.astype(BF16)
        y_lo = (y - y_hi.astype(F32)).astype(BF16)
        acc_ref[...] += (jnp.dot(onehot, y_hi, preferred_element_type=F32)
                         + jnp.dot(onehot, y_lo, preferred_element_type=F32))
        return 0

    lax.fori_loop(lo, hi, body, 0)

    wrap = t + 1 == N_TT
    nb = jnp.where(wrap, b + 1, b)
    nt = jnp.where(wrap, 0, t + 1)

    @pl.when(nb < BATCH)
    def _():
        nlo = cs_ref[nb, nt]

        @pl.when(nlo < ce_ref[nb, nt])
        def _():
            chunk_copy(nb, nlo).start()

    gate = jnp.where(t == 0, g_ref[0, 0, 0:1, :], g_ref[0, 0, 1:2, :])
    y = x_ref[0] + gate * acc_ref[...]
    o_ref[0] = y if fw_ref is None else _rms(y) * fw_ref[...]


def _combine(cs, ce, ys, x, base_col, basen_col, mt, final_w=None):
    in_specs = [
        pl.BlockSpec(memory_space=pl.ANY),
        pl.BlockSpec((1, TT, D), lambda b, t, *_: (b, t, 0)),
        pl.BlockSpec((1, TT, 1), lambda b, t, *_: (b, t, 0)),
        pl.BlockSpec((1, TT, 1), lambda b, t, *_: (b, t, 0)),
        pl.BlockSpec((1, 1, 2, D), lambda b, t, *_: (5, b, 0, 0)),
    ]
    args = [cs, ce, ys, x, base_col, basen_col, mt]
    if final_w is None:
        out_spec = pl.BlockSpec((1, TT, D), lambda b, t, *_: (b, t, 0))
        out_rows = S
    else:
        assert CTX == TT
        in_specs.append(pl.BlockSpec((1, D), lambda b, t, *_: (0, 0)))
        args.append(final_w.reshape(1, D))
        out_spec = pl.BlockSpec((1, TT, D), lambda b, t, *_: (b, jnp.maximum(t - 1, 0), 0))
        out_rows = SEQ
    grid_spec = pltpu.PrefetchScalarGridSpec(
        num_scalar_prefetch=2,
        grid=(BATCH, N_TT),
        in_specs=in_specs,
        out_specs=out_spec,
        scratch_shapes=[pltpu.VMEM((2, TT, D), F32), pltpu.VMEM((TT, D), F32), pltpu.SemaphoreType.DMA((2,))],
    )
    return pl.pallas_call(
        _combine_kernel,
        grid_spec=grid_spec,
        out_shape=jax.ShapeDtypeStruct((BATCH, out_rows, D), F32),
        compiler_params=_cparams(("arbitrary", "arbitrary")),
        name="moe_combine",
    )(*args)


def _moe(x, mt, norm_w, w_router, w_gate, w_up, w_down, li, final_w=None):
    logits = _router_logits(x, norm_w, mt, w_router)
    idx_l, g_l, dst_l, idx_c, g_c, dst_c, bases = _router_select(logits)
    boff = jnp.arange(BATCH, dtype=I32)[:, None, None]

    def rows(c, l, off):
        r = jnp.concatenate([c[:, :, :CAP_CTX], l], axis=2) + off
        return jnp.transpose(r, (1, 0, 2)).reshape(N_EXPERTS, R_EXP)

    src_rows = rows(idx_c, idx_l, boff * S)
    dst_rows = rows(dst_c, dst_l, boff * PAIRS)
    g_col = rows(g_c, g_l, 0.0).reshape(N_EXPERTS, R_EXP, 1)
    hid = _ffn_up(src_rows, x.reshape(BATCH * S, D), norm_w, mt, w_gate, w_up, li)
    ys = _ffn_down(dst_rows, hid, w_down, g_col, li)
    base, basen = bases[:, 0, :], bases[:, 1, :]
    cs = base[:, ::TT] // TT
    ce = (basen[:, TT - 1::TT] + TT - 1) // TT
    return _combine(cs, ce, ys, x, base.reshape(BATCH, S, 1), basen.reshape(BATCH, S, 1), mt, final_w)


def _pack_kernel(ctx_ref, x_ref, o_ref):
    @pl.when(pl.program_id(1) == 0)
    def _():
        o_ref[...] = ctx_ref[...]

    @pl.when(pl.program_id(1) > 0)
    def _():
        o_ref[...] = x_ref[...]


def _pack_rows(ctx, x):
    assert CTX == TT
    return pl.pallas_call(
        _pack_kernel,
        grid=(BATCH, N_TT),
        in_specs=[
            pl.BlockSpec((1, TT, D), lambda b, t: (b, 0, 0)),
            pl.BlockSpec((1, TT, D), lambda b, t: (b, jnp.maximum(t - 1, 0), 0)),
        ],
        out_specs=pl.BlockSpec((1, TT, D), lambda b, t: (b, t, 0)),
        out_shape=jax.ShapeDtypeStruct((BATCH, S, D), F32),
        compiler_params=_cparams(("arbitrary", "arbitrary")),
        name="pack_rows",
    )(ctx, x)


def kernel(x, c, ctx, c_ctx, ada_w, ada_b, norm_mix_w, norm_ffn_w,
           ev_w_in, ev_conv_w, ev_conv_b, ev_ra_w, ev_ra_b, ev_ix_w, ev_ix_b, ev_lambda, ev_sink, ev_w_out,
           od_w_in, od_conv_w, od_conv_b, od_gate_b, od_hnorm_w, od_w_out,
           moe_router, moe_w_gate, moe_w_up, moe_w_down, final_norm_w):
    assert x.shape == (BATCH, SEQ, D) and ctx.shape == (BATCH, CTX, D)
    cc = jnp.zeros((SUBLANE, D), F32).at[:BATCH].set(c).at[BATCH].set(c_ctx)
    mod = _modulation(cc, ada_w, ada_b)
    rope = _rope_tables()
    ev_w_in, ev_w_out, od_w_in, od_w_out = (w.astype(BF16) for w in (ev_w_in, ev_w_out, od_w_in, od_w_out))
    xs = _pack_rows(ctx, x)
    for layer in range(DEPTH):
        mt = _mod_table(mod[layer])
        i = layer // 2
        if layer % 2 == 0:
            xs = _even_mixer(xs, mt, norm_mix_w[layer], ev_w_in, i, ev_conv_w[i], ev_conv_b[i], ev_ra_w[i],
                             ev_ra_b[i], ev_ix_w[i], ev_ix_b[i], ev_lambda[i], ev_sink[i], ev_w_out, rope)
        else:
            xs = _odd_mixer(xs, mt, norm_mix_w[layer], od_w_in, i, od_conv_w[i], od_conv_b[i], od_gate_b[i],
                            od_hnorm_w[i], od_w_out)
        xs = _moe(xs, mt, norm_ffn_w[layer], moe_router[layer], moe_w_gate, moe_w_up, moe_w_down, layer,
                  final_norm_w if layer == DEPTH - 1 else None)
    return xs
```

```python
import functools

import jax
import jax.numpy as jnp
from jax import lax
from jax.experimental import pallas as pl
from jax.experimental.pallas import tpu as pltpu

F32 = jnp.float32
BF16 = jnp.bfloat16
I32 = jnp.int32

D = 2048
BATCH = 4
SEQ = 4096
CTX = 256
S = CTX + SEQ
DEPTH = 4
N_MOD = 6
EPS = 1e-6
GRID_W = 64

LRU_W = 1024
LRU_BLOCKS = 8
LRU_BLOCK = 128
LRU_C = 8.0
CONV_W = 4
ATT_HEADS = 8
KV_HEADS = 2
GROUP = ATT_HEADS // KV_HEADS
HEAD_DIM = 128
ATT_W = ATT_HEADS * HEAD_DIM
KV_W = KV_HEADS * HEAD_DIM
WINDOW = 128
ROPE_PAIRS = HEAD_DIM // 4
ROPE_BASE = 10000.0
EVEN_IN = 2 * LRU_W + ATT_W + 2 * KV_W

M_HEADS = 8
M_DK = 128
M_DV = 256
M_QK = M_HEADS * M_DK
M_V = M_HEADS * M_DV
ODD_MAIN = 2 * M_QK + 2 * M_V
N_GATES = 4 * M_HEADS

N_EXPERTS = 16
EC_FACTOR = 2
D_EXPERT = 1536
CAP_LAT = EC_FACTOR * SEQ // N_EXPERTS
CAP_CTX = EC_FACTOR * CTX // N_EXPERTS
RPS = CAP_CTX + CAP_LAT
R_EXP = BATCH * RPS
PAIRS = N_EXPERTS * RPS

LANE = 128
SUBLANE = 8
VMEM_LIMIT = 56 * 1024 * 1024
TM = S // 4
TILES_PER_SAMPLE = S // TM
TN = 512
TT = 256
N_TT = S // TT
CHUNK = 128
N_CHUNK = S // CHUNK
CTX_CHUNKS = CTX // CHUNK


def _cparams(sem, vmem=VMEM_LIMIT):
    return pltpu.CompilerParams(dimension_semantics=sem, vmem_limit_bytes=vmem)


def _sigmoid(x):
    return 1.0 / (1.0 + jnp.exp(-x))


def _silu(x):
    return x * _sigmoid(x)


def _softplus(x):
    return jnp.maximum(x, 0.0) + jnp.log1p(jnp.exp(-jnp.abs(x)))


def _log_sigmoid(x):
    return -_softplus(-x)


def _gelu_tanh(x):
    return 0.5 * x * (1.0 + jnp.tanh(0.7978845608028654 * (x + 0.044715 * (x * x * x))))


def _rms(x):
    return x * lax.rsqrt(jnp.mean(x * x, axis=-1, keepdims=True) + EPS)


def _ctx_select(is_ctx, tab):
    return jnp.where(is_ctx, tab[0:1, :], tab[1:2, :])


def _mod_kernel(c_ref, w_ref, b_ref, o_ref):
    a = _silu(c_ref[...]).astype(BF16)
    o_ref[0] = jnp.dot(a, w_ref[0].astype(BF16), preferred_element_type=F32) + b_ref[0]


def _modulation(cc, ada_w, ada_b):
    tn = 1024
    return pl.pallas_call(
        _mod_kernel,
        grid=(DEPTH, N_MOD * D // tn),
        in_specs=[
            pl.BlockSpec((SUBLANE, D), lambda l, j: (0, 0)),
            pl.BlockSpec((1, D, tn), lambda l, j: (l, 0, j)),
            pl.BlockSpec((1, 1, tn), lambda l, j: (l, 0, j)),
        ],
        out_specs=pl.BlockSpec((1, SUBLANE, tn), lambda l, j: (l, 0, j)),
        out_shape=jax.ShapeDtypeStruct((DEPTH, SUBLANE, N_MOD * D), F32),
        compiler_params=_cparams(("arbitrary", "arbitrary")),
        name="adaln_mod",
    )(cc, ada_w, ada_b.reshape(DEPTH, 1, N_MOD * D))


def _mod_table(mod_layer):
    m = mod_layer.reshape(SUBLANE, N_MOD, D)
    lat = jnp.transpose(m[:BATCH], (1, 0, 2))
    ctx = jnp.broadcast_to(m[BATCH][:, None, :], (N_MOD, BATCH, D))
    return jnp.stack([ctx, lat], axis=2)


def _norm_mod_rows(x, nw, sh_tab, sc_tab, row0):
    rows = x.shape[0]
    is_ctx = (row0 + lax.broadcasted_iota(I32, (rows, 1), 0)) < CTX
    y = _rms(x) * nw
    return y * (1.0 + _ctx_select(is_ctx, sc_tab)) + _ctx_select(is_ctx, sh_tab)


def _in_proj_kernel(x_ref, nw_ref, sh_ref, sc_ref, w_ref, *rest):
    i = pl.program_id(0)
    j = pl.program_id(1)
    o_ref, xn_ref = rest[-3 if len(rest) == 4 else 0], rest[-1]

    @pl.when(j == 0)
    def _():
        row0 = (i % TILES_PER_SAMPLE) * TM
        xn = _norm_mod_rows(x_ref[0], nw_ref[...], sh_ref[0, 0], sc_ref[0, 0], row0).astype(BF16)
        xn_ref[...] = xn
        if len(rest) == 4:
            rest[2][0] = jnp.dot(xn, rest[0][...].astype(BF16), preferred_element_type=F32)

    o_ref[0] = jnp.dot(xn_ref[...], w_ref[...].astype(BF16), preferred_element_type=F32).astype(o_ref.dtype)


def _in_proj(x, nw, mt, w, li, n_out, w_extra=None):
    tps = TILES_PER_SAMPLE
    tn = 2 * TN if n_out % (2 * TN) == 0 else TN
    row = lambda width, col: pl.BlockSpec((1, TM, width), lambda i, j: (i // tps, i % tps, col(j)))
    in_specs = [
        row(D, lambda j: 0),
        pl.BlockSpec((1, D), lambda i, j: (0, 0)),
        pl.BlockSpec((1, 1, 2, D), lambda i, j: (0, i // tps, 0, 0)),
        pl.BlockSpec((1, 1, 2, D), lambda i, j: (1, i // tps, 0, 0)),
        pl.BlockSpec((None, D, tn), lambda i, j: (li, 0, j)),
    ]
    out_specs = [row(tn, lambda j: j)]
    out_shape = [jax.ShapeDtypeStruct((BATCH, S, n_out), BF16)]
    args = [x, nw.reshape(1, D), mt, mt, w]
    if w_extra is not None:
        in_specs.append(pl.BlockSpec((D, LANE), lambda i, j: (0, 0)))
        out_specs.append(row(LANE, lambda j: 0))
        out_shape.append(jax.ShapeDtypeStruct((BATCH, S, LANE), F32))
        args.append(w_extra)
    res = pl.pallas_call(
        _in_proj_kernel,
        grid=(BATCH * tps, n_out // tn),
        in_specs=in_specs,
        out_specs=out_specs,
        out_shape=out_shape,
        scratch_shapes=[pltpu.VMEM((TM, D), BF16)],
        compiler_params=_cparams(("arbitrary", "arbitrary")),
        name="in_proj",
    )(*args)
    return res[0] if w_extra is None else res


def _residual_epilogue(i, x_ref, g_ref, acc, o_ref):
    tm = acc.shape[0]
    row0 = (i % (S // tm)) * tm
    is_ctx = (row0 + lax.broadcasted_iota(I32, (tm, 1), 0)) < CTX
    o_ref[0] = x_ref[0] + _ctx_select(is_ctx, g_ref[0, 0]) * acc


def _out_proj_even_kernel(a1_ref, a2_ref, x_ref, g_ref, w_ref, o_ref, a_ref):
    i = pl.program_id(0)
    j = pl.program_id(1)

    @pl.when(j == 0)
    def _():
        a_ref[:, :LRU_W] = a1_ref[0].astype(BF16)
        a_ref[:, LRU_W:] = a2_ref[0].astype(BF16)

    acc = jnp.dot(a_ref[...], w_ref[...].astype(BF16), preferred_element_type=F32)
    _residual_epilogue(i, x_ref, g_ref, acc, o_ref)


def _out_proj_odd_kernel(h_ref, og_ref, hw_ref, x_ref, g_ref, w_ref, o_ref, a_ref):
    i = pl.program_id(0)
    j = pl.program_id(1)

    @pl.when(j == 0)
    def _():
        for h in range(M_HEADS):
            sl = slice(h * M_DV, (h + 1) * M_DV)
            hn = _rms(h_ref[0, :, sl]) * hw_ref[:, sl] * _sigmoid(og_ref[0, :, sl].astype(F32))
            a_ref[:, sl] = hn.astype(BF16)

    acc = jnp.dot(a_ref[...], w_ref[...].astype(BF16), preferred_element_type=F32)
    _residual_epilogue(i, x_ref, g_ref, acc, o_ref)


def _row_spec(tm, width, col=None):
    tps = S // tm
    if col is None:
        return pl.BlockSpec((1, tm, width), lambda i, j: (i // tps, i % tps, j))
    return pl.BlockSpec((1, tm, width), lambda i, j: (i // tps, i % tps, col))


def _out_proj_even(a1, a2, x, mt, w, li):
    tm = TM
    return pl.pallas_call(
        _out_proj_even_kernel,
        grid=(BATCH * S // tm, D // TN),
        in_specs=[
            _row_spec(tm, LRU_W, 0),
            _row_spec(tm, ATT_W, 0),
            _row_spec(tm, TN),
            pl.BlockSpec((1, 1, 2, TN), lambda i, j: (2, i // (S // tm), 0, j)),
            pl.BlockSpec((None, D, TN), lambda i, j: (li, 0, j)),
        ],
        out_specs=_row_spec(tm, TN),
        out_shape=jax.ShapeDtypeStruct((BATCH, S, D), F32),
        scratch_shapes=[pltpu.VMEM((tm, D), BF16)],
        compiler_params=_cparams(("arbitrary", "arbitrary")),
        name="out_proj_even",
    )(a1, a2, x, mt, w)


def _out_proj_odd(h, p_odd, hnorm_w, x, mt, w, li):
    tm = TM // 2
    return pl.pallas_call(
        _out_proj_odd_kernel,
        grid=(BATCH * S // tm, D // TN),
        in_specs=[
            _row_spec(tm, M_V, 0),
            _row_spec(tm, M_V, (2 * M_QK + M_V) // M_V),
            pl.BlockSpec((1, M_V), lambda i, j: (0, 0)),
            _row_spec(tm, TN),
            pl.BlockSpec((1, 1, 2, TN), lambda i, j: (2, i // (S // tm), 0, j)),
            pl.BlockSpec((None, D, TN), lambda i, j: (li, 0, j)),
        ],
        out_specs=_row_spec(tm, TN),
        out_shape=jax.ShapeDtypeStruct((BATCH, S, D), F32),
        scratch_shapes=[pltpu.VMEM((tm, D), BF16)],
        compiler_params=_cparams(("arbitrary", "arbitrary")),
        name="out_proj_odd",
    )(h, p_odd, hnorm_w.reshape(1, M_V), x, mt, w)


def _seg_conv(x, cw, cb):
    x = x.astype(F32)
    n = x.shape[0]
    row = lax.broadcasted_iota(I32, (n, 1), 0)
    seg = row < CTX
    y = cb
    for j in range(CONV_W):
        off = j - CONV_W // 2
        if off == 0:
            tap = x
        else:
            src = row + off
            ok = (src >= 0) & (src < n) & ((src < CTX) == seg)
            tap = jnp.where(ok, pltpu.roll(x, (-off) % n, axis=0), 0.0)
        y = y + tap * cw[j:j + 1, :]
    return y


def _bwd_chunk(j):
    return jnp.where(j < CTX_CHUNKS, CTX_CHUNKS - 1 - j, N_CHUNK + CTX_CHUNKS - 1 - j)


def _lin_scan(a, b, h_in, reverse):
    t_len, c = a.shape
    ng = t_len // SUBLANE
    a = a.reshape(ng, SUBLANE, c)
    b = b.reshape(ng, SUBLANE, c)
    sub = lax.broadcasted_iota(I32, a.shape, 1)
    k = 1
    while k < SUBLANE:
        shift = SUBLANE - k if reverse else k
        ok = (sub < SUBLANE - k) if reverse else (sub >= k)
        a_s = pltpu.roll(a, shift, axis=1)
        b_s = pltpu.roll(b, shift, axis=1)
        b = jnp.where(ok, a * b_s + b, b)
        a = jnp.where(ok, a * a_s, a)
        k *= 2
    edge = 0 if reverse else SUBLANE - 1
    hs = [None] * ng
    for g in (range(ng - 1, -1, -1) if reverse else range(ng)):
        hs[g] = b[g] + a[g] * h_in
        h_in = hs[g][edge:edge + 1, :]
    return jnp.concatenate(hs, axis=0)


LRU_UNROLL = 2


def _lru_kernel(xa_ref, ya_ref, cw_ref, cb_ref, raw_ref, rab_ref, ixw_ref, ixb_ref, lam_ref, o_ref, xc_ref, hf_ref):
    xc_ref[...] = _seg_conv(xa_ref[0], cw_ref[...], cb_ref[...])

    def gates(x, d):
        xb = x.astype(BF16)
        r = _sigmoid(jnp.dot(xb, raw_ref[d, 0].astype(BF16), preferred_element_type=F32) + rab_ref[d, 0])
        i = _sigmoid(jnp.dot(xb, ixw_ref[d, 0].astype(BF16), preferred_element_type=F32) + ixb_ref[d, 0])
        log_a = (-LRU_C * r) * _softplus(-lam_ref[d, 0])
        a = jnp.exp(log_a)
        return a, jnp.sqrt(-jnp.tanh(log_a) * (a * a + 1.0)) * (i * x)

    hf_ref[...] = jnp.zeros((S, LRU_BLOCK), F32)

    def one(chunk, h, d):
        rows = pl.ds(pl.multiple_of(chunk * CHUNK, CHUNK), CHUNK)
        a, b = gates(xc_ref[rows, :], d)
        h_all = _lin_scan(a, b, h, d == 1)
        hf_ref[rows, :] += h_all
        return h_all[0:1, :] if d == 1 else h_all[CHUNK - 1:CHUNK, :]

    def body(i, carry):
        hf, hb = carry
        for u in range(LRU_UNROLL):
            j = i * LRU_UNROLL + u
            hf, hb = one(j, hf, 0), one(_bwd_chunk(j), hb, 1)
        return hf, hb

    zero = jnp.zeros((1, LRU_BLOCK), F32)
    lax.fori_loop(0, N_CHUNK // LRU_UNROLL, body, (zero, zero))
    o_ref[0] = (hf_ref[...] * _gelu_tanh(ya_ref[0].astype(F32))).astype(o_ref.dtype)


def _lru(p_even, conv_w, conv_b, ra_w, ra_b, ix_w, ix_b, lam):
    nb = LRU_BLOCKS
    blk4 = lambda: pl.BlockSpec((2, 1, LRU_BLOCK, LRU_BLOCK), lambda b, k: (0, k, 0, 0))
    vec4 = lambda: pl.BlockSpec((2, 1, 1, LRU_BLOCK), lambda b, k: (0, k, 0, 0))
    return pl.pallas_call(
        _lru_kernel,
        grid=(BATCH, nb),
        in_specs=[
            pl.BlockSpec((1, S, LRU_BLOCK), lambda b, k: (b, 0, k)),
            pl.BlockSpec((1, S, LRU_BLOCK), lambda b, k: (b, 0, nb + k)),
            pl.BlockSpec((CONV_W, LRU_BLOCK), lambda b, k: (0, k)),
            pl.BlockSpec((1, LRU_BLOCK), lambda b, k: (0, k)),
            blk4(), vec4(), blk4(), vec4(), vec4(),
        ],
        out_specs=pl.BlockSpec((1, S, LRU_BLOCK), lambda b, k: (b, 0, k)),
        out_shape=jax.ShapeDtypeStruct((BATCH, S, LRU_W), BF16),
        scratch_shapes=[pltpu.VMEM((S, LRU_BLOCK), F32), pltpu.VMEM((S, LRU_BLOCK), F32)],
        compiler_params=_cparams(("arbitrary", "arbitrary")),
        name="rglru",
    )(p_even, p_even, conv_w, conv_b.reshape(1, LRU_W), ra_w, ra_b.reshape(2, nb, 1, LRU_BLOCK),
      ix_w, ix_b.reshape(2, nb, 1, LRU_BLOCK), lam.reshape(2, nb, 1, LRU_BLOCK))


def _rope(x, cos, sin):
    x = x.astype(F32)
    lane = lax.broadcasted_iota(I32, (1, HEAD_DIM), 1)
    first = (lane % (2 * ROPE_PAIRS)) < ROPE_PAIRS
    swapped = jnp.where(first, pltpu.roll(x, HEAD_DIM - ROPE_PAIRS, axis=1), pltpu.roll(x, ROPE_PAIRS, axis=1))
    return x * cos + swapped * sin


def _attn_kernel(sink_ref, q_ref, kp_ref, ko_ref, kn_ref, vp_ref, vo_ref, vn_ref, ck_ref, cv_ref,
                 cq_ref, sq_ref, cp_ref, sp_ref, cn_ref, sn_ref, o_ref):
    t = pl.program_id(1)
    nq = CHUNK
    lat = t >= CTX_CHUNKS
    c_lo = jnp.where(lat, jnp.where(t > CTX_CHUNKS, 0, nq), 0)
    c_hi = jnp.where(lat, jnp.where(t < N_CHUNK - 1, 3 * nq, 2 * nq), 0)
    r = lax.broadcasted_iota(I32, (nq, CTX + 3 * nq), 0)
    c = lax.broadcasted_iota(I32, (nq, CTX + 3 * nq), 1) - CTX
    band_ok = (jnp.abs(c - nq - r) <= WINDOW) & (c >= c_lo) & (c < c_hi)
    bias = jnp.where((c < 0) | band_ok, 0.0, -jnp.inf)
    bias = jnp.concatenate([bias] * GROUP, axis=0)
    scale = HEAD_DIM ** -0.5
    for g in range(KV_HEADS):
        ks = slice(g * HEAD_DIM, (g + 1) * HEAD_DIM)
        keys = jnp.concatenate([
            ck_ref[0, :, ks].astype(F32),
            _rope(kp_ref[0, :, ks], cp_ref[...], sp_ref[...]),
            _rope(ko_ref[0, :, ks], cq_ref[...], sq_ref[...]),
            _rope(kn_ref[0, :, ks], cn_ref[...], sn_ref[...]),
        ], axis=0).astype(BF16)
        vals = jnp.concatenate([cv_ref[0, :, ks], vp_ref[0, :, ks], vo_ref[0, :, ks], vn_ref[0, :, ks]],
                               axis=0).astype(BF16)
        heads = [g * GROUP + hh for hh in range(GROUP)]
        qg = jnp.concatenate([_rope(q_ref[0, :, h * HEAD_DIM:(h + 1) * HEAD_DIM], cq_ref[...], sq_ref[...])
                              for h in heads], axis=0).astype(BF16)
        sink = jnp.concatenate([jnp.full((nq, 1), sink_ref[h], F32) for h in heads], axis=0)
        s = lax.dot_general(qg, keys, (((1,), (1,)), ((), ())), preferred_element_type=F32) * scale + bias
        m = jnp.maximum(jnp.max(s, axis=-1, keepdims=True), sink)
        p = jnp.exp(s - m)
        den = jnp.sum(p, axis=-1, keepdims=True) + jnp.exp(sink - m)
        out = jnp.dot((p / den).astype(BF16), vals, preferred_element_type=F32)
        for hh, h in enumerate(heads):
            o_ref[0, :, h * HEAD_DIM:(h + 1) * HEAD_DIM] = out[hh * nq:(hh + 1) * nq].astype(o_ref.dtype)


def _attention(p_even, sink, cos_t, sin_t):
    qc = 2 * LRU_W // ATT_W
    kc = (2 * LRU_W + ATT_W) // KV_W
    vc = kc + 1
    lo, hi = CTX_CHUNKS, N_CHUNK - 1
    prev = lambda t: jnp.clip(t - 1, lo, hi)
    nxt = lambda t: jnp.clip(t + 1, lo, hi)
    own = lambda t: t
    kv = lambda col, f: pl.BlockSpec((1, CHUNK, KV_W), lambda b, t: (b, f(t), col))
    tab = lambda f: pl.BlockSpec((CHUNK, HEAD_DIM), lambda b, t: (f(t), 0))
    return pl.pallas_call(
        _attn_kernel,
        grid=(BATCH, N_CHUNK),
        in_specs=[
            pl.BlockSpec(memory_space=pltpu.SMEM),
            pl.BlockSpec((1, CHUNK, ATT_W), lambda b, t: (b, t, qc)),
            kv(kc, prev), kv(kc, own), kv(kc, nxt),
            kv(vc, prev), kv(vc, own), kv(vc, nxt),
            pl.BlockSpec((1, CTX, KV_W), lambda b, t: (b, 0, kc)),
            pl.BlockSpec((1, CTX, KV_W), lambda b, t: (b, 0, vc)),
            tab(own), tab(own), tab(prev), tab(prev), tab(nxt), tab(nxt),
        ],
        out_specs=pl.BlockSpec((1, CHUNK, ATT_W), lambda b, t: (b, t, 0)),
        out_shape=jax.ShapeDtypeStruct((BATCH, S, ATT_W), BF16),
        compiler_params=_cparams(("arbitrary", "arbitrary")),
        name="window_attention",
    )(sink, p_even, p_even, p_even, p_even, p_even, p_even, p_even, p_even, p_even,
      cos_t, sin_t, cos_t, sin_t, cos_t, sin_t)


def _rope_tables():
    inv = jnp.power(ROPE_BASE, -jnp.arange(ROPE_PAIRS, dtype=F32) / ROPE_PAIRS)
    pos = jnp.arange(SEQ)
    row_ang = (pos // GRID_W).astype(F32)[:, None] * inv
    col_ang = (pos % GRID_W).astype(F32)[:, None] * inv
    cos = jnp.concatenate([jnp.cos(row_ang)] * 2 + [jnp.cos(col_ang)] * 2, axis=-1)
    sin = jnp.concatenate([-jnp.sin(row_ang), jnp.sin(row_ang), -jnp.sin(col_ang), jnp.sin(col_ang)], axis=-1)
    cos = jnp.concatenate([jnp.ones((CTX, HEAD_DIM), F32), cos], axis=0)
    sin = jnp.concatenate([jnp.zeros((CTX, HEAD_DIM), F32), sin], axis=0)
    return cos, sin


def _mlstm_local(qb, kb, k, vb, li_row, b_row, reverse):
    ln = qb.shape[0]
    row = lax.broadcasted_iota(I32, (ln, ln), 0)
    col = lax.broadcasted_iota(I32, (ln, ln), 1)
    b_r = jnp.broadcast_to(b_row, (ln, ln))
    li_r = jnp.broadcast_to(li_row, (ln, ln))
    b = b_r.T
    li = li_r.T
    causal = (col >= row) if reverse else (col <= row)
    logw = jnp.where(causal, b - b_r + li_r, -jnp.inf)
    m_loc = jnp.max(logw, axis=1, keepdims=True)
    s = lax.dot_general(qb, kb, (((1,), (1,)), ((), ())), preferred_element_type=F32) * jnp.exp(logw - m_loc)
    num = jnp.dot(s.astype(BF16), vb, preferred_element_type=F32)
    den = jnp.sum(s, axis=1, keepdims=True)
    edge = 0 if reverse else ln - 1
    b_last = b[edge:edge + 1, :]
    log_u = b_last - b + li
    mu_loc = jnp.max(log_u, axis=0, keepdims=True)
    uk = jnp.exp(log_u - mu_loc) * k
    kv = lax.dot_general(uk.astype(BF16), vb, (((0,), (0,)), ((), ())), preferred_element_type=F32)
    return dict(qb=qb, b=b, m_loc=m_loc, num=num, den=den, b_last=b_last, mu_loc=mu_loc, kv=kv,
                ksum=jnp.sum(uk, axis=0, keepdims=True))


def _mlstm_apply(loc, state):
    c0, n0, m0 = state
    b, m_loc = loc["b"], loc["m_loc"]
    m = jnp.maximum(b + m0, m_loc)
    intra = jnp.exp(m_loc - m)
    inter = jnp.exp(b + m0 - m)
    qc = jnp.dot(loc["qb"], c0.astype(BF16), preferred_element_type=F32)
    n8 = jnp.broadcast_to(n0, (SUBLANE, n0.shape[1])).astype(BF16)
    qn = lax.dot_general(loc["qb"], n8, (((1,), (1,)), ((), ())), preferred_element_type=F32)[:, :1]
    num = jnp.concatenate([intra, intra], axis=1) * loc["num"] + jnp.concatenate([inter, inter], axis=1) * qc
    den = intra[:, :1] * loc["den"] + inter[:, :1] * qn
    h = num / jnp.maximum(jnp.abs(den), jnp.exp(-m[:, :1]))
    m_new = jnp.maximum(loc["b_last"] + m0, loc["mu_loc"])
    decay = jnp.exp(loc["b_last"] + m0 - m_new)
    grow = jnp.exp(loc["mu_loc"] - m_new)
    c_new = decay[:, :1] * c0 + grow[:, :1] * loc["kv"]
    n_new = decay * n0 + grow * loc["ksum"]
    return h, (c_new, n_new, m_new)


MLSTM_UNROLL = 2


def _mlstm_kernel(gb_ref, q_ref, k_ref, v_ref, g_ref, cwq_ref, cbq_ref, cwk_ref, cbk_ref, o_ref,
                  qb_ref, kb_ref, kc_ref, vb_ref, gr_ref):
    hd = pl.program_id(1)
    ln = CHUNK
    qb_ref[...] = (_silu(_seg_conv(q_ref[0], cwq_ref[...], cbq_ref[...])) * (M_DK ** -0.5)).astype(BF16)
    kc = _silu(_seg_conv(k_ref[0], cwk_ref[...], cbk_ref[...]))
    kc_ref[...] = kc
    kb_ref[...] = kc.astype(BF16)
    vb_ref[...] = v_ref[0].astype(BF16)
    o_ref[0] = jnp.zeros((S, M_DV), F32)

    def rows_of(chunk):
        return pl.ds(pl.multiple_of(chunk * ln, ln), ln)

    sub = lax.broadcasted_iota(I32, (N_GATES, 1), 0)

    def gate_rows(i, _):
        for u in range(MLSTM_UNROLL):
            c = i * MLSTM_UNROLL + u
            gt = g_ref[0, rows_of(c), :].T[:N_GATES]
            pick = lambda ty: jnp.sum(jnp.where(sub == ty * M_HEADS + hd, gt, 0.0), axis=0, keepdims=True) \
                + gb_ref[ty, hd]
            gr_ref[c] = jnp.concatenate([pick(0), _log_sigmoid(pick(1)), pick(2), _log_sigmoid(pick(3)),
                                         jnp.zeros((SUBLANE - 4, ln), F32)], axis=0)
        return 0

    lax.fori_loop(0, N_CHUNK // MLSTM_UNROLL, gate_rows, 0)
    gr = gr_ref[...]
    lane = lax.broadcasted_iota(I32, gr.shape, 2)
    kind = lax.broadcasted_iota(I32, gr.shape, 1)
    pre = suf = gr
    kk = 1
    while kk < ln:
        pre = pre + jnp.where(lane >= kk, pltpu.roll(pre, kk, axis=2), 0.0)
        suf = suf + jnp.where(lane < ln - kk, pltpu.roll(suf, ln - kk, axis=2), 0.0)
        kk *= 2
    gr_ref[...] = jnp.where(kind == 1, pre, jnp.where(kind == 3, suf, gr))

    def local(chunk, d):
        rows = rows_of(chunk)
        gr = gr_ref[chunk]
        return _mlstm_local(qb_ref[rows, :], kb_ref[rows, :], kc_ref[rows, :], vb_ref[rows, :],
                            gr[2 * d:2 * d + 1], gr[2 * d + 1:2 * d + 2], d == 1)

    def body(i, carry):
        steps = [i * MLSTM_UNROLL + u for u in range(MLSTM_UNROLL)]
        chunks = ([j for j in steps], [_bwd_chunk(j) for j in steps])
        locs = [[local(c, d) for c in chunks[d]] for d in range(2)]
        states = list(carry)
        for u in range(MLSTM_UNROLL):
            for d in range(2):
                h, states[d] = _mlstm_apply(locs[d][u], states[d])
                o_ref[0, rows_of(chunks[d][u]), :] += h
        return tuple(states)

    zero = (jnp.zeros((M_DK, M_DV), F32), jnp.zeros((1, M_DK), F32), jnp.zeros((1, ln), F32))
    lax.fori_loop(0, N_CHUNK // MLSTM_UNROLL, body, (zero, zero))


def _mlstm(p_odd, gates, conv_w, conv_b, gate_b):
    nh = M_HEADS
    cw = lambda off: pl.BlockSpec((CONV_W, M_DK), lambda b, h: (0, off + h))
    cb = lambda off: pl.BlockSpec((1, M_DK), lambda b, h: (0, off + h))
    conv_b = conv_b.reshape(1, 2 * M_QK)
    assert N_CHUNK % MLSTM_UNROLL == 0 and CHUNK == M_DK
    return pl.pallas_call(
        _mlstm_kernel,
        grid=(BATCH, nh),
        in_specs=[
            pl.BlockSpec(memory_space=pltpu.SMEM),
            pl.BlockSpec((1, S, M_DK), lambda b, h: (b, 0, h)),
            pl.BlockSpec((1, S, M_DK), lambda b, h: (b, 0, nh + h)),
            pl.BlockSpec((1, S, M_DV), lambda b, h: (b, 0, 2 * M_QK // M_DV + h)),
            pl.BlockSpec((1, S, LANE), lambda b, h: (b, 0, 0)),
            cw(0), cb(0), cw(nh), cb(nh),
        ],
        out_specs=pl.BlockSpec((1, S, M_DV), lambda b, h: (b, 0, h)),
        out_shape=jax.ShapeDtypeStruct((BATCH, S, M_V), F32),
        scratch_shapes=[pltpu.VMEM((S, M_DK), BF16), pltpu.VMEM((S, M_DK), BF16), pltpu.VMEM((S, M_DK), F32),
                        pltpu.VMEM((S, M_DV), BF16), pltpu.VMEM((N_CHUNK, SUBLANE, CHUNK), F32)],
        compiler_params=_cparams(("arbitrary", "arbitrary")),
        name="mlstm",
    )(gate_b, p_odd, p_odd, p_odd, gates, conv_w, conv_b, conv_w, conv_b)


def _even_mixer(x, mt, norm_w, w_in, li, conv_w, conv_b, ra_w, ra_b, ix_w, ix_b, lam, sink, w_out, rope):
    p = _in_proj(x, norm_w, mt, w_in, li, EVEN_IN)
    a = _lru(p, conv_w, conv_b, ra_w, ra_b, ix_w, ix_b, lam)
    b = _attention(p, sink, rope[0], rope[1])
    return _out_proj_even(a, b, x, mt, w_out, li)


def _odd_mixer(x, mt, norm_w, w_in, li, conv_w, conv_b, gate_b, hnorm_w, w_out):
    wg = jnp.pad(w_in[li, :, ODD_MAIN:], ((0, 0), (0, LANE - N_GATES)))
    p, gates = _in_proj(x, norm_w, mt, w_in, li, ODD_MAIN, w_extra=wg)
    h = _mlstm(p, gates, conv_w, conv_b, gate_b)
    return _out_proj_odd(h, p, hnorm_w, x, mt, w_out, li)


def _router_logits_kernel(x_ref, nw_ref, sh_ref, sc_ref, w_ref, o_ref):
    t = pl.program_id(1)
    h = _norm_mod_rows(x_ref[0], nw_ref[...], sh_ref[0, 0], sc_ref[0, 0], t * TT)

    def split(a):
        hi = a.astype(BF16)
        return hi, (a - hi.astype(F32)).astype(BF16)

    nt = lambda a, b: lax.dot_general(a, b, (((1,), (1,)), ((), ())), preferred_element_type=F32)
    (w_hi, w_lo), (h_hi, h_lo) = split(w_ref[...]), split(h)
    o_ref[0] = nt(w_hi, h_hi) + (nt(w_hi, h_lo) + nt(w_lo, h_hi))


def _router_logits(x, nw, mt, w_router):
    return pl.pallas_call(
        _router_logits_kernel,
        grid=(BATCH, N_TT),
        in_specs=[
            pl.BlockSpec((1, TT, D), lambda b, t: (b, t, 0)),
            pl.BlockSpec((1, D), lambda b, t: (0, 0)),
            pl.BlockSpec((1, 1, 2, D), lambda b, t: (3, b, 0, 0)),
            pl.BlockSpec((1, 1, 2, D), lambda b, t: (4, b, 0, 0)),
            pl.BlockSpec((N_EXPERTS, D), lambda b, t: (0, 0)),
        ],
        out_specs=pl.BlockSpec((1, N_EXPERTS, TT), lambda b, t: (b, 0, t)),
        out_shape=jax.ShapeDtypeStruct((BATCH, N_EXPERTS, S), F32),
        compiler_params=_cparams(("arbitrary", "arbitrary")),
        name="router_logits",
    )(x, nw.reshape(1, D), mt, mt, w_router.T)


def _cumsum_lanes(x):
    n = x.shape[1]
    lane = lax.broadcasted_iota(I32, x.shape, 1)
    k = 1
    while k < n:
        x = x + jnp.where(lane >= k, pltpu.roll(x, k, axis=1), 0.0)
        k *= 2
    return x


def _cumsum_rows_excl(x):
    n = x.shape[0]
    row = lax.broadcasted_iota(I32, x.shape, 0)
    inc = x
    k = 1
    while k < n:
        inc = inc + jnp.where(row >= k, pltpu.roll(inc, k, axis=0), 0.0)
        k *= 2
    return inc - x


def _split3(x):
    hi = x.astype(BF16).astype(F32)
    r = x - hi
    mid = r.astype(BF16).astype(F32)
    return hi, mid, (r - mid).astype(BF16).astype(F32)


def _router_select_kernel(lg_ref, idxl_ref, gl_ref, dstl_ref, idxc_ref, gc_ref, dstc_ref, base_ref, vt_ref):
    ne = N_EXPERTS
    lg = lg_ref[0]
    ex = jnp.exp(lg - jnp.max(lg, axis=0, keepdims=True))
    aff = ex / jnp.sum(ex, axis=0, keepdims=True)
    bits = pltpu.bitcast(aff, I32)
    lane = lax.broadcasted_iota(I32, (ne, S), 1)
    sel = jnp.zeros((ne, S), F32)
    pos = jnp.zeros((ne, S), F32)
    for lo, hi, cap in ((0, CTX, CAP_CTX), (CTX, S, CAP_LAT)):
        vb = jnp.where((lane >= lo) & (lane < hi), bits, -1)
        thr = jnp.zeros((ne, 1), I32)
        for bit in range(30, -1, -1):
            cand = thr | (1 << bit)
            cnt = jnp.sum(jnp.where(vb >= cand, 1.0, 0.0), axis=1, keepdims=True)
            thr = jnp.where(cnt >= cap, cand, thr)
        gt = vb > thr
        eq = jnp.where(vb == thr, 1.0, 0.0)
        need = cap - jnp.sum(jnp.where(gt, 1.0, 0.0), axis=1, keepdims=True)
        eq_rank = _cumsum_lanes(eq) - eq
        s_seg = jnp.where(gt | ((eq > 0.0) & (eq_rank < need)), 1.0, 0.0)
        sel = sel + s_seg
        pos = pos + s_seg * (_cumsum_lanes(s_seg) - s_seg)
    cnt_tok = jnp.broadcast_to(jnp.sum(sel, axis=0, keepdims=True), (ne, S))
    base = _cumsum_lanes(cnt_tok) - cnt_tok
    dest = base + _cumsum_rows_excl(sel)
    base_ref[0] = jnp.concatenate([base[0:1], base[0:1] + cnt_tok[0:1], jnp.zeros((SUBLANE - 2, S), F32)],
                                  axis=0).astype(I32)
    posm = jnp.where(sel > 0.0, pos, -1.0)
    lane_f = lane[0:1].astype(F32)
    idx_hi = jnp.floor(lane_f * (1.0 / 64.0))
    idx_lo = lane_f - 64.0 * idx_hi
    dst_hi = jnp.floor(dest * (1.0 / 128.0))
    dst_lo = dest - 128.0 * dst_hi
    g_hi, g_mid, g_lo = _split3(aff)
    for e in range(ne):
        vt_ref[e] = jnp.concatenate([idx_hi, idx_lo, dst_hi[e:e + 1], dst_lo[e:e + 1], g_hi[e:e + 1],
                                     g_mid[e:e + 1], g_lo[e:e + 1], posm[e:e + 1]], axis=0)
    slot = lax.broadcasted_iota(I32, (LANE, 1), 0).astype(F32)

    def compact(vt, prow, s0):
        onehot = jnp.where(prow == slot + s0, 1.0, 0.0).astype(BF16)
        res = lax.dot_general(vt, onehot, (((1,), (1,)), ((), ())), preferred_element_type=F32)
        return (res[0:1] * 64.0 + res[1:2]).astype(I32), res[4:5] + res[5:6] + res[6:7], \
            (res[2:3] * 128.0 + res[3:4]).astype(I32)

    def per_expert(e, _):
        blk = vt_ref[e]
        prow = blk[SUBLANE - 1:SUBLANE]
        vt = blk.astype(BF16)
        idxc_ref[0, e], gc_ref[0, e], dstc_ref[0, e] = compact(vt[:, :CTX], prow[:, :CTX], 0.0)
        for sc in range(CAP_LAT // LANE):
            cs = slice(sc * LANE, (sc + 1) * LANE)
            idxl_ref[0, e, :, cs], gl_ref[0, e, :, cs], dstl_ref[0, e, :, cs] = compact(
                vt[:, CTX:], prow[:, CTX:], float(sc * LANE))
        return 0

    lax.fori_loop(0, ne, per_expert, 0)


def _router_select(logits):
    ne = N_EXPERTS
    out = lambda n, dt: jax.ShapeDtypeStruct((BATCH, ne, 1, n), dt)
    ospec = lambda n: pl.BlockSpec((1, ne, 1, n), lambda b: (b, 0, 0, 0))
    res = pl.pallas_call(
        _router_select_kernel,
        grid=(BATCH,),
        in_specs=[pl.BlockSpec((1, ne, S), lambda b: (b, 0, 0))],
        out_specs=[ospec(CAP_LAT), ospec(CAP_LAT), ospec(CAP_LAT), ospec(LANE), ospec(LANE), ospec(LANE),
                   pl.BlockSpec((1, SUBLANE, S), lambda b: (b, 0, 0))],
        out_shape=[out(CAP_LAT, I32), out(CAP_LAT, F32), out(CAP_LAT, I32), out(LANE, I32), out(LANE, F32),
                   out(LANE, I32), jax.ShapeDtypeStruct((BATCH, SUBLANE, S), I32)],
        scratch_shapes=[pltpu.VMEM((ne, SUBLANE, S), F32)],
        compiler_params=_cparams(("arbitrary",)),
        name="router_select",
    )(logits)
    return [r.reshape(BATCH, ne, r.shape[-1]) for r in res[:6]] + [res[6]]


ROW_UNROLL = 16


def _start_rows(n_rows, start_one):
    def body(i, _):
        for u in range(ROW_UNROLL):
            start_one(i * ROW_UNROLL + u)
        return 0

    lax.fori_loop(0, n_rows // ROW_UNROLL, body, 0)


def _ffn_up_kernel(src_ref, x_hbm, nw_ref, sh_ref, sc_ref, wg_ref, wu_ref, o_ref, xs_ref, gbuf_ref, sem):
    j = pl.program_id(1)

    @pl.when(j == 0)
    def _():
        def gather(b):
            slot = b % 2
            _start_rows(RPS, lambda r: pltpu.make_async_copy(
                x_hbm.at[pl.ds(src_ref[0, 0, b * RPS + r], 1)], gbuf_ref.at[slot, pl.ds(r, 1)],
                sem.at[slot]).start())

        gather(0)
        for b in range(BATCH):
            slot = b % 2
            if b + 1 < BATCH:
                gather(b + 1)
            pltpu.make_async_copy(x_hbm.at[pl.ds(0, RPS)], gbuf_ref.at[slot], sem.at[slot]).wait()
            is_ctx = lax.broadcasted_iota(I32, (RPS, 1), 0) < CAP_CTX
            y = _rms(gbuf_ref[slot]) * nw_ref[...]
            y = y * (1.0 + _ctx_select(is_ctx, sc_ref[0, b])) + _ctx_select(is_ctx, sh_ref[0, b])
            xs_ref[b * RPS:(b + 1) * RPS, :] = y.astype(BF16)

    xs = xs_ref[...]
    hg = jnp.dot(xs, wg_ref[0].astype(BF16), preferred_element_type=F32)
    hu = jnp.dot(xs, wu_ref[0].astype(BF16), preferred_element_type=F32)
    o_ref[0] = (_silu(hg) * hu).astype(BF16)


def _ffn_up(src_rows, x2d, nw, mt, w_gate, w_up, li):
    tf = 256
    assert RPS % ROW_UNROLL == 0
    return pl.pallas_call(
        _ffn_up_kernel,
        grid=(N_EXPERTS, D_EXPERT // tf),
        in_specs=[
            pl.BlockSpec((1, 1, R_EXP), lambda e, j: (e, 0, 0), memory_space=pltpu.SMEM),
            pl.BlockSpec(memory_space=pl.ANY),
            pl.BlockSpec((1, D), lambda e, j: (0, 0)),
            pl.BlockSpec((1, BATCH, 2, D), lambda e, j: (3, 0, 0, 0)),
            pl.BlockSpec((1, BATCH, 2, D), lambda e, j: (4, 0, 0, 0)),
            pl.BlockSpec((None, 1, D, tf), lambda e, j: (li, e, 0, j)),
            pl.BlockSpec((None, 1, D, tf), lambda e, j: (li, e, 0, j)),
        ],
        out_specs=pl.BlockSpec((1, R_EXP, tf), lambda e, j: (e, 0, j)),
        out_shape=jax.ShapeDtypeStruct((N_EXPERTS, R_EXP, D_EXPERT), BF16),
        scratch_shapes=[pltpu.VMEM((R_EXP, D), BF16), pltpu.VMEM((2, RPS, D), F32),
                        pltpu.SemaphoreType.DMA((2,))],
        compiler_params=_cparams(("arbitrary", "arbitrary")),
        name="ffn_up",
    )(src_rows.reshape(N_EXPERTS, 1, R_EXP), x2d, nw.reshape(1, D), mt, mt, w_gate, w_up)


def _ffn_down_kernel(dst_ref, hid_ref, w_ref, g_ref, y_hbm, ybuf_ref, sem):
    i = pl.program_id(1)
    step = pl.program_id(0) * BATCH + i
    slot = i % 2
    acc = jnp.dot(hid_ref[0], w_ref[0].astype(BF16), preferred_element_type=F32)
    ybuf_ref[slot] = acc * g_ref[0]

    def drain(s):
        pltpu.make_async_copy(ybuf_ref.at[s], y_hbm.at[pl.ds(0, RPS)], sem.at[s]).wait()

    @pl.when(step > 0)
    def _():
        drain(1 - slot)

    _start_rows(RPS, lambda r: pltpu.make_async_copy(
        ybuf_ref.at[slot, pl.ds(r, 1)], y_hbm.at[pl.ds(dst_ref[0, 0, i * RPS + r], 1)], sem.at[slot]).start())

    @pl.when(step == N_EXPERTS * BATCH - 1)
    def _():
        drain(slot)


def _ffn_down(dst_rows, hid, w_down, g_col, li):
    assert BATCH % 2 == 0
    return pl.pallas_call(
        _ffn_down_kernel,
        grid=(N_EXPERTS, BATCH),
        in_specs=[
            pl.BlockSpec((1, 1, R_EXP), lambda e, i: (e, 0, 0), memory_space=pltpu.SMEM),
            pl.BlockSpec((1, RPS, D_EXPERT), lambda e, i: (e, i, 0)),
            pl.BlockSpec((None, 1, D_EXPERT, D), lambda e, i: (li, e, 0, 0)),
            pl.BlockSpec((1, RPS, 1), lambda e, i: (e, i, 0)),
        ],
        out_specs=pl.BlockSpec(memory_space=pl.ANY),
        out_shape=jax.ShapeDtypeStruct((BATCH * PAIRS, D), F32),
        scratch_shapes=[pltpu.VMEM((2, RPS, D), F32), pltpu.SemaphoreType.DMA((2,))],
        compiler_params=_cparams(("arbitrary", "arbitrary")),
        name="ffn_down",
    )(dst_rows.reshape(N_EXPERTS, 1, R_EXP), hid, w_down, g_col)


def _combine_kernel(cs_ref, ce_ref, ys_hbm, x_ref, base_ref, basen_ref, g_ref, *rest):
    fw_ref = rest[0] if len(rest) == 5 else None
    o_ref, buf_ref, acc_ref, sem = rest[-4:]
    b = pl.program_id(0)
    t = pl.program_id(1)
    lo = cs_ref[b, t]
    hi = ce_ref[b, t]

    def chunk_copy(bb, c):
        return pltpu.make_async_copy(ys_hbm.at[pl.ds(bb * PAIRS + c * TT, TT)], buf_ref.at[c % 2], sem.at[c % 2])

    @pl.when((b == 0) & (t == 0) & (lo < hi))
    def _():
        chunk_copy(b, lo).start()

    acc_ref[...] = jnp.zeros((TT, D), F32)
    base = base_ref[0]
    basen = basen_ref[0]
    lane = lax.broadcasted_iota(I32, (1, TT), 1)

    def body(c, _):
        @pl.when(c + 1 < hi)
        def _():
            chunk_copy(b, c + 1).start()

        chunk_copy(b, c).wait()
        r = c * TT + lane
        onehot = jnp.where((base <= r) & (r < basen), 1.0, 0.0).astype(BF16)
        y = buf_ref[c % 2]
        y_hi = y.astype(BF16)
        y_lo = (y - y_hi.astype(F32)).astype(BF16)
        acc_ref[...] += (jnp.dot(onehot, y_hi, preferred_element_type=F32)
                         + jnp.dot(onehot, y_lo, preferred_element_type=F32))
        return 0

    lax.fori_loop(lo, hi, body, 0)

    wrap = t + 1 == N_TT
    nb = jnp.where(wrap, b + 1, b)
    nt = jnp.where(wrap, 0, t + 1)

    @pl.when(nb < BATCH)
    def _():
        nlo = cs_ref[nb, nt]

        @pl.when(nlo < ce_ref[nb, nt])
        def _():
            chunk_copy(nb, nlo).start()

    gate = jnp.where(t == 0, g_ref[0, 0, 0:1, :], g_ref[0, 0, 1:2, :])
    y = x_ref[0] + gate * acc_ref[...]
    o_ref[0] = y if fw_ref is None else _rms(y) * fw_ref[...]


def _combine(cs, ce, ys, x, base_col, basen_col, mt, final_w=None):
    in_specs = [
        pl.BlockSpec(memory_space=pl.ANY),
        pl.BlockSpec((1, TT, D), lambda b, t, *_: (b, t, 0)),
        pl.BlockSpec((1, TT, 1), lambda b, t, *_: (b, t, 0)),
        pl.BlockSpec((1, TT, 1), lambda b, t, *_: (b, t, 0)),
        pl.BlockSpec((1, 1, 2, D), lambda b, t, *_: (5, b, 0, 0)),
    ]
    args = [cs, ce, ys, x, base_col, basen_col, mt]
    if final_w is None:
        out_spec = pl.BlockSpec((1, TT, D), lambda b, t, *_: (b, t, 0))
        out_rows = S
    else:
        assert CTX == TT
        in_specs.append(pl.BlockSpec((1, D), lambda b, t, *_: (0, 0)))
        args.append(final_w.reshape(1, D))
        out_spec = pl.BlockSpec((1, TT, D), lambda b, t, *_: (b, jnp.maximum(t - 1, 0), 0))
        out_rows = SEQ
    grid_spec = pltpu.PrefetchScalarGridSpec(
        num_scalar_prefetch=2,
        grid=(BATCH, N_TT),
        in_specs=in_specs,
        out_specs=out_spec,
        scratch_shapes=[pltpu.VMEM((2, TT, D), F32), pltpu.VMEM((TT, D), F32), pltpu.SemaphoreType.DMA((2,))],
    )
    return pl.pallas_call(
        _combine_kernel,
        grid_spec=grid_spec,
        out_shape=jax.ShapeDtypeStruct((BATCH, out_rows, D), F32),
        compiler_params=_cparams(("arbitrary", "arbitrary")),
        name="moe_combine",
    )(*args)


def _moe(x, mt, norm_w, w_router, w_gate, w_up, w_down, li, final_w=None):
    logits = _router_logits(x, norm_w, mt, w_router)
    idx_l, g_l, dst_l, idx_c, g_c, dst_c, bases = _router_select(logits)
    boff = jnp.arange(BATCH, dtype=I32)[:, None, None]

    def rows(c, l, off):
        r = jnp.concatenate([c[:, :, :CAP_CTX], l], axis=2) + off
        return jnp.transpose(r, (1, 0, 2)).reshape(N_EXPERTS, R_EXP)

    src_rows = rows(idx_c, idx_l, boff * S)
    dst_rows = rows(dst_c, dst_l, boff * PAIRS)
    g_col = rows(g_c, g_l, 0.0).reshape(N_EXPERTS, R_EXP, 1)
    hid = _ffn_up(src_rows, x.reshape(BATCH * S, D), norm_w, mt, w_gate, w_up, li)
    ys = _ffn_down(dst_rows, hid, w_down, g_col, li)
    base, basen = bases[:, 0, :], bases[:, 1, :]
    cs = base[:, ::TT] // TT
    ce = (basen[:, TT - 1::TT] + TT - 1) // TT
    return _combine(cs, ce, ys, x, base.reshape(BATCH, S, 1), basen.reshape(BATCH, S, 1), mt, final_w)


def _pack_kernel(ctx_ref, x_ref, o_ref):
    @pl.when(pl.program_id(1) == 0)
    def _():
        o_ref[...] = ctx_ref[...]

    @pl.when(pl.program_id(1) > 0)
    def _():
        o_ref[...] = x_ref[...]


def _pack_rows(ctx, x):
    assert CTX == TT
    return pl.pallas_call(
        _pack_kernel,
        grid=(BATCH, N_TT),
        in_specs=[
            pl.BlockSpec((1, TT, D), lambda b, t: (b, 0, 0)),
            pl.BlockSpec((1, TT, D), lambda b, t: (b, jnp.maximum(t - 1, 0), 0)),
        ],
        out_specs=pl.BlockSpec((1, TT, D), lambda b, t: (b, t, 0)),
        out_shape=jax.ShapeDtypeStruct((BATCH, S, D), F32),
        compiler_params=_cparams(("arbitrary", "arbitrary")),
        name="pack_rows",
    )(ctx, x)


def kernel(x, c, ctx, c_ctx, ada_w, ada_b, norm_mix_w, norm_ffn_w,
           ev_w_in, ev_conv_w, ev_conv_b, ev_ra_w, ev_ra_b, ev_ix_w, ev_ix_b, ev_lambda, ev_sink, ev_w_out,
           od_w_in, od_conv_w, od_conv_b, od_gate_b, od_hnorm_w, od_w_out,
           moe_router, moe_w_gate, moe_w_up, moe_w_down, final_norm_w):
    assert x.shape == (BATCH, SEQ, D) and ctx.shape == (BATCH, CTX, D)
    cc = jnp.zeros((SUBLANE, D), F32).at[:BATCH].set(c).at[BATCH].set(c_ctx)
    mod = _modulation(cc, ada_w, ada_b)
    rope = _rope_tables()
    ev_w_in, ev_w_out, od_w_in, od_w_out = (w.astype(BF16) for w in (ev_w_in, ev_w_out, od_w_in, od_w_out))
    xs = _pack_rows(ctx, x)
    for layer in range(DEPTH):
        mt = _mod_table(mod[layer])
        i = layer // 2
        if layer % 2 == 0:
            xs = _even_mixer(xs, mt, norm_mix_w[layer], ev_w_in, i, ev_conv_w[i], ev_conv_b[i], ev_ra_w[i],
                             ev_ra_b[i], ev_ix_w[i], ev_ix_b[i], ev_lambda[i], ev_sink[i], ev_w_out, rope)
        else:
            xs = _odd_mixer(xs, mt, norm_mix_w[layer], od_w_in, i, od_conv_w[i], od_conv_b[i], od_gate_b[i],
                            od_hnorm_w[i], od_w_out)
        xs = _moe(xs, mt, norm_ffn_w[layer], moe_router[layer], moe_w_gate, moe_w_up, moe_w_down, layer,
                  final_norm_w if layer == DEPTH - 1 else None)
    return xs
```

```python
import functools

import jax
import jax.numpy as jnp
from jax import lax
from jax.experimental import pallas as pl
from jax.experimental.pallas import tpu as pltpu

F32 = jnp.float32
BF16 = jnp.bfloat16
I32 = jnp.int32

D = 2048
BATCH = 4
SEQ = 4096
CTX = 256
S = CTX + SEQ
DEPTH = 4
N_MOD = 6
EPS = 1e-6
GRID_W = 64

LRU_W = 1024
LRU_BLOCKS = 8
LRU_BLOCK = 128
LRU_C = 8.0
CONV_W = 4
ATT_HEADS = 8
KV_HEADS = 2
GROUP = ATT_HEADS // KV_HEADS
HEAD_DIM = 128
ATT_W = ATT_HEADS * HEAD_DIM
KV_W = KV_HEADS * HEAD_DIM
WINDOW = 128
ROPE_PAIRS = HEAD_DIM // 4
ROPE_BASE = 10000.0
EVEN_IN = 2 * LRU_W + ATT_W + 2 * KV_W

M_HEADS = 8
M_DK = 128
M_DV = 256
M_QK = M_HEADS * M_DK
M_V = M_HEADS * M_DV
ODD_MAIN = 2 * M_QK + 2 * M_V
N_GATES = 4 * M_HEADS

N_EXPERTS = 16
EC_FACTOR = 2
D_EXPERT = 1536
CAP_LAT = EC_FACTOR * SEQ // N_EXPERTS
CAP_CTX = EC_FACTOR * CTX // N_EXPERTS
RPS = CAP_CTX + CAP_LAT
R_EXP = BATCH * RPS
PAIRS = N_EXPERTS * RPS

LANE = 128
SUBLANE = 8
VMEM_LIMIT = 56 * 1024 * 1024
TM = S // 4
TILES_PER_SAMPLE = S // TM
TN = 512
TT = 256
N_TT = S // TT
CHUNK = 128
N_CHUNK = S // CHUNK
CTX_CHUNKS = CTX // CHUNK


def _cparams(sem, vmem=VMEM_LIMIT):
    return pltpu.CompilerParams(dimension_semantics=sem, vmem_limit_bytes=vmem)


def _sigmoid(x):
    return 1.0 / (1.0 + jnp.exp(-x))


def _silu(x):
    return x * _sigmoid(x)


def _softplus(x):
    return jnp.maximum(x, 0.0) + jnp.log1p(jnp.exp(-jnp.abs(x)))


def _log_sigmoid(x):
    return -_softplus(-x)


def _gelu_tanh(x):
    return 0.5 * x * (1.0 + jnp.tanh(0.7978845608028654 * (x + 0.044715 * (x * x * x))))


def _rms(x):
    return x * lax.rsqrt(jnp.mean(x * x, axis=-1, keepdims=True) + EPS)


def _ctx_select(is_ctx, tab):
    return jnp.where(is_ctx, tab[0:1, :], tab[1:2, :])


def _mod_kernel(c_ref, w_ref, b_ref, o_ref):
    a = _silu(c_ref[...]).astype(BF16)
    o_ref[0] = jnp.dot(a, w_ref[0].astype(BF16), preferred_element_type=F32) + b_ref[0]


def _modulation(cc, ada_w, ada_b):
    tn = 1024
    return pl.pallas_call(
        _mod_kernel,
        grid=(DEPTH, N_MOD * D // tn),
        in_specs=[
            pl.BlockSpec((SUBLANE, D), lambda l, j: (0, 0)),
            pl.BlockSpec((1, D, tn), lambda l, j: (l, 0, j)),
            pl.BlockSpec((1, 1, tn), lambda l, j: (l, 0, j)),
        ],
        out_specs=pl.BlockSpec((1, SUBLANE, tn), lambda l, j: (l, 0, j)),
        out_shape=jax.ShapeDtypeStruct((DEPTH, SUBLANE, N_MOD * D), F32),
        compiler_params=_cparams(("arbitrary", "arbitrary")),
        name="adaln_mod",
    )(cc, ada_w, ada_b.reshape(DEPTH, 1, N_MOD * D))


def _mod_table(mod_layer):
    m = mod_layer.reshape(SUBLANE, N_MOD, D)
    lat = jnp.transpose(m[:BATCH], (1, 0, 2))
    ctx = jnp.broadcast_to(m[BATCH][:, None, :], (N_MOD, BATCH, D))
    return jnp.stack([ctx, lat], axis=2)


def _norm_mod_rows(x, nw, sh_tab, sc_tab, row0):
    rows = x.shape[0]
    is_ctx = (row0 + lax.broadcasted_iota(I32, (rows, 1), 0)) < CTX
    y = _rms(x) * nw
    return y * (1.0 + _ctx_select(is_ctx, sc_tab)) + _ctx_select(is_ctx, sh_tab)


def _in_proj_kernel(x_ref, nw_ref, sh_ref, sc_ref, w_ref, *rest):
    i = pl.program_id(0)
    j = pl.program_id(1)
    o_ref, xn_ref = rest[-3 if len(rest) == 4 else 0], rest[-1]

    @pl.when(j == 0)
    def _():
        row0 = (i % TILES_PER_SAMPLE) * TM
        xn = _norm_mod_rows(x_ref[0], nw_ref[...], sh_ref[0, 0], sc_ref[0, 0], row0).astype(BF16)
        xn_ref[...] = xn
        if len(rest) == 4:
            rest[2][0] = jnp.dot(xn, rest[0][...].astype(BF16), preferred_element_type=F32)

    o_ref[0] = jnp.dot(xn_ref[...], w_ref[...].astype(BF16), preferred_element_type=F32).astype(o_ref.dtype)


def _in_proj(x, nw, mt, w, li, n_out, w_extra=None):
    tps = TILES_PER_SAMPLE
    tn = 2 * TN if n_out % (2 * TN) == 0 else TN
    row = lambda width, col: pl.BlockSpec((1, TM, width), lambda i, j: (i // tps, i % tps, col(j)))
    in_specs = [
        row(D, lambda j: 0),
        pl.BlockSpec((1, D), lambda i, j: (0, 0)),
        pl.BlockSpec((1, 1, 2, D), lambda i, j: (0, i // tps, 0, 0)),
        pl.BlockSpec((1, 1, 2, D), lambda i, j: (1, i // tps, 0, 0)),
        pl.BlockSpec((None, D, tn), lambda i, j: (li, 0, j)),
    ]
    out_specs = [row(tn, lambda j: j)]
    out_shape = [jax.ShapeDtypeStruct((BATCH, S, n_out), BF16)]
    args = [x, nw.reshape(1, D), mt, mt, w]
    if w_extra is not None:
        in_specs.append(pl.BlockSpec((D, LANE), lambda i, j: (0, 0)))
        out_specs.append(row(LANE, lambda j: 0))
        out_shape.append(jax.ShapeDtypeStruct((BATCH, S, LANE), F32))
        args.append(w_extra)
    res = pl.pallas_call(
        _in_proj_kernel,
        grid=(BATCH * tps, n_out // tn),
        in_specs=in_specs,
        out_specs=out_specs,
        out_shape=out_shape,
        scratch_shapes=[pltpu.VMEM((TM, D), BF16)],
        compiler_params=_cparams(("arbitrary", "arbitrary")),
        name="in_proj",
    )(*args)
    return res[0] if w_extra is None else res


def _residual_epilogue(i, x_ref, g_ref, acc, o_ref):
    tm = acc.shape[0]
    row0 = (i % (S // tm)) * tm
    is_ctx = (row0 + lax.broadcasted_iota(I32, (tm, 1), 0)) < CTX
    o_ref[0] = x_ref[0] + _ctx_select(is_ctx, g_ref[0, 0]) * acc


def _out_proj_even_kernel(a1_ref, a2_ref, x_ref, g_ref, w_ref, o_ref, a_ref):
    i = pl.program_id(0)
    j = pl.program_id(1)

    @pl.when(j == 0)
    def _():
        a_ref[:, :LRU_W] = a1_ref[0].astype(BF16)
        a_ref[:, LRU_W:] = a2_ref[0].astype(BF16)

    acc = jnp.dot(a_ref[...], w_ref[...].astype(BF16), preferred_element_type=F32)
    _residual_epilogue(i, x_ref, g_ref, acc, o_ref)


def _out_proj_odd_kernel(h_ref, og_ref, hw_ref, x_ref, g_ref, w_ref, o_ref, a_ref):
    i = pl.program_id(0)
    j = pl.program_id(1)

    @pl.when(j == 0)
    def _():
        for h in range(M_HEADS):
            sl = slice(h * M_DV, (h + 1) * M_DV)
            hn = _rms(h_ref[0, :, sl]) * hw_ref[:, sl] * _sigmoid(og_ref[0, :, sl].astype(F32))
            a_ref[:, sl] = hn.astype(BF16)

    acc = jnp.dot(a_ref[...], w_ref[...].astype(BF16), preferred_element_type=F32)
    _residual_epilogue(i, x_ref, g_ref, acc, o_ref)


def _row_spec(tm, width, col=None):
    tps = S // tm
    if col is None:
        return pl.BlockSpec((1, tm, width), lambda i, j: (i // tps, i % tps, j))
    return pl.BlockSpec((1, tm, width), lambda i, j: (i // tps, i % tps, col))


def _out_proj_even(a1, a2, x, mt, w, li):
    tm = TM
    return pl.pallas_call(
        _out_proj_even_kernel,
        grid=(BATCH * S // tm, D // TN),
        in_specs=[
            _row_spec(tm, LRU_W, 0),
            _row_spec(tm, ATT_W, 0),
            _row_spec(tm, TN),
            pl.BlockSpec((1, 1, 2, TN), lambda i, j: (2, i // (S // tm), 0, j)),
            pl.BlockSpec((None, D, TN), lambda i, j: (li, 0, j)),
        ],
        out_specs=_row_spec(tm, TN),
        out_shape=jax.ShapeDtypeStruct((BATCH, S, D), F32),
        scratch_shapes=[pltpu.VMEM((tm, D), BF16)],
        compiler_params=_cparams(("arbitrary", "arbitrary")),
        name="out_proj_even",
    )(a1, a2, x, mt, w)


def _out_proj_odd(h, p_odd, hnorm_w, x, mt, w, li):
    tm = TM
    return pl.pallas_call(
        _out_proj_odd_kernel,
        grid=(BATCH * S // tm, D // TN),
        in_specs=[
            _row_spec(tm, M_V, 0),
            _row_spec(tm, M_V, (2 * M_QK + M_V) // M_V),
            pl.BlockSpec((1, M_V), lambda i, j: (0, 0)),
            _row_spec(tm, TN),
            pl.BlockSpec((1, 1, 2, TN), lambda i, j: (2, i // (S // tm), 0, j)),
            pl.BlockSpec((None, D, TN), lambda i, j: (li, 0, j)),
        ],
        out_specs=_row_spec(tm, TN),
        out_shape=jax.ShapeDtypeStruct((BATCH, S, D), F32),
        scratch_shapes=[pltpu.VMEM((tm, D), BF16)],
        compiler_params=_cparams(("arbitrary", "arbitrary")),
        name="out_proj_odd",
    )(h, p_odd, hnorm_w.reshape(1, M_V), x, mt, w)


def _seg_conv(x, cw, cb):
    x = x.astype(F32)
    n = x.shape[0]
    row = lax.broadcasted_iota(I32, (n, 1), 0)
    seg = row < CTX
    y = cb
    for j in range(CONV_W):
        off = j - CONV_W // 2
        if off == 0:
            tap = x
        else:
            src = row + off
            ok = (src >= 0) & (src < n) & ((src < CTX) == seg)
            tap = jnp.where(ok, pltpu.roll(x, (-off) % n, axis=0), 0.0)
        y = y + tap * cw[j:j + 1, :]
    return y


def _bwd_chunk(j):
    return jnp.where(j < CTX_CHUNKS, CTX_CHUNKS - 1 - j, N_CHUNK + CTX_CHUNKS - 1 - j)


def _lin_scan(a, b, h_in, reverse):
    t_len, c = a.shape
    ng = t_len // SUBLANE
    a = a.reshape(ng, SUBLANE, c)
    b = b.reshape(ng, SUBLANE, c)
    sub = lax.broadcasted_iota(I32, a.shape, 1)
    k = 1
    while k < SUBLANE:
        shift = SUBLANE - k if reverse else k
        ok = (sub < SUBLANE - k) if reverse else (sub >= k)
        a_s = pltpu.roll(a, shift, axis=1)
        b_s = pltpu.roll(b, shift, axis=1)
        b = jnp.where(ok, a * b_s + b, b)
        a = jnp.where(ok, a * a_s, a)
        k *= 2
    edge = 0 if reverse else SUBLANE - 1
    hs = [None] * ng
    for g in (range(ng - 1, -1, -1) if reverse else range(ng)):
        hs[g] = b[g] + a[g] * h_in
        h_in = hs[g][edge:edge + 1, :]
    return jnp.concatenate(hs, axis=0)


LRU_UNROLL = 2


def _lru_kernel(xa_ref, ya_ref, cw_ref, cb_ref, raw_ref, rab_ref, ixw_ref, ixb_ref, lam_ref, o_ref, xc_ref, hf_ref):
    xc_ref[...] = _seg_conv(xa_ref[0], cw_ref[...], cb_ref[...])

    def gates(x, d):
        xb = x.astype(BF16)
        r = _sigmoid(jnp.dot(xb, raw_ref[d, 0].astype(BF16), preferred_element_type=F32) + rab_ref[d, 0])
        i = _sigmoid(jnp.dot(xb, ixw_ref[d, 0].astype(BF16), preferred_element_type=F32) + ixb_ref[d, 0])
        log_a = (-LRU_C * r) * _softplus(-lam_ref[d, 0])
        a = jnp.exp(log_a)
        return a, jnp.sqrt(-jnp.tanh(log_a) * (a * a + 1.0)) * (i * x)

    hf_ref[...] = jnp.zeros((S, LRU_BLOCK), F32)

    def one(chunk, h, d):
        rows = pl.ds(pl.multiple_of(chunk * CHUNK, CHUNK), CHUNK)
        a, b = gates(xc_ref[rows, :], d)
        h_all = _lin_scan(a, b, h, d == 1)
        hf_ref[rows, :] += h_all
        return h_all[0:1, :] if d == 1 else h_all[CHUNK - 1:CHUNK, :]

    def body(i, carry):
        hf, hb = carry
        for u in range(LRU_UNROLL):
            j = i * LRU_UNROLL + u
            hf, hb = one(j, hf, 0), one(_bwd_chunk(j), hb, 1)
        return hf, hb

    zero = jnp.zeros((1, LRU_BLOCK), F32)
    lax.fori_loop(0, N_CHUNK // LRU_UNROLL, body, (zero, zero))
    o_ref[0] = (hf_ref[...] * _gelu_tanh(ya_ref[0].astype(F32))).astype(o_ref.dtype)


def _lru(p_even, conv_w, conv_b, ra_w, ra_b, ix_w, ix_b, lam):
    nb = LRU_BLOCKS
    blk4 = lambda: pl.BlockSpec((2, 1, LRU_BLOCK, LRU_BLOCK), lambda b, k: (0, k, 0, 0))
    vec4 = lambda: pl.BlockSpec((2, 1, 1, LRU_BLOCK), lambda b, k: (0, k, 0, 0))
    return pl.pallas_call(
        _lru_kernel,
        grid=(BATCH, nb),
        in_specs=[
            pl.BlockSpec((1, S, LRU_BLOCK), lambda b, k: (b, 0, k)),
            pl.BlockSpec((1, S, LRU_BLOCK), lambda b, k: (b, 0, nb + k)),
            pl.BlockSpec((CONV_W, LRU_BLOCK), lambda b, k: (0, k)),
            pl.BlockSpec((1, LRU_BLOCK), lambda b, k: (0, k)),
            blk4(), vec4(), blk4(), vec4(), vec4(),
        ],
        out_specs=pl.BlockSpec((1, S, LRU_BLOCK), lambda b, k: (b, 0, k)),
        out_shape=jax.ShapeDtypeStruct((BATCH, S, LRU_W), BF16),
        scratch_shapes=[pltpu.VMEM((S, LRU_BLOCK), F32), pltpu.VMEM((S, LRU_BLOCK), F32)],
        compiler_params=_cparams(("arbitrary", "arbitrary")),
        name="rglru",
    )(p_even, p_even, conv_w, conv_b.reshape(1, LRU_W), ra_w, ra_b.reshape(2, nb, 1, LRU_BLOCK),
      ix_w, ix_b.reshape(2, nb, 1, LRU_BLOCK), lam.reshape(2, nb, 1, LRU_BLOCK))


def _rope(x, cos, sin):
    x = x.astype(F32)
    lane = lax.broadcasted_iota(I32, (1, HEAD_DIM), 1)
    first = (lane % (2 * ROPE_PAIRS)) < ROPE_PAIRS
    swapped = jnp.where(first, pltpu.roll(x, HEAD_DIM - ROPE_PAIRS, axis=1), pltpu.roll(x, ROPE_PAIRS, axis=1))
    return x * cos + swapped * sin


def _attn_kernel(sink_ref, q_ref, kp_ref, ko_ref, kn_ref, vp_ref, vo_ref, vn_ref, ck_ref, cv_ref,
                 cq_ref, sq_ref, cp_ref, sp_ref, cn_ref, sn_ref, o_ref):
    t = pl.program_id(1)
    nq = CHUNK
    lat = t >= CTX_CHUNKS
    c_lo = jnp.where(lat, jnp.where(t > CTX_CHUNKS, 0, nq), 0)
    c_hi = jnp.where(lat, jnp.where(t < N_CHUNK - 1, 3 * nq, 2 * nq), 0)
    r = lax.broadcasted_iota(I32, (nq, CTX + 3 * nq), 0)
    c = lax.broadcasted_iota(I32, (nq, CTX + 3 * nq), 1) - CTX
    band_ok = (jnp.abs(c - nq - r) <= WINDOW) & (c >= c_lo) & (c < c_hi)
    bias = jnp.where((c < 0) | band_ok, 0.0, -jnp.inf)
    bias = jnp.concatenate([bias] * GROUP, axis=0)
    scale = HEAD_DIM ** -0.5
    for g in range(KV_HEADS):
        ks = slice(g * HEAD_DIM, (g + 1) * HEAD_DIM)
        keys = jnp.concatenate([
            ck_ref[0, :, ks].astype(F32),
            _rope(kp_ref[0, :, ks], cp_ref[...], sp_ref[...]),
            _rope(ko_ref[0, :, ks], cq_ref[...], sq_ref[...]),
            _rope(kn_ref[0, :, ks], cn_ref[...], sn_ref[...]),
        ], axis=0).astype(BF16)
        vals = jnp.concatenate([cv_ref[0, :, ks], vp_ref[0, :, ks], vo_ref[0, :, ks], vn_ref[0, :, ks]],
                               axis=0).astype(BF16)
        heads = [g * GROUP + hh for hh in range(GROUP)]
        qg = jnp.concatenate([_rope(q_ref[0, :, h * HEAD_DIM:(h + 1) * HEAD_DIM], cq_ref[...], sq_ref[...])
                              for h in heads], axis=0).astype(BF16)
        sink = jnp.concatenate([jnp.full((nq, 1), sink_ref[h], F32) for h in heads], axis=0)
        s = lax.dot_general(qg, keys, (((1,), (1,)), ((), ())), preferred_element_type=F32) * scale + bias
        m = jnp.maximum(jnp.max(s, axis=-1, keepdims=True), sink)
        p = jnp.exp(s - m)
        den = jnp.sum(p, axis=-1, keepdims=True) + jnp.exp(sink - m)
        out = jnp.dot((p / den).astype(BF16), vals, preferred_element_type=F32)
        for hh, h in enumerate(heads):
            o_ref[0, :, h * HEAD_DIM:(h + 1) * HEAD_DIM] = out[hh * nq:(hh + 1) * nq].astype(o_ref.dtype)


def _attention(p_even, sink, cos_t, sin_t):
    qc = 2 * LRU_W // ATT_W
    kc = (2 * LRU_W + ATT_W) // KV_W
    vc = kc + 1
    lo, hi = CTX_CHUNKS, N_CHUNK - 1
    prev = lambda t: jnp.clip(t - 1, lo, hi)
    nxt = lambda t: jnp.clip(t + 1, lo, hi)
    own = lambda t: t
    kv = lambda col, f: pl.BlockSpec((1, CHUNK, KV_W), lambda b, t: (b, f(t), col))
    tab = lambda f: pl.BlockSpec((CHUNK, HEAD_DIM), lambda b, t: (f(t), 0))
    return pl.pallas_call(
        _attn_kernel,
        grid=(BATCH, N_CHUNK),
        in_specs=[
            pl.BlockSpec(memory_space=pltpu.SMEM),
            pl.BlockSpec((1, CHUNK, ATT_W), lambda b, t: (b, t, qc)),
            kv(kc, prev), kv(kc, own), kv(kc, nxt),
            kv(vc, prev), kv(vc, own), kv(vc, nxt),
            pl.BlockSpec((1, CTX, KV_W), lambda b, t: (b, 0, kc)),
            pl.BlockSpec((1, CTX, KV_W), lambda b, t: (b, 0, vc)),
            tab(own), tab(own), tab(prev), tab(prev), tab(nxt), tab(nxt),
        ],
        out_specs=pl.BlockSpec((1, CHUNK, ATT_W), lambda b, t: (b, t, 0)),
        out_shape=jax.ShapeDtypeStruct((BATCH, S, ATT_W), BF16),
        compiler_params=_cparams(("arbitrary", "arbitrary")),
        name="window_attention",
    )(sink, p_even, p_even, p_even, p_even, p_even, p_even, p_even, p_even, p_even,
      cos_t, sin_t, cos_t, sin_t, cos_t, sin_t)


def _rope_tables():
    inv = jnp.power(ROPE_BASE, -jnp.arange(ROPE_PAIRS, dtype=F32) / ROPE_PAIRS)
    pos = jnp.arange(SEQ)
    row_ang = (pos // GRID_W).astype(F32)[:, None] * inv
    col_ang = (pos % GRID_W).astype(F32)[:, None] * inv
    cos = jnp.concatenate([jnp.cos(row_ang)] * 2 + [jnp.cos(col_ang)] * 2, axis=-1)
    sin = jnp.concatenate([-jnp.sin(row_ang), jnp.sin(row_ang), -jnp.sin(col_ang), jnp.sin(col_ang)], axis=-1)
    cos = jnp.concatenate([jnp.ones((CTX, HEAD_DIM), F32), cos], axis=0)
    sin = jnp.concatenate([jnp.zeros((CTX, HEAD_DIM), F32), sin], axis=0)
    return cos, sin


def _mlstm_local(qb, kb, k, vb, li_row, b_row, reverse):
    ln = qb.shape[0]
    row = lax.broadcasted_iota(I32, (ln, ln), 0)
    col = lax.broadcasted_iota(I32, (ln, ln), 1)
    b_r = jnp.broadcast_to(b_row, (ln, ln))
    li_r = jnp.broadcast_to(li_row, (ln, ln))
    b = b_r.T
    li = li_r.T
    causal = (col >= row) if reverse else (col <= row)
    logw = jnp.where(causal, b - b_r + li_r, -jnp.inf)
    m_loc = jnp.max(logw, axis=1, keepdims=True)
    s = lax.dot_general(qb, kb, (((1,), (1,)), ((), ())), preferred_element_type=F32) * jnp.exp(logw - m_loc)
    num = jnp.dot(s.astype(BF16), vb, preferred_element_type=F32)
    den = jnp.sum(s, axis=1, keepdims=True)
    edge = 0 if reverse else ln - 1
    b_last = b[edge:edge + 1, :]
    log_u = b_last - b + li
    mu_loc = jnp.max(log_u, axis=0, keepdims=True)
    uk = jnp.exp(log_u - mu_loc) * k
    kv = lax.dot_general(uk.astype(BF16), vb, (((0,), (0,)), ((), ())), preferred_element_type=F32)
    return dict(qb=qb, b=b, m_loc=m_loc, num=num, den=den, b_last=b_last, mu_loc=mu_loc, kv=kv,
                ksum=jnp.sum(uk, axis=0, keepdims=True))


def _mlstm_apply(loc, state):
    c0, n0, m0 = state
    b, m_loc = loc["b"], loc["m_loc"]
    m = jnp.maximum(b + m0, m_loc)
    intra = jnp.exp(m_loc - m)
    inter = jnp.exp(b + m0 - m)
    qc = jnp.dot(loc["qb"], c0.astype(BF16), preferred_element_type=F32)
    n8 = jnp.broadcast_to(n0, (SUBLANE, n0.shape[1])).astype(BF16)
    qn = lax.dot_general(loc["qb"], n8, (((1,), (1,)), ((), ())), preferred_element_type=F32)[:, :1]
    num = jnp.concatenate([intra, intra], axis=1) * loc["num"] + jnp.concatenate([inter, inter], axis=1) * qc
    den = intra[:, :1] * loc["den"] + inter[:, :1] * qn
    h = num / jnp.maximum(jnp.abs(den), jnp.exp(-m[:, :1]))
    m_new = jnp.maximum(loc["b_last"] + m0, loc["mu_loc"])
    decay = jnp.exp(loc["b_last"] + m0 - m_new)
    grow = jnp.exp(loc["mu_loc"] - m_new)
    c_new = decay[:, :1] * c0 + grow[:, :1] * loc["kv"]
    n_new = decay * n0 + grow * loc["ksum"]
    return h, (c_new, n_new, m_new)


MLSTM_UNROLL = 2


def _mlstm_kernel(gb_ref, q_ref, k_ref, v_ref, g_ref, cwq_ref, cbq_ref, cwk_ref, cbk_ref, o_ref,
                  qb_ref, kb_ref, kc_ref, vb_ref, gr_ref):
    hd = pl.program_id(1)
    ln = CHUNK
    qb_ref[...] = (_silu(_seg_conv(q_ref[0], cwq_ref[...], cbq_ref[...])) * (M_DK ** -0.5)).astype(BF16)
    kc = _silu(_seg_conv(k_ref[0], cwk_ref[...], cbk_ref[...]))
    kc_ref[...] = kc
    kb_ref[...] = kc.astype(BF16)
    vb_ref[...] = v_ref[0].astype(BF16)
    o_ref[0] = jnp.zeros((S, M_DV), F32)

    def rows_of(chunk):
        return pl.ds(pl.multiple_of(chunk * ln, ln), ln)

    sub = lax.broadcasted_iota(I32, (N_GATES, 1), 0)

    def gate_rows(i, _):
        for u in range(MLSTM_UNROLL):
            c = i * MLSTM_UNROLL + u
            gt = g_ref[0, rows_of(c), :].T[:N_GATES]
            pick = lambda ty: jnp.sum(jnp.where(sub == ty * M_HEADS + hd, gt, 0.0), axis=0, keepdims=True) \
                + gb_ref[ty, hd]
            gr_ref[c] = jnp.concatenate([pick(0), _log_sigmoid(pick(1)), pick(2), _log_sigmoid(pick(3)),
                                         jnp.zeros((SUBLANE - 4, ln), F32)], axis=0)
        return 0

    lax.fori_loop(0, N_CHUNK // MLSTM_UNROLL, gate_rows, 0)
    gr = gr_ref[...]
    lane = lax.broadcasted_iota(I32, gr.shape, 2)
    kind = lax.broadcasted_iota(I32, gr.shape, 1)
    pre = suf = gr
    kk = 1
    while kk < ln:
        pre = pre + jnp.where(lane >= kk, pltpu.roll(pre, kk, axis=2), 0.0)
        suf = suf + jnp.where(lane < ln - kk, pltpu.roll(suf, ln - kk, axis=2), 0.0)
        kk *= 2
    gr_ref[...] = jnp.where(kind == 1, pre, jnp.where(kind == 3, suf, gr))

    def local(chunk, d):
        rows = rows_of(chunk)
        gr = gr_ref[chunk]
        return _mlstm_local(qb_ref[rows, :], kb_ref[rows, :], kc_ref[rows, :], vb_ref[rows, :],
                            gr[2 * d:2 * d + 1], gr[2 * d + 1:2 * d + 2], d == 1)

    def body(i, carry):
        steps = [i * MLSTM_UNROLL + u for u in range(MLSTM_UNROLL)]
        chunks = ([j for j in steps], [_bwd_chunk(j) for j in steps])
        locs = [[local(c, d) for c in chunks[d]] for d in range(2)]
        states = list(carry)
        for u in range(MLSTM_UNROLL):
            for d in range(2):
                h, states[d] = _mlstm_apply(locs[d][u], states[d])
                o_ref[0, rows_of(chunks[d][u]), :] += h
        return tuple(states)

    zero = (jnp.zeros((M_DK, M_DV), F32), jnp.zeros((1, M_DK), F32), jnp.zeros((1, ln), F32))
    lax.fori_loop(0, N_CHUNK // MLSTM_UNROLL, body, (zero, zero))


def _mlstm(p_odd, gates, conv_w, conv_b, gate_b):
    nh = M_HEADS
    cw = lambda off: pl.BlockSpec((CONV_W, M_DK), lambda b, h: (0, off + h))
    cb = lambda off: pl.BlockSpec((1, M_DK), lambda b, h: (0, off + h))
    conv_b = conv_b.reshape(1, 2 * M_QK)
    assert N_CHUNK % MLSTM_UNROLL == 0 and CHUNK == M_DK
    return pl.pallas_call(
        _mlstm_kernel,
        grid=(BATCH, nh),
        in_specs=[
            pl.BlockSpec(memory_space=pltpu.SMEM),
            pl.BlockSpec((1, S, M_DK), lambda b, h: (b, 0, h)),
            pl.BlockSpec((1, S, M_DK), lambda b, h: (b, 0, nh + h)),
            pl.BlockSpec((1, S, M_DV), lambda b, h: (b, 0, 2 * M_QK // M_DV + h)),
            pl.BlockSpec((1, S, LANE), lambda b, h: (b, 0, 0)),
            cw(0), cb(0), cw(nh), cb(nh),
        ],
        out_specs=pl.BlockSpec((1, S, M_DV), lambda b, h: (b, 0, h)),
        out_shape=jax.ShapeDtypeStruct((BATCH, S, M_V), F32),
        scratch_shapes=[pltpu.VMEM((S, M_DK), BF16), pltpu.VMEM((S, M_DK), BF16), pltpu.VMEM((S, M_DK), F32),
                        pltpu.VMEM((S, M_DV), BF16), pltpu.VMEM((N_CHUNK, SUBLANE, CHUNK), F32)],
        compiler_params=_cparams(("arbitrary", "arbitrary")),
        name="mlstm",
    )(gate_b, p_odd, p_odd, p_odd, gates, conv_w, conv_b, conv_w, conv_b)


def _even_mixer(x, mt, norm_w, w_in, li, conv_w, conv_b, ra_w, ra_b, ix_w, ix_b, lam, sink, w_out, rope):
    p = _in_proj(x, norm_w, mt, w_in, li, EVEN_IN)
    a = _lru(p, conv_w, conv_b, ra_w, ra_b, ix_w, ix_b, lam)
    b = _attention(p, sink, rope[0], rope[1])
    return _out_proj_even(a, b, x, mt, w_out, li)


def _odd_mixer(x, mt, norm_w, w_in, w_gates, li, conv_w, conv_b, gate_b, hnorm_w, w_out):
    wg = jnp.pad(w_gates[li], ((0, 0), (0, LANE - N_GATES)))
    p, gates = _in_proj(x, norm_w, mt, w_in, li, ODD_MAIN, w_extra=wg)
    h = _mlstm(p, gates, conv_w, conv_b, gate_b)
    return _out_proj_odd(h, p, hnorm_w, x, mt, w_out, li)


def _router_logits_kernel(x_ref, nw_ref, sh_ref, sc_ref, w_ref, o_ref):
    t = pl.program_id(1)
    h = _norm_mod_rows(x_ref[0], nw_ref[...], sh_ref[0, 0], sc_ref[0, 0], t * TT)

    def split(a):
        hi = a.astype(BF16)
        return hi, (a - hi.astype(F32)).astype(BF16)

    nt = lambda a, b: lax.dot_general(a, b, (((1,), (1,)), ((), ())), preferred_element_type=F32)
    (w_hi, w_lo), (h_hi, h_lo) = split(w_ref[...]), split(h)
    o_ref[0] = nt(w_hi, h_hi) + (nt(w_hi, h_lo) + nt(w_lo, h_hi))


def _router_logits(x, nw, mt, w_router):
    return pl.pallas_call(
        _router_logits_kernel,
        grid=(BATCH, N_TT),
        in_specs=[
            pl.BlockSpec((1, TT, D), lambda b, t: (b, t, 0)),
            pl.BlockSpec((1, D), lambda b, t: (0, 0)),
            pl.BlockSpec((1, 1, 2, D), lambda b, t: (3, b, 0, 0)),
            pl.BlockSpec((1, 1, 2, D), lambda b, t: (4, b, 0, 0)),
            pl.BlockSpec((N_EXPERTS, D), lambda b, t: (0, 0)),
        ],
        out_specs=pl.BlockSpec((1, N_EXPERTS, TT), lambda b, t: (b, 0, t)),
        out_shape=jax.ShapeDtypeStruct((BATCH, N_EXPERTS, S), F32),
        compiler_params=_cparams(("arbitrary", "arbitrary")),
        name="router_logits",
    )(x, nw.reshape(1, D), mt, mt, w_router.T)


def _cumsum_lanes(x):
    n = x.shape[1]
    lane = lax.broadcasted_iota(I32, x.shape, 1)
    k = 1
    while k < n:
        x = x + jnp.where(lane >= k, pltpu.roll(x, k, axis=1), 0.0)
        k *= 2
    return x


def _cumsum_rows_excl(x):
    n = x.shape[0]
    row = lax.broadcasted_iota(I32, x.shape, 0)
    inc = x
    k = 1
    while k < n:
        inc = inc + jnp.where(row >= k, pltpu.roll(inc, k, axis=0), 0.0)
        k *= 2
    return inc - x


def _split3(x):
    hi = x.astype(BF16).astype(F32)
    r = x - hi
    mid = r.astype(BF16).astype(F32)
    return hi, mid, (r - mid).astype(BF16).astype(F32)


def _router_select_kernel(lg_ref, idxl_ref, gl_ref, dstl_ref, idxc_ref, gc_ref, dstc_ref, base_ref, vt_ref):
    ne = N_EXPERTS
    lg = lg_ref[0]
    ex = jnp.exp(lg - jnp.max(lg, axis=0, keepdims=True))
    aff = ex / jnp.sum(ex, axis=0, keepdims=True)
    bits = pltpu.bitcast(aff, I32)
    lane = lax.broadcasted_iota(I32, (ne, S), 1)
    sel = jnp.zeros((ne, S), F32)
    pos = jnp.zeros((ne, S), F32)
    for lo, hi, cap in ((0, CTX, CAP_CTX), (CTX, S, CAP_LAT)):
        vb = jnp.where((lane >= lo) & (lane < hi), bits, -1)
        thr = jnp.zeros((ne, 1), I32)
        for bit in range(30, -1, -1):
            cand = thr | (1 << bit)
            cnt = jnp.sum(jnp.where(vb >= cand, 1.0, 0.0), axis=1, keepdims=True)
            thr = jnp.where(cnt >= cap, cand, thr)
        gt = vb > thr
        eq = jnp.where(vb == thr, 1.0, 0.0)
        need = cap - jnp.sum(jnp.where(gt, 1.0, 0.0), axis=1, keepdims=True)
        eq_rank = _cumsum_lanes(eq) - eq
        s_seg = jnp.where(gt | ((eq > 0.0) & (eq_rank < need)), 1.0, 0.0)
        sel = sel + s_seg
        pos = pos + s_seg * (_cumsum_lanes(s_seg) - s_seg)
    cnt_tok = jnp.broadcast_to(jnp.sum(sel, axis=0, keepdims=True), (ne, S))
    base = _cumsum_lanes(cnt_tok) - cnt_tok
    dest = base + _cumsum_rows_excl(sel)
    base_ref[0] = jnp.concatenate([base[0:1], base[0:1] + cnt_tok[0:1], jnp.zeros((SUBLANE - 2, S), F32)],
                                  axis=0).astype(I32)
    posm = jnp.where(sel > 0.0, pos, -1.0)
    lane_f = lane[0:1].astype(F32)
    idx_hi = jnp.floor(lane_f * (1.0 / 64.0))
    idx_lo = lane_f - 64.0 * idx_hi
    dst_hi = jnp.floor(dest * (1.0 / 128.0))
    dst_lo = dest - 128.0 * dst_hi
    g_hi, g_mid, g_lo = _split3(aff)
    for e in range(ne):
        vt_ref[e] = jnp.concatenate([idx_hi, idx_lo, dst_hi[e:e + 1], dst_lo[e:e + 1], g_hi[e:e + 1],
                                     g_mid[e:e + 1], g_lo[e:e + 1], posm[e:e + 1]], axis=0)
    slot = lax.broadcasted_iota(I32, (LANE, 1), 0).astype(F32)

    def compact(vt, prow, s0):
        onehot = jnp.where(prow == slot + s0, 1.0, 0.0).astype(BF16)
        res = lax.dot_general(vt, onehot, (((1,), (1,)), ((), ())), preferred_element_type=F32)
        return (res[0:1] * 64.0 + res[1:2]).astype(I32), res[4:5] + res[5:6] + res[6:7], \
            (res[2:3] * 128.0 + res[3:4]).astype(I32)

    def per_expert(e, _):
        blk = vt_ref[e]
        prow = blk[SUBLANE - 1:SUBLANE]
        vt = blk.astype(BF16)
        idxc_ref[0, e], gc_ref[0, e], dstc_ref[0, e] = compact(vt[:, :CTX], prow[:, :CTX], 0.0)
        for sc in range(CAP_LAT // LANE):
            cs = slice(sc * LANE, (sc + 1) * LANE)
            idxl_ref[0, e, :, cs], gl_ref[0, e, :, cs], dstl_ref[0, e, :, cs] = compact(
                vt[:, CTX:], prow[:, CTX:], float(sc * LANE))
        return 0

    lax.fori_loop(0, ne, per_expert, 0)


def _router_select(logits):
    ne = N_EXPERTS
    out = lambda n, dt: jax.ShapeDtypeStruct((BATCH, ne, 1, n), dt)
    ospec = lambda n: pl.BlockSpec((1, ne, 1, n), lambda b: (b, 0, 0, 0))
    res = pl.pallas_call(
        _router_select_kernel,
        grid=(BATCH,),
        in_specs=[pl.BlockSpec((1, ne, S), lambda b: (b, 0, 0))],
        out_specs=[ospec(CAP_LAT), ospec(CAP_LAT), ospec(CAP_LAT), ospec(LANE), ospec(LANE), ospec(LANE),
                   pl.BlockSpec((1, SUBLANE, S), lambda b: (b, 0, 0))],
        out_shape=[out(CAP_LAT, I32), out(CAP_LAT, F32), out(CAP_LAT, I32), out(LANE, I32), out(LANE, F32),
                   out(LANE, I32), jax.ShapeDtypeStruct((BATCH, SUBLANE, S), I32)],
        scratch_shapes=[pltpu.VMEM((ne, SUBLANE, S), F32)],
        compiler_params=_cparams(("arbitrary",)),
        name="router_select",
    )(logits)
    return [r.reshape(BATCH, ne, r.shape[-1]) for r in res[:6]] + [res[6]]


ROW_UNROLL = 16


def _start_rows(n_rows, start_one):
    def body(i, _):
        for u in range(ROW_UNROLL):
            start_one(i * ROW_UNROLL + u)
        return 0

    lax.fori_loop(0, n_rows // ROW_UNROLL, body, 0)


def _ffn_up_kernel(src_ref, x_hbm, nw_ref, sh_ref, sc_ref, wg_ref, wu_ref, o_ref, xs_ref, gbuf_ref, sem):
    j = pl.program_id(1)

    @pl.when(j == 0)
    def _():
        def gather(b):
            slot = b % 2
            _start_rows(RPS, lambda r: pltpu.make_async_copy(
                x_hbm.at[pl.ds(src_ref[0, 0, b * RPS + r], 1)], gbuf_ref.at[slot, pl.ds(r, 1)],
                sem.at[slot]).start())

        gather(0)
        for b in range(BATCH):
            slot = b % 2
            if b + 1 < BATCH:
                gather(b + 1)
            pltpu.make_async_copy(x_hbm.at[pl.ds(0, RPS)], gbuf_ref.at[slot], sem.at[slot]).wait()
            is_ctx = lax.broadcasted_iota(I32, (RPS, 1), 0) < CAP_CTX
            y = _rms(gbuf_ref[slot]) * nw_ref[...]
            y = y * (1.0 + _ctx_select(is_ctx, sc_ref[0, b])) + _ctx_select(is_ctx, sh_ref[0, b])
            xs_ref[b * RPS:(b + 1) * RPS, :] = y.astype(BF16)

    xs = xs_ref[...]
    hg = jnp.dot(xs, wg_ref[0].astype(BF16), preferred_element_type=F32)
    hu = jnp.dot(xs, wu_ref[0].astype(BF16), preferred_element_type=F32)
    o_ref[0] = (_silu(hg) * hu).astype(BF16)


def _ffn_up(src_rows, x2d, nw, mt, w_gate, w_up, li):
    tf = 256
    assert RPS % ROW_UNROLL == 0
    return pl.pallas_call(
        _ffn_up_kernel,
        grid=(N_EXPERTS, D_EXPERT // tf),
        in_specs=[
            pl.BlockSpec((1, 1, R_EXP), lambda e, j: (e, 0, 0), memory_space=pltpu.SMEM),
            pl.BlockSpec(memory_space=pl.ANY),
            pl.BlockSpec((1, D), lambda e, j: (0, 0)),
            pl.BlockSpec((1, BATCH, 2, D), lambda e, j: (3, 0, 0, 0)),
            pl.BlockSpec((1, BATCH, 2, D), lambda e, j: (4, 0, 0, 0)),
            pl.BlockSpec((None, 1, D, tf), lambda e, j: (li, e, 0, j)),
            pl.BlockSpec((None, 1, D, tf), lambda e, j: (li, e, 0, j)),
        ],
        out_specs=pl.BlockSpec((1, R_EXP, tf), lambda e, j: (e, 0, j)),
        out_shape=jax.ShapeDtypeStruct((N_EXPERTS, R_EXP, D_EXPERT), BF16),
        scratch_shapes=[pltpu.VMEM((R_EXP, D), BF16), pltpu.VMEM((2, RPS, D), F32),
                        pltpu.SemaphoreType.DMA((2,))],
        compiler_params=_cparams(("arbitrary", "arbitrary")),
        name="ffn_up",
    )(src_rows.reshape(N_EXPERTS, 1, R_EXP), x2d, nw.reshape(1, D), mt, mt, w_gate, w_up)


def _ffn_down_kernel(dst_ref, hid_ref, w_ref, g_ref, y_hbm, ybuf_ref, sem):
    i = pl.program_id(1)
    step = pl.program_id(0) * BATCH + i
    slot = i % 2
    acc = jnp.dot(hid_ref[0], w_ref[0].astype(BF16), preferred_element_type=F32)
    ybuf_ref[slot] = acc * g_ref[0]

    def drain(s):
        pltpu.make_async_copy(ybuf_ref.at[s], y_hbm.at[pl.ds(0, RPS)], sem.at[s]).wait()

    @pl.when(step > 0)
    def _():
        drain(1 - slot)

    _start_rows(RPS, lambda r: pltpu.make_async_copy(
        ybuf_ref.at[slot, pl.ds(r, 1)], y_hbm.at[pl.ds(dst_ref[0, 0, i * RPS + r], 1)], sem.at[slot]).start())

    @pl.when(step == N_EXPERTS * BATCH - 1)
    def _():
        drain(slot)


def _ffn_down(dst_rows, hid, w_down, g_col, li):
    assert BATCH % 2 == 0
    return pl.pallas_call(
        _ffn_down_kernel,
        grid=(N_EXPERTS, BATCH),
        in_specs=[
            pl.BlockSpec((1, 1, R_EXP), lambda e, i: (e, 0, 0), memory_space=pltpu.SMEM),
            pl.BlockSpec((1, RPS, D_EXPERT), lambda e, i: (e, i, 0)),
            pl.BlockSpec((None, 1, D_EXPERT, D), lambda e, i: (li, e, 0, 0)),
            pl.BlockSpec((1, RPS, 1), lambda e, i: (e, i, 0)),
        ],
        out_specs=pl.BlockSpec(memory_space=pl.ANY),
        out_shape=jax.ShapeDtypeStruct((BATCH * PAIRS, D), F32),
        scratch_shapes=[pltpu.VMEM((2, RPS, D), F32), pltpu.SemaphoreType.DMA((2,))],
        compiler_params=_cparams(("arbitrary", "arbitrary")),
        name="ffn_down",
    )(dst_rows.reshape(N_EXPERTS, 1, R_EXP), hid, w_down, g_col)


def _combine_kernel(cs_ref, ce_ref, ys_hbm, x_ref, base_ref, basen_ref, g_ref, *rest):
    fw_ref = rest[0] if len(rest) == 5 else None
    o_ref, buf_ref, acc_ref, sem = rest[-4:]
    b = pl.program_id(0)
    t = pl.program_id(1)
    lo = cs_ref[b, t]
    hi = ce_ref[b, t]

    def chunk_copy(bb, c):
        return pltpu.make_async_copy(ys_hbm.at[pl.ds(bb * PAIRS + c * TT, TT)], buf_ref.at[c % 2], sem.at[c % 2])

    @pl.when((b == 0) & (t == 0) & (lo < hi))
    def _():
        chunk_copy(b, lo).start()

    acc_ref[...] = jnp.zeros((TT, D), F32)
    base = base_ref[0]
    basen = basen_ref[0]
    lane = lax.broadcasted_iota(I32, (1, TT), 1)

    def body(c, _):
        @pl.when(c + 1 < hi)
        def _():
            chunk_copy(b, c + 1).start()

        chunk_copy(b, c).wait()
        r = c * TT + lane
        onehot = jnp.where((base <= r) & (r < basen), 1.0, 0.0).astype(BF16)
        y = buf_ref[c % 2]
        y_hi = y.astype(BF16)
        y_lo = (y - y_hi.astype(F32)).astype(BF16)
        acc_ref[...] += (jnp.dot(onehot, y_hi, preferred_element_type=F32)
                         + jnp.dot(onehot, y_lo, preferred_element_type=F32))
        return 0

    lax.fori_loop(lo, hi, body, 0)

    wrap = t + 1 == N_TT
    nb = jnp.where(wrap, b + 1, b)
    nt = jnp.where(wrap, 0, t + 1)

    @pl.when(nb < BATCH)
    def _():
        nlo = cs_ref[nb, nt]

        @pl.when(nlo < ce_ref[nb, nt])
        def _():
            chunk_copy(nb, nlo).start()

    gate = jnp.where(t == 0, g_ref[0, 0, 0:1, :], g_ref[0, 0, 1:2, :])
    y = x_ref[0] + gate * acc_ref[...]
    o_ref[0] = y if fw_ref is None else _rms(y) * fw_ref[...]


def _combine(cs, ce, ys, x, base_col, basen_col, mt, final_w=None):
    in_specs = [
        pl.BlockSpec(memory_space=pl.ANY),
        pl.BlockSpec((1, TT, D), lambda b, t, *_: (b, t, 0)),
        pl.BlockSpec((1, TT, 1), lambda b, t, *_: (b, t, 0)),
        pl.BlockSpec((1, TT, 1), lambda b, t, *_: (b, t, 0)),
        pl.BlockSpec((1, 1, 2, D), lambda b, t, *_: (5, b, 0, 0)),
    ]
    args = [cs, ce, ys, x, base_col, basen_col, mt]
    if final_w is None:
        out_spec = pl.BlockSpec((1, TT, D), lambda b, t, *_: (b, t, 0))
        out_rows = S
    else:
        assert CTX == TT
        in_specs.append(pl.BlockSpec((1, D), lambda b, t, *_: (0, 0)))
        args.append(final_w.reshape(1, D))
        out_spec = pl.BlockSpec((1, TT, D), lambda b, t, *_: (b, jnp.maximum(t - 1, 0), 0))
        out_rows = SEQ
    grid_spec = pltpu.PrefetchScalarGridSpec(
        num_scalar_prefetch=2,
        grid=(BATCH, N_TT),
        in_specs=in_specs,
        out_specs=out_spec,
        scratch_shapes=[pltpu.VMEM((2, TT, D), F32), pltpu.VMEM((TT, D), F32), pltpu.SemaphoreType.DMA((2,))],
    )
    return pl.pallas_call(
        _combine_kernel,
        grid_spec=grid_spec,
        out_shape=jax.ShapeDtypeStruct((BATCH, out_rows, D), F32),
        compiler_params=_cparams(("arbitrary", "arbitrary")),
        name="moe_combine",
    )(*args)


def _moe(x, mt, norm_w, w_router, w_gate, w_up, w_down, li, final_w=None):
    logits = _router_logits(x, norm_w, mt, w_router)
    idx_l, g_l, dst_l, idx_c, g_c, dst_c, bases = _router_select(logits)
    boff = jnp.arange(BATCH, dtype=I32)[:, None, None]

    def rows(c, l, off):
        r = jnp.concatenate([c[:, :, :CAP_CTX], l], axis=2) + off
        return jnp.transpose(r, (1, 0, 2)).reshape(N_EXPERTS, R_EXP)

    src_rows = rows(idx_c, idx_l, boff * S)
    dst_rows = rows(dst_c, dst_l, boff * PAIRS)
    g_col = rows(g_c, g_l, 0.0).reshape(N_EXPERTS, R_EXP, 1)
    hid = _ffn_up(src_rows, x.reshape(BATCH * S, D), norm_w, mt, w_gate, w_up, li)
    ys = _ffn_down(dst_rows, hid, w_down, g_col, li)
    base, basen = bases[:, 0, :], bases[:, 1, :]
    cs = base[:, ::TT] // TT
    ce = (basen[:, TT - 1::TT] + TT - 1) // TT
    return _combine(cs, ce, ys, x, base.reshape(BATCH, S, 1), basen.reshape(BATCH, S, 1), mt, final_w)


def _pack_kernel(ctx_ref, x_ref, o_ref):
    @pl.when(pl.program_id(1) == 0)
    def _():
        o_ref[...] = ctx_ref[...]

    @pl.when(pl.program_id(1) > 0)
    def _():
        o_ref[...] = x_ref[...]


def _pack_rows(ctx, x):
    assert CTX == TT
    return pl.pallas_call(
        _pack_kernel,
        grid=(BATCH, N_TT),
        in_specs=[
            pl.BlockSpec((1, TT, D), lambda b, t: (b, 0, 0)),
            pl.BlockSpec((1, TT, D), lambda b, t: (b, jnp.maximum(t - 1, 0), 0)),
        ],
        out_specs=pl.BlockSpec((1, TT, D), lambda b, t: (b, t, 0)),
        out_shape=jax.ShapeDtypeStruct((BATCH, S, D), F32),
        compiler_params=_cparams(("arbitrary", "arbitrary")),
        name="pack_rows",
    )(ctx, x)


def kernel(x, c, ctx, c_ctx, ada_w, ada_b, norm_mix_w, norm_ffn_w,
           ev_w_in, ev_conv_w, ev_conv_b, ev_ra_w, ev_ra_b, ev_ix_w, ev_ix_b, ev_lambda, ev_sink, ev_w_out,
           od_w_in, od_conv_w, od_conv_b, od_gate_b, od_hnorm_w, od_w_out,
           moe_router, moe_w_gate, moe_w_up, moe_w_down, final_norm_w):
    assert x.shape == (BATCH, SEQ, D) and ctx.shape == (BATCH, CTX, D)
    cc = jnp.zeros((SUBLANE, D), F32).at[:BATCH].set(c).at[BATCH].set(c_ctx)
    mod = _modulation(cc, ada_w, ada_b)
    rope = _rope_tables()
    od_w_gates = od_w_in[:, :, ODD_MAIN:]
    ev_w_in, ev_w_out, od_w_in, od_w_out = (w.astype(BF16) for w in (
        ev_w_in, ev_w_out, od_w_in[:, :, :ODD_MAIN], od_w_out))
    xs = _pack_rows(ctx, x)
    for layer in range(DEPTH):
        mt = _mod_table(mod[layer])
        i = layer // 2
        if layer % 2 == 0:
            xs = _even_mixer(xs, mt, norm_mix_w[layer], ev_w_in, i, ev_conv_w[i], ev_conv_b[i], ev_ra_w[i],
                             ev_ra_b[i], ev_ix_w[i], ev_ix_b[i], ev_lambda[i], ev_sink[i], ev_w_out, rope)
        else:
            xs = _odd_mixer(xs, mt, norm_mix_w[layer], od_w_in, od_w_gates, i, od_conv_w[i], od_conv_b[i],
                            od_gate_b[i], od_hnorm_w[i], od_w_out)
        xs = _moe(xs, mt, norm_ffn_w[layer], moe_router[layer], moe_w_gate, moe_w_up, moe_w_down, layer,
                  final_norm_w if layer == DEPTH - 1 else None)
    return xs
```

```python
import functools

import jax
import jax.numpy as jnp
from jax import lax
from jax.experimental import pallas as pl
from jax.experimental.pallas import tpu as pltpu

F32 = jnp.float32
BF16 = jnp.bfloat16
I32 = jnp.int32

D = 2048
BATCH = 4
SEQ = 4096
CTX = 256
S = CTX + SEQ
DEPTH = 4
N_MOD = 6
EPS = 1e-6
GRID_W = 64

LRU_W = 1024
LRU_BLOCKS = 8
LRU_BLOCK = 128
LRU_C = 8.0
CONV_W = 4
ATT_HEADS = 8
KV_HEADS = 2
GROUP = ATT_HEADS // KV_HEADS
HEAD_DIM = 128
ATT_W = ATT_HEADS * HEAD_DIM
KV_W = KV_HEADS * HEAD_DIM
WINDOW = 128
ROPE_PAIRS = HEAD_DIM // 4
ROPE_BASE = 10000.0
EVEN_IN = 2 * LRU_W + ATT_W + 2 * KV_W

M_HEADS = 8
M_DK = 128
M_DV = 256
M_QK = M_HEADS * M_DK
M_V = M_HEADS * M_DV
ODD_MAIN = 2 * M_QK + 2 * M_V
N_GATES = 4 * M_HEADS

N_EXPERTS = 16
EC_FACTOR = 2
D_EXPERT = 1536
CAP_LAT = EC_FACTOR * SEQ // N_EXPERTS
CAP_CTX = EC_FACTOR * CTX // N_EXPERTS
RPS = CAP_CTX + CAP_LAT
R_EXP = BATCH * RPS
PAIRS = N_EXPERTS * RPS

LANE = 128
SUBLANE = 8
VMEM_LIMIT = 56 * 1024 * 1024
TM = S // 4
TILES_PER_SAMPLE = S // TM
TN = 512
TT = 256
N_TT = S // TT
CHUNK = 128
N_CHUNK = S // CHUNK
CTX_CHUNKS = CTX // CHUNK


def _cparams(sem, vmem=VMEM_LIMIT):
    return pltpu.CompilerParams(dimension_semantics=sem, vmem_limit_bytes=vmem)


def _sigmoid(x):
    return 1.0 / (1.0 + jnp.exp(-x))


def _silu(x):
    return x * _sigmoid(x)


def _softplus(x):
    return jnp.maximum(x, 0.0) + jnp.log1p(jnp.exp(-jnp.abs(x)))


def _log_sigmoid(x):
    return -_softplus(-x)


def _gelu_tanh(x):
    return 0.5 * x * (1.0 + jnp.tanh(0.7978845608028654 * (x + 0.044715 * (x * x * x))))


def _rms(x):
    return x * lax.rsqrt(jnp.mean(x * x, axis=-1, keepdims=True) + EPS)


def _ctx_select(is_ctx, tab):
    return jnp.where(is_ctx, tab[0:1, :], tab[1:2, :])


def _mod_kernel(c_ref, w_ref, b_ref, o_ref):
    a = _silu(c_ref[...]).astype(BF16)
    o_ref[0] = jnp.dot(a, w_ref[0].astype(BF16), preferred_element_type=F32) + b_ref[0]


def _modulation(cc, ada_w, ada_b):
    tn = 1024
    return pl.pallas_call(
        _mod_kernel,
        grid=(DEPTH, N_MOD * D // tn),
        in_specs=[
            pl.BlockSpec((SUBLANE, D), lambda l, j: (0, 0)),
            pl.BlockSpec((1, D, tn), lambda l, j: (l, 0, j)),
            pl.BlockSpec((1, 1, tn), lambda l, j: (l, 0, j)),
        ],
        out_specs=pl.BlockSpec((1, SUBLANE, tn), lambda l, j: (l, 0, j)),
        out_shape=jax.ShapeDtypeStruct((DEPTH, SUBLANE, N_MOD * D), F32),
        compiler_params=_cparams(("arbitrary", "arbitrary")),
        name="adaln_mod",
    )(cc, ada_w, ada_b.reshape(DEPTH, 1, N_MOD * D))


def _mod_table(mod_layer):
    m = mod_layer.reshape(SUBLANE, N_MOD, D)
    lat = jnp.transpose(m[:BATCH], (1, 0, 2))
    ctx = jnp.broadcast_to(m[BATCH][:, None, :], (N_MOD, BATCH, D))
    return jnp.stack([ctx, lat], axis=2)


def _norm_mod_rows(x, nw, sh_tab, sc_tab, row0):
    rows = x.shape[0]
    is_ctx = (row0 + lax.broadcasted_iota(I32, (rows, 1), 0)) < CTX
    y = _rms(x) * nw
    return y * (1.0 + _ctx_select(is_ctx, sc_tab)) + _ctx_select(is_ctx, sh_tab)


def _in_proj_kernel(x_ref, nw_ref, sh_ref, sc_ref, w_ref, *rest):
    i = pl.program_id(0)
    j = pl.program_id(1)
    o_ref, xn_ref = rest[-3 if len(rest) == 4 else 0], rest[-1]

    @pl.when(j == 0)
    def _():
        row0 = (i % TILES_PER_SAMPLE) * TM
        xn = _norm_mod_rows(x_ref[0], nw_ref[...], sh_ref[0, 0], sc_ref[0, 0], row0).astype(BF16)
        xn_ref[...] = xn
        if len(rest) == 4:
            rest[2][0] = jnp.dot(xn, rest[0][...].astype(BF16), preferred_element_type=F32)

    o_ref[0] = jnp.dot(xn_ref[...], w_ref[...].astype(BF16), preferred_element_type=F32).astype(o_ref.dtype)


def _in_proj(x, nw, mt, w, li, n_out, w_extra=None):
    tps = TILES_PER_SAMPLE
    tn = 2 * TN if n_out % (2 * TN) == 0 else TN
    row = lambda width, col: pl.BlockSpec((1, TM, width), lambda i, j: (i // tps, i % tps, col(j)))
    in_specs = [
        row(D, lambda j: 0),
        pl.BlockSpec((1, D), lambda i, j: (0, 0)),
        pl.BlockSpec((1, 1, 2, D), lambda i, j: (0, i // tps, 0, 0)),
        pl.BlockSpec((1, 1, 2, D), lambda i, j: (1, i // tps, 0, 0)),
        pl.BlockSpec((None, D, tn), lambda i, j: (li, 0, j)),
    ]
    out_specs = [row(tn, lambda j: j)]
    out_shape = [jax.ShapeDtypeStruct((BATCH, S, n_out), BF16)]
    args = [x, nw.reshape(1, D), mt, mt, w]
    if w_extra is not None:
        in_specs.append(pl.BlockSpec((D, LANE), lambda i, j: (0, 0)))
        out_specs.append(row(LANE, lambda j: 0))
        out_shape.append(jax.ShapeDtypeStruct((BATCH, S, LANE), F32))
        args.append(w_extra)
    res = pl.pallas_call(
        _in_proj_kernel,
        grid=(BATCH * tps, n_out // tn),
        in_specs=in_specs,
        out_specs=out_specs,
        out_shape=out_shape,
        scratch_shapes=[pltpu.VMEM((TM, D), BF16)],
        compiler_params=_cparams(("arbitrary", "arbitrary")),
        name="in_proj",
    )(*args)
    return res[0] if w_extra is None else res


def _residual_epilogue(i, x_ref, g_ref, acc, o_ref):
    tm = acc.shape[0]
    row0 = (i % (S // tm)) * tm
    is_ctx = (row0 + lax.broadcasted_iota(I32, (tm, 1), 0)) < CTX
    o_ref[0] = x_ref[0] + _ctx_select(is_ctx, g_ref[0, 0]) * acc


def _out_proj_even_kernel(a1_ref, a2_ref, x_ref, g_ref, w_ref, o_ref, a_ref):
    i = pl.program_id(0)
    j = pl.program_id(1)

    @pl.when(j == 0)
    def _():
        a_ref[:, :LRU_W] = a1_ref[0].astype(BF16)
        a_ref[:, LRU_W:] = a2_ref[0].astype(BF16)

    acc = jnp.dot(a_ref[...], w_ref[...].astype(BF16), preferred_element_type=F32)
    _residual_epilogue(i, x_ref, g_ref, acc, o_ref)


def _out_proj_odd_kernel(h_ref, og_ref, hw_ref, x_ref, g_ref, w_ref, o_ref, a_ref):
    i = pl.program_id(0)
    j = pl.program_id(1)

    @pl.when(j == 0)
    def _():
        for h in range(M_HEADS):
            sl = slice(h * M_DV, (h + 1) * M_DV)
            hn = _rms(h_ref[0, :, sl]) * hw_ref[:, sl] * _sigmoid(og_ref[0, :, sl].astype(F32))
            a_ref[:, sl] = hn.astype(BF16)

    acc = jnp.dot(a_ref[...], w_ref[...].astype(BF16), preferred_element_type=F32)
    _residual_epilogue(i, x_ref, g_ref, acc, o_ref)


def _row_spec(tm, width, col=None):
    tps = S // tm
    if col is None:
        return pl.BlockSpec((1, tm, width), lambda i, j: (i // tps, i % tps, j))
    return pl.BlockSpec((1, tm, width), lambda i, j: (i // tps, i % tps, col))


def _out_proj_even(a1, a2, x, mt, w, li):
    tm = TM
    return pl.pallas_call(
        _out_proj_even_kernel,
        grid=(BATCH * S // tm, D // TN),
        in_specs=[
            _row_spec(tm, LRU_W, 0),
            _row_spec(tm, ATT_W, 0),
            _row_spec(tm, TN),
            pl.BlockSpec((1, 1, 2, TN), lambda i, j: (2, i // (S // tm), 0, j)),
            pl.BlockSpec((None, D, TN), lambda i, j: (li, 0, j)),
        ],
        out_specs=_row_spec(tm, TN),
        out_shape=jax.ShapeDtypeStruct((BATCH, S, D), F32),
        scratch_shapes=[pltpu.VMEM((tm, D), BF16)],
        compiler_params=_cparams(("arbitrary", "arbitrary")),
        name="out_proj_even",
    )(a1, a2, x, mt, w)


def _out_proj_odd(h, p_odd, hnorm_w, x, mt, w, li):
    tm = TM
    return pl.pallas_call(
        _out_proj_odd_kernel,
        grid=(BATCH * S // tm, D // TN),
        in_specs=[
            _row_spec(tm, M_V, 0),
            _row_spec(tm, M_V, (2 * M_QK + M_V) // M_V),
            pl.BlockSpec((1, M_V), lambda i, j: (0, 0)),
            _row_spec(tm, TN),
            pl.BlockSpec((1, 1, 2, TN), lambda i, j: (2, i // (S // tm), 0, j)),
            pl.BlockSpec((None, D, TN), lambda i, j: (li, 0, j)),
        ],
        out_specs=_row_spec(tm, TN),
        out_shape=jax.ShapeDtypeStruct((BATCH, S, D), F32),
        scratch_shapes=[pltpu.VMEM((tm, D), BF16)],
        compiler_params=_cparams(("arbitrary", "arbitrary")),
        name="out_proj_odd",
    )(h, p_odd, hnorm_w.reshape(1, M_V), x, mt, w)


def _seg_conv(x, cw, cb):
    x = x.astype(F32)
    n = x.shape[0]
    row = lax.broadcasted_iota(I32, (n, 1), 0)
    seg = row < CTX
    y = cb
    for j in range(CONV_W):
        off = j - CONV_W // 2
        if off == 0:
            tap = x
        else:
            src = row + off
            ok = (src >= 0) & (src < n) & ((src < CTX) == seg)
            tap = jnp.where(ok, pltpu.roll(x, (-off) % n, axis=0), 0.0)
        y = y + tap * cw[j:j + 1, :]
    return y


def _bwd_chunk(j):
    return jnp.where(j < CTX_CHUNKS, CTX_CHUNKS - 1 - j, N_CHUNK + CTX_CHUNKS - 1 - j)


def _lin_scan(a, b, h_in, reverse):
    t_len, c = a.shape
    ng = t_len // SUBLANE
    a = a.reshape(ng, SUBLANE, c)
    b = b.reshape(ng, SUBLANE, c)
    sub = lax.broadcasted_iota(I32, a.shape, 1)
    k = 1
    while k < SUBLANE:
        shift = SUBLANE - k if reverse else k
        ok = (sub < SUBLANE - k) if reverse else (sub >= k)
        a_s = pltpu.roll(a, shift, axis=1)
        b_s = pltpu.roll(b, shift, axis=1)
        b = jnp.where(ok, a * b_s + b, b)
        a = jnp.where(ok, a * a_s, a)
        k *= 2
    edge = 0 if reverse else SUBLANE - 1
    hs = [None] * ng
    for g in (range(ng - 1, -1, -1) if reverse else range(ng)):
        hs[g] = b[g] + a[g] * h_in
        h_in = hs[g][edge:edge + 1, :]
    return jnp.concatenate(hs, axis=0)


LRU_UNROLL = 2


def _lru_kernel(xa_ref, ya_ref, cw_ref, cb_ref, raw_ref, rab_ref, ixw_ref, ixb_ref, lam_ref, o_ref, xc_ref, hf_ref):
    xc_ref[...] = _seg_conv(xa_ref[0], cw_ref[...], cb_ref[...])

    def gates(x, d):
        xb = x.astype(BF16)
        r = _sigmoid(jnp.dot(xb, raw_ref[d, 0].astype(BF16), preferred_element_type=F32) + rab_ref[d, 0])
        i = _sigmoid(jnp.dot(xb, ixw_ref[d, 0].astype(BF16), preferred_element_type=F32) + ixb_ref[d, 0])
        log_a = (-LRU_C * r) * _softplus(-lam_ref[d, 0])
        a = jnp.exp(log_a)
        return a, jnp.sqrt(-jnp.tanh(log_a) * (a * a + 1.0)) * (i * x)

    hf_ref[...] = jnp.zeros((S, LRU_BLOCK), F32)

    def one(chunk, h, d):
        rows = pl.ds(pl.multiple_of(chunk * CHUNK, CHUNK), CHUNK)
        a, b = gates(xc_ref[rows, :], d)
        h_all = _lin_scan(a, b, h, d == 1)
        hf_ref[rows, :] += h_all
        return h_all[0:1, :] if d == 1 else h_all[CHUNK - 1:CHUNK, :]

    def body(i, carry):
        hf, hb = carry
        for u in range(LRU_UNROLL):
            j = i * LRU_UNROLL + u
            hf, hb = one(j, hf, 0), one(_bwd_chunk(j), hb, 1)
        return hf, hb

    zero = jnp.zeros((1, LRU_BLOCK), F32)
    lax.fori_loop(0, N_CHUNK // LRU_UNROLL, body, (zero, zero))
    o_ref[0] = (hf_ref[...] * _gelu_tanh(ya_ref[0].astype(F32))).astype(o_ref.dtype)


def _lru(p_even, conv_w, conv_b, ra_w, ra_b, ix_w, ix_b, lam):
    nb = LRU_BLOCKS
    blk4 = lambda: pl.BlockSpec((2, 1, LRU_BLOCK, LRU_BLOCK), lambda b, k: (0, k, 0, 0))
    vec4 = lambda: pl.BlockSpec((2, 1, 1, LRU_BLOCK), lambda b, k: (0, k, 0, 0))
    return pl.pallas_call(
        _lru_kernel,
        grid=(BATCH, nb),
        in_specs=[
            pl.BlockSpec((1, S, LRU_BLOCK), lambda b, k: (b, 0, k)),
            pl.BlockSpec((1, S, LRU_BLOCK), lambda b, k: (b, 0, nb + k)),
            pl.BlockSpec((CONV_W, LRU_BLOCK), lambda b, k: (0, k)),
            pl.BlockSpec((1, LRU_BLOCK), lambda b, k: (0, k)),
            blk4(), vec4(), blk4(), vec4(), vec4(),
        ],
        out_specs=pl.BlockSpec((1, S, LRU_BLOCK), lambda b, k: (b, 0, k)),
        out_shape=jax.ShapeDtypeStruct((BATCH, S, LRU_W), BF16),
        scratch_shapes=[pltpu.VMEM((S, LRU_BLOCK), F32), pltpu.VMEM((S, LRU_BLOCK), F32)],
        compiler_params=_cparams(("arbitrary", "arbitrary")),
        name="rglru",
    )(p_even, p_even, conv_w, conv_b.reshape(1, LRU_W), ra_w, ra_b.reshape(2, nb, 1, LRU_BLOCK),
      ix_w, ix_b.reshape(2, nb, 1, LRU_BLOCK), lam.reshape(2, nb, 1, LRU_BLOCK))


def _rope(x, cos, sin):
    x = x.astype(F32)
    lane = lax.broadcasted_iota(I32, (1, HEAD_DIM), 1)
    first = (lane % (2 * ROPE_PAIRS)) < ROPE_PAIRS
    swapped = jnp.where(first, pltpu.roll(x, HEAD_DIM - ROPE_PAIRS, axis=1), pltpu.roll(x, ROPE_PAIRS, axis=1))
    return x * cos + swapped * sin


def _attn_kernel(sink_ref, q_ref, kp_ref, ko_ref, kn_ref, vp_ref, vo_ref, vn_ref, ck_ref, cv_ref,
                 cq_ref, sq_ref, cp_ref, sp_ref, cn_ref, sn_ref, o_ref):
    t = pl.program_id(1)
    nq = CHUNK
    lat = t >= CTX_CHUNKS
    c_lo = jnp.where(lat, jnp.where(t > CTX_CHUNKS, 0, nq), 0)
    c_hi = jnp.where(lat, jnp.where(t < N_CHUNK - 1, 3 * nq, 2 * nq), 0)
    r = lax.broadcasted_iota(I32, (nq, CTX + 3 * nq), 0)
    c = lax.broadcasted_iota(I32, (nq, CTX + 3 * nq), 1) - CTX
    band_ok = (jnp.abs(c - nq - r) <= WINDOW) & (c >= c_lo) & (c < c_hi)
    bias = jnp.where((c < 0) | band_ok, 0.0, -jnp.inf)
    bias = jnp.concatenate([bias] * GROUP, axis=0)
    scale = HEAD_DIM ** -0.5
    for g in range(KV_HEADS):
        ks = slice(g * HEAD_DIM, (g + 1) * HEAD_DIM)
        keys = jnp.concatenate([
            ck_ref[0, :, ks].astype(F32),
            _rope(kp_ref[0, :, ks], cp_ref[...], sp_ref[...]),
            _rope(ko_ref[0, :, ks], cq_ref[...], sq_ref[...]),
            _rope(kn_ref[0, :, ks], cn_ref[...], sn_ref[...]),
        ], axis=0).astype(BF16)
        vals = jnp.concatenate([cv_ref[0, :, ks], vp_ref[0, :, ks], vo_ref[0, :, ks], vn_ref[0, :, ks]],
                               axis=0).astype(BF16)
        heads = [g * GROUP + hh for hh in range(GROUP)]
        qg = jnp.concatenate([_rope(q_ref[0, :, h * HEAD_DIM:(h + 1) * HEAD_DIM], cq_ref[...], sq_ref[...])
                              for h in heads], axis=0).astype(BF16)
        sink = jnp.concatenate([jnp.full((nq, 1), sink_ref[h], F32) for h in heads], axis=0)
        s = lax.dot_general(qg, keys, (((1,), (1,)), ((), ())), preferred_element_type=F32) * scale + bias
        m = jnp.maximum(jnp.max(s, axis=-1, keepdims=True), sink)
        p = jnp.exp(s - m)
        den = jnp.sum(p, axis=-1, keepdims=True) + jnp.exp(sink - m)
        out = jnp.dot((p / den).astype(BF16), vals, preferred_element_type=F32)
        for hh, h in enumerate(heads):
            o_ref[0, :, h * HEAD_DIM:(h + 1) * HEAD_DIM] = out[hh * nq:(hh + 1) * nq].astype(o_ref.dtype)


def _attention(p_even, sink, cos_t, sin_t):
    qc = 2 * LRU_W // ATT_W
    kc = (2 * LRU_W + ATT_W) // KV_W
    vc = kc + 1
    lo, hi = CTX_CHUNKS, N_CHUNK - 1
    prev = lambda t: jnp.clip(t - 1, lo, hi)
    nxt = lambda t: jnp.clip(t + 1, lo, hi)
    own = lambda t: t
    kv = lambda col, f: pl.BlockSpec((1, CHUNK, KV_W), lambda b, t: (b, f(t), col))
    tab = lambda f: pl.BlockSpec((CHUNK, HEAD_DIM), lambda b, t: (f(t), 0))
    return pl.pallas_call(
        _attn_kernel,
        grid=(BATCH, N_CHUNK),
        in_specs=[
            pl.BlockSpec(memory_space=pltpu.SMEM),
            pl.BlockSpec((1, CHUNK, ATT_W), lambda b, t: (b, t, qc)),
            kv(kc, prev), kv(kc, own), kv(kc, nxt),
            kv(vc, prev), kv(vc, own), kv(vc, nxt),
            pl.BlockSpec((1, CTX, KV_W), lambda b, t: (b, 0, kc)),
            pl.BlockSpec((1, CTX, KV_W), lambda b, t: (b, 0, vc)),
            tab(own), tab(own), tab(prev), tab(prev), tab(nxt), tab(nxt),
        ],
        out_specs=pl.BlockSpec((1, CHUNK, ATT_W), lambda b, t: (b, t, 0)),
        out_shape=jax.ShapeDtypeStruct((BATCH, S, ATT_W), BF16),
        compiler_params=_cparams(("arbitrary", "arbitrary")),
        name="window_attention",
    )(sink, p_even, p_even, p_even, p_even, p_even, p_even, p_even, p_even, p_even,
      cos_t, sin_t, cos_t, sin_t, cos_t, sin_t)


def _rope_tables():
    inv = jnp.power(ROPE_BASE, -jnp.arange(ROPE_PAIRS, dtype=F32) / ROPE_PAIRS)
    pos = jnp.arange(SEQ)
    row_ang = (pos // GRID_W).astype(F32)[:, None] * inv
    col_ang = (pos % GRID_W).astype(F32)[:, None] * inv
    cos = jnp.concatenate([jnp.cos(row_ang)] * 2 + [jnp.cos(col_ang)] * 2, axis=-1)
    sin = jnp.concatenate([-jnp.sin(row_ang), jnp.sin(row_ang), -jnp.sin(col_ang), jnp.sin(col_ang)], axis=-1)
    cos = jnp.concatenate([jnp.ones((CTX, HEAD_DIM), F32), cos], axis=0)
    sin = jnp.concatenate([jnp.zeros((CTX, HEAD_DIM), F32), sin], axis=0)
    return cos, sin


def _mlstm_local(qb, kb, k, vb, li_row, b_row, reverse):
    ln = qb.shape[0]
    row = lax.broadcasted_iota(I32, (ln, ln), 0)
    col = lax.broadcasted_iota(I32, (ln, ln), 1)
    b_r = jnp.broadcast_to(b_row, (ln, ln))
    li_r = jnp.broadcast_to(li_row, (ln, ln))
    b = b_r.T
    li = li_r.T
    causal = (col >= row) if reverse else (col <= row)
    logw = jnp.where(causal, b - b_r + li_r, -jnp.inf)
    m_loc = jnp.max(logw, axis=1, keepdims=True)
    s = lax.dot_general(qb, kb, (((1,), (1,)), ((), ())), preferred_element_type=F32) * jnp.exp(logw - m_loc)
    num = jnp.dot(s.astype(BF16), vb, preferred_element_type=F32)
    den = jnp.sum(s, axis=1, keepdims=True)
    edge = 0 if reverse else ln - 1
    b_last = b[edge:edge + 1, :]
    log_u = b_last - b + li
    mu_loc = jnp.max(log_u, axis=0, keepdims=True)
    uk = jnp.exp(log_u - mu_loc) * k
    kv = lax.dot_general(uk.astype(BF16), vb, (((0,), (0,)), ((), ())), preferred_element_type=F32)
    return dict(qb=qb, b=b, m_loc=m_loc, num=num, den=den, b_last=b_last, mu_loc=mu_loc, kv=kv,
                ksum=jnp.sum(uk, axis=0, keepdims=True))


def _mlstm_apply(loc, state):
    c0, n0, m0 = state
    b, m_loc = loc["b"], loc["m_loc"]
    m = jnp.maximum(b + m0, m_loc)
    intra = jnp.exp(m_loc - m)
    inter = jnp.exp(b + m0 - m)
    qc = jnp.dot(loc["qb"], c0.astype(BF16), preferred_element_type=F32)
    n8 = jnp.broadcast_to(n0, (SUBLANE, n0.shape[1])).astype(BF16)
    qn = lax.dot_general(loc["qb"], n8, (((1,), (1,)), ((), ())), preferred_element_type=F32)[:, :1]
    num = jnp.concatenate([intra, intra], axis=1) * loc["num"] + jnp.concatenate([inter, inter], axis=1) * qc
    den = intra[:, :1] * loc["den"] + inter[:, :1] * qn
    h = num / jnp.maximum(jnp.abs(den), jnp.exp(-m[:, :1]))
    m_new = jnp.maximum(loc["b_last"] + m0, loc["mu_loc"])
    decay = jnp.exp(loc["b_last"] + m0 - m_new)
    grow = jnp.exp(loc["mu_loc"] - m_new)
    c_new = decay[:, :1] * c0 + grow[:, :1] * loc["kv"]
    n_new = decay * n0 + grow * loc["ksum"]
    return h, (c_new, n_new, m_new)


MLSTM_UNROLL = 2


def _mlstm_kernel(gb_ref, q_ref, k_ref, v_ref, g_ref, cwq_ref, cbq_ref, cwk_ref, cbk_ref, o_ref,
                  qb_ref, kb_ref, kc_ref, vb_ref, gr_ref):
    hd = pl.program_id(1)
    ln = CHUNK
    qb_ref[...] = (_silu(_seg_conv(q_ref[0], cwq_ref[...], cbq_ref[...])) * (M_DK ** -0.5)).astype(BF16)
    kc = _silu(_seg_conv(k_ref[0], cwk_ref[...], cbk_ref[...]))
    kc_ref[...] = kc
    kb_ref[...] = kc.astype(BF16)
    vb_ref[...] = v_ref[0].astype(BF16)
    o_ref[0] = jnp.zeros((S, M_DV), F32)

    def rows_of(chunk):
        return pl.ds(pl.multiple_of(chunk * ln, ln), ln)

    sub = lax.broadcasted_iota(I32, (N_GATES, 1), 0)

    def gate_rows(i, _):
        for u in range(MLSTM_UNROLL):
            c = i * MLSTM_UNROLL + u
            gt = g_ref[0, rows_of(c), :].T[:N_GATES]
            pick = lambda ty: jnp.sum(jnp.where(sub == ty * M_HEADS + hd, gt, 0.0), axis=0, keepdims=True) \
                + gb_ref[ty, hd]
            gr_ref[c] = jnp.concatenate([pick(0), _log_sigmoid(pick(1)), pick(2), _log_sigmoid(pick(3)),
                                         jnp.zeros((SUBLANE - 4, ln), F32)], axis=0)
        return 0

    lax.fori_loop(0, N_CHUNK // MLSTM_UNROLL, gate_rows, 0)
    gr = gr_ref[...]
    lane = lax.broadcasted_iota(I32, gr.shape, 2)
    kind = lax.broadcasted_iota(I32, gr.shape, 1)
    pre = suf = gr
    kk = 1
    while kk < ln:
        pre = pre + jnp.where(lane >= kk, pltpu.roll(pre, kk, axis=2), 0.0)
        suf = suf + jnp.where(lane < ln - kk, pltpu.roll(suf, ln - kk, axis=2), 0.0)
        kk *= 2
    gr_ref[...] = jnp.where(kind == 1, pre, jnp.where(kind == 3, suf, gr))

    def local(chunk, d):
        rows = rows_of(chunk)
        gr = gr_ref[chunk]
        return _mlstm_local(qb_ref[rows, :], kb_ref[rows, :], kc_ref[rows, :], vb_ref[rows, :],
                            gr[2 * d:2 * d + 1], gr[2 * d + 1:2 * d + 2], d == 1)

    def body(i, carry):
        steps = [i * MLSTM_UNROLL + u for u in range(MLSTM_UNROLL)]
        chunks = ([j for j in steps], [_bwd_chunk(j) for j in steps])
        locs = [[local(c, d) for c in chunks[d]] for d in range(2)]
        states = list(carry)
        for u in range(MLSTM_UNROLL):
            for d in range(2):
                h, states[d] = _mlstm_apply(locs[d][u], states[d])
                o_ref[0, rows_of(chunks[d][u]), :] += h
        return tuple(states)

    zero = (jnp.zeros((M_DK, M_DV), F32), jnp.zeros((1, M_DK), F32), jnp.zeros((1, ln), F32))
    lax.fori_loop(0, N_CHUNK // MLSTM_UNROLL, body, (zero, zero))


def _mlstm(p_odd, gates, conv_w, conv_b, gate_b):
    nh = M_HEADS
    cw = lambda off: pl.BlockSpec((CONV_W, M_DK), lambda b, h: (0, off + h))
    cb = lambda off: pl.BlockSpec((1, M_DK), lambda b, h: (0, off + h))
    conv_b = conv_b.reshape(1, 2 * M_QK)
    assert N_CHUNK % MLSTM_UNROLL == 0 and CHUNK == M_DK
    return pl.pallas_call(
        _mlstm_kernel,
        grid=(BATCH, nh),
        in_specs=[
            pl.BlockSpec(memory_space=pltpu.SMEM),
            pl.BlockSpec((1, S, M_DK), lambda b, h: (b, 0, h)),
            pl.BlockSpec((1, S, M_DK), lambda b, h: (b, 0, nh + h)),
            pl.BlockSpec((1, S, M_DV), lambda b, h: (b, 0, 2 * M_QK // M_DV + h)),
            pl.BlockSpec((1, S, LANE), lambda b, h: (b, 0, 0)),
            cw(0), cb(0), cw(nh), cb(nh),
        ],
        out_specs=pl.BlockSpec((1, S, M_DV), lambda b, h: (b, 0, h)),
        out_shape=jax.ShapeDtypeStruct((BATCH, S, M_V), F32),
        scratch_shapes=[pltpu.VMEM((S, M_DK), BF16), pltpu.VMEM((S, M_DK), BF16), pltpu.VMEM((S, M_DK), F32),
                        pltpu.VMEM((S, M_DV), BF16), pltpu.VMEM((N_CHUNK, SUBLANE, CHUNK), F32)],
        compiler_params=_cparams(("arbitrary", "arbitrary")),
        name="mlstm",
    )(gate_b, p_odd, p_odd, p_odd, gates, conv_w, conv_b, conv_w, conv_b)


def _even_mixer(x, mt, norm_w, w_in, li, conv_w, conv_b, ra_w, ra_b, ix_w, ix_b, lam, sink, w_out, rope):
    p = _in_proj(x, norm_w, mt, w_in, li, EVEN_IN)
    a = _lru(p, conv_w, conv_b, ra_w, ra_b, ix_w, ix_b, lam)
    b = _attention(p, sink, rope[0], rope[1])
    return _out_proj_even(a, b, x, mt, w_out, li)


def _odd_mixer(x, mt, norm_w, w_in, w_gates, li, conv_w, conv_b, gate_b, hnorm_w, w_out):
    wg = jnp.pad(w_gates[li], ((0, 0), (0, LANE - N_GATES)))
    p, gates = _in_proj(x, norm_w, mt, w_in, li, ODD_MAIN, w_extra=wg)
    h = _mlstm(p, gates, conv_w, conv_b, gate_b)
    return _out_proj_odd(h, p, hnorm_w, x, mt, w_out, li)


def _router_logits_kernel(x_ref, nw_ref, sh_ref, sc_ref, w_ref, o_ref):
    t = pl.program_id(1)
    h = _norm_mod_rows(x_ref[0], nw_ref[...], sh_ref[0, 0], sc_ref[0, 0], t * TT)

    def split(a):
        hi = a.astype(BF16)
        return hi, (a - hi.astype(F32)).astype(BF16)

    nt = lambda a, b: lax.dot_general(a, b, (((1,), (1,)), ((), ())), preferred_element_type=F32)
    (w_hi, w_lo), (h_hi, h_lo) = split(w_ref[...]), split(h)
    o_ref[0] = nt(w_hi, h_hi) + (nt(w_hi, h_lo) + nt(w_lo, h_hi))


def _router_logits(x, nw, mt, w_router):
    return pl.pallas_call(
        _router_logits_kernel,
        grid=(BATCH, N_TT),
        in_specs=[
            pl.BlockSpec((1, TT, D), lambda b, t: (b, t, 0)),
            pl.BlockSpec((1, D), lambda b, t: (0, 0)),
            pl.BlockSpec((1, 1, 2, D), lambda b, t: (3, b, 0, 0)),
            pl.BlockSpec((1, 1, 2, D), lambda b, t: (4, b, 0, 0)),
            pl.BlockSpec((N_EXPERTS, D), lambda b, t: (0, 0)),
        ],
        out_specs=pl.BlockSpec((1, N_EXPERTS, TT), lambda b, t: (b, 0, t)),
        out_shape=jax.ShapeDtypeStruct((BATCH, N_EXPERTS, S), F32),
        compiler_params=_cparams(("arbitrary", "arbitrary")),
        name="router_logits",
    )(x, nw.reshape(1, D), mt, mt, w_router.T)


def _cumsum_lanes(x):
    n = x.shape[1]
    lane = lax.broadcasted_iota(I32, x.shape, 1)
    k = 1
    while k < n:
        x = x + jnp.where(lane >= k, pltpu.roll(x, k, axis=1), 0.0)
        k *= 2
    return x


def _cumsum_rows_excl(x):
    n = x.shape[0]
    row = lax.broadcasted_iota(I32, x.shape, 0)
    inc = x
    k = 1
    while k < n:
        inc = inc + jnp.where(row >= k, pltpu.roll(inc, k, axis=0), 0.0)
        k *= 2
    return inc - x


def _split3(x):
    hi = x.astype(BF16).astype(F32)
    r = x - hi
    mid = r.astype(BF16).astype(F32)
    return hi, mid, (r - mid).astype(BF16).astype(F32)


def _router_select_kernel(lg_ref, idxl_ref, gl_ref, dstl_ref, idxc_ref, gc_ref, dstc_ref, base_ref, vt_ref):
    ne = N_EXPERTS
    lg = lg_ref[0]
    ex = jnp.exp(lg - jnp.max(lg, axis=0, keepdims=True))
    aff = ex / jnp.sum(ex, axis=0, keepdims=True)
    bits = pltpu.bitcast(aff, I32)
    lane = lax.broadcasted_iota(I32, (ne, S), 1)
    sel = jnp.zeros((ne, S), F32)
    pos = jnp.zeros((ne, S), F32)
    for lo, hi, cap in ((0, CTX, CAP_CTX), (CTX, S, CAP_LAT)):
        vb = jnp.where((lane >= lo) & (lane < hi), bits, -1)
        thr = jnp.zeros((ne, 1), I32)
        for bit in range(30, -1, -1):
            cand = thr | (1 << bit)
            cnt = jnp.sum(jnp.where(vb >= cand, 1.0, 0.0), axis=1, keepdims=True)
            thr = jnp.where(cnt >= cap, cand, thr)
        gt = vb > thr
        eq = jnp.where(vb == thr, 1.0, 0.0)
        need = cap - jnp.sum(jnp.where(gt, 1.0, 0.0), axis=1, keepdims=True)
        eq_rank = _cumsum_lanes(eq) - eq
        s_seg = jnp.where(gt | ((eq > 0.0) & (eq_rank < need)), 1.0, 0.0)
        sel = sel + s_seg
        pos = pos + s_seg * (_cumsum_lanes(s_seg) - s_seg)
    cnt_tok = jnp.broadcast_to(jnp.sum(sel, axis=0, keepdims=True), (ne, S))
    base = _cumsum_lanes(cnt_tok) - cnt_tok
    dest = base + _cumsum_rows_excl(sel)
    base_ref[0] = jnp.concatenate([base[0:1], base[0:1] + cnt_tok[0:1], jnp.zeros((SUBLANE - 2, S), F32)],
                                  axis=0).astype(I32)
    posm = jnp.where(sel > 0.0, pos, -1.0)
    lane_f = lane[0:1].astype(F32)
    idx_hi = jnp.floor(lane_f * (1.0 / 64.0))
    idx_lo = lane_f - 64.0 * idx_hi
    dst_hi = jnp.floor(dest * (1.0 / 128.0))
    dst_lo = dest - 128.0 * dst_hi
    g_hi, g_mid, g_lo = _split3(aff)
    for e in range(ne):
        vt_ref[e] = jnp.concatenate([idx_hi, idx_lo, dst_hi[e:e + 1], dst_lo[e:e + 1], g_hi[e:e + 1],
                                     g_mid[e:e + 1], g_lo[e:e + 1], posm[e:e + 1]], axis=0)
    slot = lax.broadcasted_iota(I32, (LANE, 1), 0).astype(F32)

    def compact(vt, prow, s0):
        onehot = jnp.where(prow == slot + s0, 1.0, 0.0).astype(BF16)
        res = lax.dot_general(vt, onehot, (((1,), (1,)), ((), ())), preferred_element_type=F32)
        return (res[0:1] * 64.0 + res[1:2]).astype(I32), res[4:5] + res[5:6] + res[6:7], \
            (res[2:3] * 128.0 + res[3:4]).astype(I32)

    def per_expert(e):
        blk = vt_ref[e]
        prow = blk[SUBLANE - 1:SUBLANE]
        vt = blk.astype(BF16)
        idxc_ref[0, e], gc_ref[0, e], dstc_ref[0, e] = compact(vt[:, :CTX], prow[:, :CTX], 0.0)
        for sc in range(CAP_LAT // LANE):
            cs = slice(sc * LANE, (sc + 1) * LANE)
            idxl_ref[0, e, :, cs], gl_ref[0, e, :, cs], dstl_ref[0, e, :, cs] = compact(
                vt[:, CTX:], prow[:, CTX:], float(sc * LANE))

    def expert_pair(i, _):
        per_expert(2 * i)
        per_expert(2 * i + 1)
        return 0

    lax.fori_loop(0, ne // 2, expert_pair, 0)


def _router_select(logits):
    ne = N_EXPERTS
    out = lambda n, dt: jax.ShapeDtypeStruct((BATCH, ne, 1, n), dt)
    ospec = lambda n: pl.BlockSpec((1, ne, 1, n), lambda b: (b, 0, 0, 0))
    res = pl.pallas_call(
        _router_select_kernel,
        grid=(BATCH,),
        in_specs=[pl.BlockSpec((1, ne, S), lambda b: (b, 0, 0))],
        out_specs=[ospec(CAP_LAT), ospec(CAP_LAT), ospec(CAP_LAT), ospec(LANE), ospec(LANE), ospec(LANE),
                   pl.BlockSpec((1, SUBLANE, S), lambda b: (b, 0, 0))],
        out_shape=[out(CAP_LAT, I32), out(CAP_LAT, F32), out(CAP_LAT, I32), out(LANE, I32), out(LANE, F32),
                   out(LANE, I32), jax.ShapeDtypeStruct((BATCH, SUBLANE, S), I32)],
        scratch_shapes=[pltpu.VMEM((ne, SUBLANE, S), F32)],
        compiler_params=_cparams(("arbitrary",)),
        name="router_select",
    )(logits)
    return [r.reshape(BATCH, ne, r.shape[-1]) for r in res[:6]] + [res[6]]


ROW_UNROLL = 16


def _start_rows(n_rows, start_one):
    def body(i, _):
        for u in range(ROW_UNROLL):
            start_one(i * ROW_UNROLL + u)
        return 0

    lax.fori_loop(0, n_rows // ROW_UNROLL, body, 0)


def _ffn_up_kernel(src_ref, x_hbm, nw_ref, sh_ref, sc_ref, wg_ref, wu_ref, o_ref, xs_ref, gbuf_ref, sem):
    j = pl.program_id(1)

    @pl.when(j == 0)
    def _():
        def gather(b):
            slot = b % 2
            _start_rows(RPS, lambda r: pltpu.make_async_copy(
                x_hbm.at[pl.ds(src_ref[0, 0, b * RPS + r], 1)], gbuf_ref.at[slot, pl.ds(r, 1)],
                sem.at[slot]).start())

        gather(0)
        for b in range(BATCH):
            slot = b % 2
            if b + 1 < BATCH:
                gather(b + 1)
            pltpu.make_async_copy(x_hbm.at[pl.ds(0, RPS)], gbuf_ref.at[slot], sem.at[slot]).wait()
            is_ctx = lax.broadcasted_iota(I32, (RPS, 1), 0) < CAP_CTX
            y = _rms(gbuf_ref[slot]) * nw_ref[...]
            y = y * (1.0 + _ctx_select(is_ctx, sc_ref[0, b])) + _ctx_select(is_ctx, sh_ref[0, b])
            xs_ref[b * RPS:(b + 1) * RPS, :] = y.astype(BF16)

    xs = xs_ref[...]
    hg = jnp.dot(xs, wg_ref[0].astype(BF16), preferred_element_type=F32)
    hu = jnp.dot(xs, wu_ref[0].astype(BF16), preferred_element_type=F32)
    o_ref[0] = (_silu(hg) * hu).astype(BF16)


def _ffn_up(src_rows, x2d, nw, mt, w_gate, w_up, li):
    tf = 256
    assert RPS % ROW_UNROLL == 0
    return pl.pallas_call(
        _ffn_up_kernel,
        grid=(N_EXPERTS, D_EXPERT // tf),
        in_specs=[
            pl.BlockSpec((1, 1, R_EXP), lambda e, j: (e, 0, 0), memory_space=pltpu.SMEM),
            pl.BlockSpec(memory_space=pl.ANY),
            pl.BlockSpec((1, D), lambda e, j: (0, 0)),
            pl.BlockSpec((1, BATCH, 2, D), lambda e, j: (3, 0, 0, 0)),
            pl.BlockSpec((1, BATCH, 2, D), lambda e, j: (4, 0, 0, 0)),
            pl.BlockSpec((None, 1, D, tf), lambda e, j: (li, e, 0, j)),
            pl.BlockSpec((None, 1, D, tf), lambda e, j: (li, e, 0, j)),
        ],
        out_specs=pl.BlockSpec((1, R_EXP, tf), lambda e, j: (e, 0, j)),
        out_shape=jax.ShapeDtypeStruct((N_EXPERTS, R_EXP, D_EXPERT), BF16),
        scratch_shapes=[pltpu.VMEM((R_EXP, D), BF16), pltpu.VMEM((2, RPS, D), F32),
                        pltpu.SemaphoreType.DMA((2,))],
        compiler_params=_cparams(("arbitrary", "arbitrary")),
        name="ffn_up",
    )(src_rows.reshape(N_EXPERTS, 1, R_EXP), x2d, nw.reshape(1, D), mt, mt, w_gate, w_up)


def _ffn_down_kernel(dst_ref, hid_ref, w_ref, g_ref, y_hbm, ybuf_ref, sem):
    i = pl.program_id(1)
    step = pl.program_id(0) * BATCH + i
    slot = i % 2
    acc = jnp.dot(hid_ref[0], w_ref[0].astype(BF16), preferred_element_type=F32)
    ybuf_ref[slot] = acc * g_ref[0]

    def drain(s):
        pltpu.make_async_copy(ybuf_ref.at[s], y_hbm.at[pl.ds(0, RPS)], sem.at[s]).wait()

    @pl.when(step > 0)
    def _():
        drain(1 - slot)

    _start_rows(RPS, lambda r: pltpu.make_async_copy(
        ybuf_ref.at[slot, pl.ds(r, 1)], y_hbm.at[pl.ds(dst_ref[0, 0, i * RPS + r], 1)], sem.at[slot]).start())

    @pl.when(step == N_EXPERTS * BATCH - 1)
    def _():
        drain(slot)


def _ffn_down(dst_rows, hid, w_down, g_col, li):
    assert BATCH % 2 == 0
    return pl.pallas_call(
        _ffn_down_kernel,
        grid=(N_EXPERTS, BATCH),
        in_specs=[
            pl.BlockSpec((1, 1, R_EXP), lambda e, i: (e, 0, 0), memory_space=pltpu.SMEM),
            pl.BlockSpec((1, RPS, D_EXPERT), lambda e, i: (e, i, 0)),
            pl.BlockSpec((None, 1, D_EXPERT, D), lambda e, i: (li, e, 0, 0)),
            pl.BlockSpec((1, RPS, 1), lambda e, i: (e, i, 0)),
        ],
        out_specs=pl.BlockSpec(memory_space=pl.ANY),
        out_shape=jax.ShapeDtypeStruct((BATCH * PAIRS, D), F32),
        scratch_shapes=[pltpu.VMEM((2, RPS, D), F32), pltpu.SemaphoreType.DMA((2,))],
        compiler_params=_cparams(("arbitrary", "arbitrary")),
        name="ffn_down",
    )(dst_rows.reshape(N_EXPERTS, 1, R_EXP), hid, w_down, g_col)


def _combine_kernel(cs_ref, ce_ref, ys_hbm, x_ref, base_ref, basen_ref, g_ref, *rest):
    fw_ref = rest[0] if len(rest) == 5 else None
    o_ref, buf_ref, acc_ref, sem = rest[-4:]
    b = pl.program_id(0)
    t = pl.program_id(1)
    lo = cs_ref[b, t]
    hi = ce_ref[b, t]

    def chunk_copy(bb, c):
        return pltpu.make_async_copy(ys_hbm.at[pl.ds(bb * PAIRS + c * TT, TT)], buf_ref.at[c % 2], sem.at[c % 2])

    @pl.when((b == 0) & (t == 0) & (lo < hi))
    def _():
        chunk_copy(b, lo).start()

    acc_ref[...] = jnp.zeros((TT, D), F32)
    base = base_ref[0]
    basen = basen_ref[0]
    lane = lax.broadcasted_iota(I32, (1, TT), 1)

    def body(c, _):
        @pl.when(c + 1 < hi)
        def _():
            chunk_copy(b, c + 1).start()

        chunk_copy(b, c).wait()
        r = c * TT + lane
        onehot = jnp.where((base <= r) & (r < basen), 1.0, 0.0).astype(BF16)
        y = buf_ref[c % 2]
        y_hi = y.astype(BF16)
        y_lo = (y - y_hi.astype(F32)).astype(BF16)
        acc_ref[...] += (jnp.dot(onehot, y_hi, preferred_element_type=F32)
                         + jnp.dot(onehot, y_lo, preferred_element_type=F32))
        return 0

    lax.fori_loop(lo, hi, body, 0)

    wrap = t + 1 == N_TT
    nb = jnp.where(wrap, b + 1, b)
    nt = jnp.where(wrap, 0, t + 1)

    @pl.when(nb < BATCH)
    def _():
        nlo = cs_ref[nb, nt]

        @pl.when(nlo < ce_ref[nb, nt])
        def _():
            chunk_copy(nb, nlo).start()

    gate = jnp.where(t == 0, g_ref[0, 0, 0:1, :], g_ref[0, 0, 1:2, :])
    y = x_ref[0] + gate * acc_ref[...]
    o_ref[0] = y if fw_ref is None else _rms(y) * fw_ref[...]


def _combine(cs, ce, ys, x, base_col, basen_col, mt, final_w=None):
    in_specs = [
        pl.BlockSpec(memory_space=pl.ANY),
        pl.BlockSpec((1, TT, D), lambda b, t, *_: (b, t, 0)),
        pl.BlockSpec((1, TT, 1), lambda b, t, *_: (b, t, 0)),
        pl.BlockSpec((1, TT, 1), lambda b, t, *_: (b, t, 0)),
        pl.BlockSpec((1, 1, 2, D), lambda b, t, *_: (5, b, 0, 0)),
    ]
    args = [cs, ce, ys, x, base_col, basen_col, mt]
    if final_w is None:
        out_spec = pl.BlockSpec((1, TT, D), lambda b, t, *_: (b, t, 0))
        out_rows = S
    else:
        assert CTX == TT
        in_specs.append(pl.BlockSpec((1, D), lambda b, t, *_: (0, 0)))
        args.append(final_w.reshape(1, D))
        out_spec = pl.BlockSpec((1, TT, D), lambda b, t, *_: (b, jnp.maximum(t - 1, 0), 0))
        out_rows = SEQ
    grid_spec = pltpu.PrefetchScalarGridSpec(
        num_scalar_prefetch=2,
        grid=(BATCH, N_TT),
        in_specs=in_specs,
        out_specs=out_spec,
        scratch_shapes=[pltpu.VMEM((2, TT, D), F32), pltpu.VMEM((TT, D), F32), pltpu.SemaphoreType.DMA((2,))],
    )
    return pl.pallas_call(
        _combine_kernel,
        grid_spec=grid_spec,
        out_shape=jax.ShapeDtypeStruct((BATCH, out_rows, D), F32),
        compiler_params=_cparams(("arbitrary", "arbitrary")),
        name="moe_combine",
    )(*args)


def _moe(x, mt, norm_w, w_router, w_gate, w_up, w_down, li, final_w=None):
    logits = _router_logits(x, norm_w, mt, w_router)
    idx_l, g_l, dst_l, idx_c, g_c, dst_c, bases = _router_select(logits)
    boff = jnp.arange(BATCH, dtype=I32)[:, None, None]

    def rows(c, l, off):
        r = jnp.concatenate([c[:, :, :CAP_CTX], l], axis=2) + off
        return jnp.transpose(r, (1, 0, 2)).reshape(N_EXPERTS, R_EXP)

    src_rows = rows(idx_c, idx_l, boff * S)
    dst_rows = rows(dst_c, dst_l, boff * PAIRS)
    g_col = rows(g_c, g_l, 0.0).reshape(N_EXPERTS, R_EXP, 1)
    hid = _ffn_up(src_rows, x.reshape(BATCH * S, D), norm_w, mt, w_gate, w_up, li)
    ys = _ffn_down(dst_rows, hid, w_down, g_col, li)
    base, basen = bases[:, 0, :], bases[:, 1, :]
    cs = base[:, ::TT] // TT
    ce = (basen[:, TT - 1::TT] + TT - 1) // TT
    return _combine(cs, ce, ys, x, base.reshape(BATCH, S, 1), basen.reshape(BATCH, S, 1), mt, final_w)


def _pack_kernel(ctx_ref, x_ref, o_ref):
    @pl.when(pl.program_id(1) == 0)
    def _():
        o_ref[...] = ctx_ref[...]

    @pl.when(pl.program_id(1) > 0)
    def _():
        o_ref[...] = x_ref[...]


def _pack_rows(ctx, x):
    assert CTX == TT
    return pl.pallas_call(
        _pack_kernel,
        grid=(BATCH, N_TT),
        in_specs=[
            pl.BlockSpec((1, TT, D), lambda b, t: (b, 0, 0)),
            pl.BlockSpec((1, TT, D), lambda b, t: (b, jnp.maximum(t - 1, 0), 0)),
        ],
        out_specs=pl.BlockSpec((1, TT, D), lambda b, t: (b, t, 0)),
        out_shape=jax.ShapeDtypeStruct((BATCH, S, D), F32),
        compiler_params=_cparams(("arbitrary", "arbitrary")),
        name="pack_rows",
    )(ctx, x)


def kernel(x, c, ctx, c_ctx, ada_w, ada_b, norm_mix_w, norm_ffn_w,
           ev_w_in, ev_conv_w, ev_conv_b, ev_ra_w, ev_ra_b, ev_ix_w, ev_ix_b, ev_lambda, ev_sink, ev_w_out,
           od_w_in, od_conv_w, od_conv_b, od_gate_b, od_hnorm_w, od_w_out,
           moe_router, moe_w_gate, moe_w_up, moe_w_down, final_norm_w):
    assert x.shape == (BATCH, SEQ, D) and ctx.shape == (BATCH, CTX, D)
    cc = jnp.zeros((SUBLANE, D), F32).at[:BATCH].set(c).at[BATCH].set(c_ctx)
    mod = _modulation(cc, ada_w, ada_b)
    rope = _rope_tables()
    od_w_gates = od_w_in[:, :, ODD_MAIN:]
    ev_w_in, ev_w_out, od_w_in, od_w_out = (w.astype(BF16) for w in (
        ev_w_in, ev_w_out, od_w_in[:, :, :ODD_MAIN], od_w_out))
    xs = _pack_rows(ctx, x)
    for layer in range(DEPTH):
        mt = _mod_table(mod[layer])
        i = layer // 2
        if layer % 2 == 0:
            xs = _even_mixer(xs, mt, norm_mix_w[layer], ev_w_in, i, ev_conv_w[i], ev_conv_b[i], ev_ra_w[i],
                             ev_ra_b[i], ev_ix_w[i], ev_ix_b[i], ev_lambda[i], ev_sink[i], ev_w_out, rope)
        else:
            xs = _odd_mixer(xs, mt, norm_mix_w[layer], od_w_in, od_w_gates, i, od_conv_w[i], od_conv_b[i],
                            od_gate_b[i], od_hnorm_w[i], od_w_out)
        xs = _moe(xs, mt, norm_ffn_w[layer], moe_router[layer], moe_w_gate, moe_w_up, moe_w_down, layer,
                  final_norm_w if layer == DEPTH - 1 else None)
    return xs
```
